```python
import math
import jax, jax.numpy as jnp
from jax import lax
import numpy as np

D_MODEL = 1024
BATCH = 2
SEQ = 8192
DEPTH = 2

HEAD_DIM = 64
A_HEADS = 4
A_QK = 2 * HEAD_DIM
A_V = 2 * HEAD_DIM
A_WIDTH = A_HEADS * A_V
B_HEADS = 8
B_DIM = HEAD_DIM
B_WIDTH = B_HEADS * B_DIM
DILATED_BRANCHES = ((128, 1), (512, 4), (2048, 16))
ATTN_BLOCK = 128
EVEN_IN = 2 * A_HEADS * A_QK + A_WIDTH + 3 * B_WIDTH
C_HEADS = 4
C_DK = D_MODEL // 2 // C_HEADS
C_DV = D_MODEL // C_HEADS
C_GATE_RANK = 16
C_TAU = 16.0
C_CHUNK = 64
ODD_IN = 2 * C_HEADS * C_DK + 2 * C_HEADS * C_DV + C_GATE_RANK
FFN_HIDDEN = -(-8 * D_MODEL // (3 * 256)) * 256
N_EVEN = (DEPTH + 1) // 2
N_ODD = DEPTH // 2
EPS = 1e-6

kernel_name = "hybrid_diffattn_dilated_gla_block"

f32 = jnp.float32


def rmsnorm(x, g):
    x32 = x.astype(f32)
    y = x32 * lax.rsqrt(jnp.mean(x32 * x32, axis=-1, keepdims=True) + EPS) * g.astype(f32)
    return y.astype(x.dtype)


def alibi_slopes(n):
    return 2.0 ** (-8.0 * jnp.arange(1, n + 1, dtype=f32) / n)


def diff_attention(q, k, v, lam, lam_init, subln_g):
    Bn, S = q.shape[0], q.shape[1]
    nb = S // ATTN_BLOCK
    qf = (q.astype(f32) * HEAD_DIM ** -0.5).reshape(Bn, nb, ATTN_BLOCK, A_HEADS, 2, HEAD_DIM)
    qf = qf.transpose(1, 0, 3, 4, 2, 5)
    kf = k.astype(f32).transpose(0, 2, 3, 1, 4)
    vf = v.astype(f32).transpose(0, 2, 1, 3)
    slopes = alibi_slopes(A_HEADS)
    kpos = jnp.arange(S)

    def block(args):
        qb, n = args
        qpos = n * ATTN_BLOCK + jnp.arange(ATTN_BLOCK)
        dist = qpos[:, None] - kpos[None, :]
        causal = dist >= 0
        s = jnp.einsum('bhcqd,bhckd->bhcqk', qb, kf)
        s = jnp.where(causal, s - (slopes[:, None, None] * dist.astype(f32))[None, :, None], -jnp.inf)
        p = jax.nn.softmax(s, axis=-1)
        w = p[:, :, 0] - lam * p[:, :, 1]
        return jnp.einsum('bhqk,bhkv->bhqv', w, vf)

    o = lax.map(block, (qf, jnp.arange(nb)))
    o = rmsnorm(o, subln_g) * (1.0 - lam_init)
    return o.transpose(1, 0, 3, 2, 4).reshape(Bn, S, A_WIDTH)


def dilated_branch(q, k, v, window, dilation, slopes):
    Bn, S, H, dh = q.shape
    L = S // dilation
    steps = window // dilation
    nb = -(-L // ATTN_BLOCK)
    Lp = nb * ATTN_BLOCK

    def streams(t):
        t = t.reshape(Bn, L, dilation, H, dh).transpose(0, 2, 1, 3, 4)
        return jnp.pad(t, ((0, 0), (0, 0), (0, Lp - L), (0, 0), (0, 0)))

    def banded(t):
        tb = jnp.pad(t, ((0, 0), (0, 0), (ATTN_BLOCK, 0), (0, 0), (0, 0)))
        tb = tb.reshape(Bn, dilation, nb + 1, ATTN_BLOCK, H, dh)
        return jnp.concatenate([tb[:, :, :-1], tb[:, :, 1:]], axis=3)

    qb = streams(q).reshape(Bn, dilation, nb, ATTN_BLOCK, H, dh)
    kb = banded(streams(k))
    vb = banded(streams(v))
    a = jnp.arange(ATTN_BLOCK)
    bi = jnp.arange(2 * ATTN_BLOCK)
    rel = ATTN_BLOCK + a[:, None] - bi[None, :]
    kidx = (jnp.arange(nb) * ATTN_BLOCK)[:, None] - ATTN_BLOCK + bi[None, :]
    valid = ((rel >= 0) & (rel <= steps))[None] & (kidx >= 0)[:, None, :]
    dist = (rel * dilation).astype(f32)
    s = jnp.einsum('brnqhd,brnkhd->brnhqk', qb, kb)
    s = jnp.where(valid[None, None, :, None], s - slopes[:, None, None] * dist, -jnp.inf)
    lse = jax.nn.logsumexp(s, axis=-1)
    p = jnp.exp(s - lse[..., None])
    o = jnp.einsum('brnhqk,brnkhd->brnqhd', p, vb)
    o = o.reshape(Bn, dilation, Lp, H, dh)[:, :, :L].transpose(0, 2, 1, 3, 4).reshape(Bn, S, H, dh)
    lse = lse.transpose(0, 1, 2, 4, 3).reshape(Bn, dilation, Lp, H)[:, :, :L]
    lse = lse.transpose(0, 2, 1, 3).reshape(Bn, S, H)
    return o, lse


def dilated_attention(q, k, v):
    qf = q.astype(f32) * B_DIM ** -0.5
    kf = k.astype(f32)
    vf = v.astype(f32)
    slopes = alibi_slopes(B_HEADS)
    outs, lses = [], []
    for window, dilation in DILATED_BRANCHES:
        o, l = dilated_branch(qf, kf, vf, window, dilation, slopes)
        outs.append(o)
        lses.append(l)
    wts = jax.nn.softmax(jnp.stack(lses, 0), axis=0)
    o = jnp.einsum('nbsh,nbshd->bshd', wts, jnp.stack(outs, 0))
    return o.reshape(q.shape[0], q.shape[1], B_WIDTH)


def even_mixer(h, w_in, lam_p, subln_g, w_out, layer_idx):
    Bn, S, _ = h.shape
    proj = h @ w_in
    o1 = A_HEADS * A_QK
    o2 = 2 * o1
    o3 = o2 + A_WIDTH
    qa = proj[..., :o1].reshape(Bn, S, A_HEADS, 2, HEAD_DIM)
    ka = proj[..., o1:o2].reshape(Bn, S, A_HEADS, 2, HEAD_DIM)
    va = proj[..., o2:o3].reshape(Bn, S, A_HEADS, A_V)
    qb = proj[..., o3:o3 + B_WIDTH].reshape(Bn, S, B_HEADS, B_DIM)
    kb = proj[..., o3 + B_WIDTH:o3 + 2 * B_WIDTH].reshape(Bn, S, B_HEADS, B_DIM)
    vb = proj[..., o3 + 2 * B_WIDTH:].reshape(Bn, S, B_HEADS, B_DIM)
    lam_init = 0.8 - 0.6 * math.exp(-0.3 * layer_idx)
    lp = lam_p.astype(f32)
    lam = jnp.exp(jnp.sum(lp[0] * lp[1])) - jnp.exp(jnp.sum(lp[2] * lp[3])) + lam_init
    ya = diff_attention(qa, ka, va, lam, lam_init, subln_g)
    yb = dilated_attention(qb, kb, vb)
    y = jnp.concatenate([ya, yb], axis=-1).astype(h.dtype)
    return y @ w_out


def gla(q, k, v, log_a):
    Bn, S, H, dk = q.shape
    dv = v.shape[-1]
    N = S // C_CHUNK

    def chunks(t):
        return t.reshape(Bn, N, C_CHUNK, H, t.shape[-1]).transpose(1, 0, 3, 2, 4)

    causal = jnp.tril(jnp.ones((C_CHUNK, C_CHUNK), bool))[:, :, None]

    def step(state, inp):
        qc, kc, vc, gc = inp
        b = jnp.cumsum(gc, axis=-2)
        diff = b[:, :, :, None, :] - b[:, :, None, :, :]
        decay = jnp.where(causal, jnp.exp(jnp.where(causal, diff, 0.0)), 0.0)
        attn = jnp.einsum('bhijk,bhjk->bhij', qc[:, :, :, None, :] * decay, kc)
        o = jnp.einsum('bhij,bhjv->bhiv', attn, vc) + jnp.einsum('bhik,bhkv->bhiv', qc * jnp.exp(b), state)
        b_last = b[:, :, -1:, :]
        state = jnp.exp(b_last[:, :, 0, :, None]) * state + jnp.einsum('bhjk,bhjv->bhkv', kc * jnp.exp(b_last - b), vc)
        return state, o

    s0 = jnp.zeros((Bn, H, dk, dv), f32)
    _, o = lax.scan(step, s0, (chunks(q), chunks(k), chunks(v), chunks(log_a)))
    return o.transpose(1, 0, 3, 2, 4).reshape(Bn, S, H, dv)


def odd_mixer(h, w_in, w_g2, b_g2, head_g, w_out):
    Bn, S, _ = h.shape
    proj = (h @ w_in).astype(f32)
    nk = C_HEADS * C_DK
    nv = C_HEADS * C_DV
    q = proj[..., :nk].reshape(Bn, S, C_HEADS, C_DK) * C_DK ** -0.5
    k = proj[..., nk:2 * nk].reshape(Bn, S, C_HEADS, C_DK)
    v = proj[..., 2 * nk:2 * nk + nv].reshape(Bn, S, C_HEADS, C_DV)
    r = proj[..., 2 * nk + nv:2 * nk + 2 * nv]
    glr = proj[..., 2 * nk + 2 * nv:]
    log_a = jax.nn.log_sigmoid(glr @ w_g2.astype(f32) + b_g2.astype(f32)) / C_TAU
    log_a = log_a.reshape(Bn, S, C_HEADS, C_DK)
    o = gla(q, k, v, log_a)
    o = rmsnorm(o, head_g).reshape(Bn, S, nv) * jax.nn.silu(r)
    return o.astype(h.dtype) @ w_out


def swiglu(h, w_gate, w_up, w_down):
    return (jax.nn.silu(h @ w_gate) * (h @ w_up)) @ w_down


def setup_inputs(seed: int = 0) -> dict:
    key = jax.random.key(seed)
    ks = jax.random.split(key, 20)

    def w(k, shape, fan_in, gain=1.0):
        return jax.random.normal(k, shape, f32) * (gain * fan_in ** -0.5)

    return {
        "x": jax.random.normal(ks[0], (BATCH, SEQ, D_MODEL), f32),
        "c": jax.random.normal(ks[1], (BATCH, D_MODEL), f32),
        "ada_w": w(ks[2], (DEPTH, D_MODEL, 6 * D_MODEL), D_MODEL, 0.5),
        "ada_b": 0.01 * jax.random.normal(ks[3], (DEPTH, 6 * D_MODEL), f32),
        "norm_g": 1.0 + 0.02 * jax.random.normal(ks[4], (DEPTH, 4, D_MODEL), f32),
        "ev_w_in": w(ks[5], (N_EVEN, D_MODEL, EVEN_IN), D_MODEL),
        "ev_lambda": 0.1 * jax.random.normal(ks[6], (N_EVEN, 4, HEAD_DIM), f32),
        "ev_subln_g": 1.0 + 0.02 * jax.random.normal(ks[7], (N_EVEN, A_V), f32),
        "ev_w_out": w(ks[8], (N_EVEN, A_WIDTH + B_WIDTH, D_MODEL), A_WIDTH + B_WIDTH),
        "od_w_in": w(ks[9], (N_ODD, D_MODEL, ODD_IN), D_MODEL),
        "od_w_g2": w(ks[10], (N_ODD, C_GATE_RANK, C_HEADS * C_DK), C_GATE_RANK),
        "od_b_g2": 0.1 * jax.random.normal(ks[11], (N_ODD, C_HEADS * C_DK), f32),
        "od_head_g": 1.0 + 0.02 * jax.random.normal(ks[12], (N_ODD, C_DV), f32),
        "od_w_out": w(ks[13], (N_ODD, C_HEADS * C_DV, D_MODEL), C_HEADS * C_DV),
        "ffn_w_gate": w(ks[14], (DEPTH, D_MODEL, FFN_HIDDEN), D_MODEL),
        "ffn_w_up": w(ks[15], (DEPTH, D_MODEL, FFN_HIDDEN), D_MODEL),
        "ffn_w_down": w(ks[16], (DEPTH, FFN_HIDDEN, D_MODEL), FFN_HIDDEN),
    }


def reference(x, c, ada_w, ada_b, norm_g, ev_w_in, ev_lambda, ev_subln_g, ev_w_out,
              od_w_in, od_w_g2, od_b_g2, od_head_g, od_w_out, ffn_w_gate, ffn_w_up, ffn_w_down):
    for l in range(DEPTH):
        mod = (jax.nn.silu(c) @ ada_w[l] + ada_b[l])[:, None, :]
        sh1, sc1, g1, sh2, sc2, g2 = jnp.split(mod, 6, axis=-1)
        h = rmsnorm(x, norm_g[l, 0]) * (1 + sc1) + sh1
        if l % 2 == 0:
            e = l // 2
            y = even_mixer(h, ev_w_in[e], ev_lambda[e], ev_subln_g[e], ev_w_out[e], l)
        else:
            o = l // 2
            y = odd_mixer(h, od_w_in[o], od_w_g2[o], od_b_g2[o], od_head_g[o], od_w_out[o])
        x = x + (1 + g1) * rmsnorm(y, norm_g[l, 1])
        h = rmsnorm(x, norm_g[l, 2]) * (1 + sc2) + sh2
        y = swiglu(h, ffn_w_gate[l], ffn_w_up[l], ffn_w_down[l])
        x = x + (1 + g2) * rmsnorm(y, norm_g[l, 3])
    return x
```

```python
import functools
import math

import jax
import jax.numpy as jnp
from jax import lax
from jax.experimental import pallas as pl
from jax.experimental.pallas import tpu as pltpu

f32 = jnp.float32
bf16 = jnp.bfloat16

EPS = 1e-6
LANES = 128
HEAD_DIM = 64
A_HEADS = 4
B_HEADS = 8
DILATIONS = (1, 4, 16)
BAND = 128
C_HEADS = 4
C_DK = 128
C_DV = 256
C_GATE_RANK = 16
C_TAU = 16.0
VMEM_LIMIT = 56 * 1024 * 1024

NT_DIMS = (((1,), (1,)), ((), ()))
TN_DIMS = (((0,), (0,)), ((), ()))


def _rms(x, g):
    return x * lax.rsqrt(jnp.mean(x * x, axis=-1, keepdims=True) + EPS) * g


def _params(*sem):
    return pltpu.CompilerParams(dimension_semantics=sem, vmem_limit_bytes=VMEM_LIMIT)


def _resident(shape):
    nd = len(shape)
    return pl.BlockSpec(shape, lambda *_: (0,) * nd, pipeline_mode=pl.Buffered(1))


def _adaln_kernel(c_ref, w_ref, b_ref, o_ref):
    c = c_ref[...]
    a = (c * jax.nn.sigmoid(c)).astype(bf16)
    o_ref[0] = jnp.dot(a, w_ref[0].astype(bf16), preferred_element_type=f32) + b_ref[0]


def _adaln(c, ada_w, ada_b):
    depth, d, n = ada_w.shape
    bsz = c.shape[0]
    rows = 8
    cp = jnp.zeros((rows, d), f32).at[:bsz].set(c)
    tn = 768
    out = pl.pallas_call(
        _adaln_kernel,
        grid=(depth, n // tn),
        in_specs=[
            pl.BlockSpec((rows, d), lambda l, j: (0, 0)),
            pl.BlockSpec((1, d, tn), lambda l, j: (l, 0, j)),
            pl.BlockSpec((1, 1, tn), lambda l, j: (l, 0, j)),
        ],
        out_specs=pl.BlockSpec((1, rows, tn), lambda l, j: (l, 0, j)),
        out_shape=jax.ShapeDtypeStruct((depth, rows, n), f32),
        compiler_params=_params("parallel", "parallel"),
        name="adaln",
    )(cp, ada_w, ada_b.reshape(depth, 1, n))
    return out[:, :bsz].reshape(depth, bsz, 6, 1, d)


def _modulated(x_ref, g_ref, sc_ref, sh_ref):
    h = _rms(x_ref[0], g_ref[...])
    return (h * (1.0 + sc_ref[0]) + sh_ref[0]).astype(bf16)


def _inproj_even_kernel(x_ref, g_ref, sc_ref, sh_ref, w_ref, oa_ref, ob_ref, *, nc):
    hb = _modulated(x_ref, g_ref, sc_ref, sh_ref)
    na = oa_ref.shape[-1]
    for c in range(0, na, nc):
        oa_ref[0, :, c:c + nc] = jnp.dot(hb, w_ref[:, c:c + nc], preferred_element_type=f32).astype(bf16)
    for c in range(0, ob_ref.shape[-1], nc):
        ob_ref[0, :, c:c + nc] = jnp.dot(hb, w_ref[:, na + c:na + c + nc], preferred_element_type=f32)


def _inproj_even(x, g, sc, sh, w, tm=512):
    bsz, s, d = x.shape
    na = 3 * A_HEADS * 2 * HEAD_DIM
    nb = w.shape[1] - na
    row = lambda b, i: (b, i, 0)
    per_b = lambda b, i: (b, 0, 0)
    return pl.pallas_call(
        functools.partial(_inproj_even_kernel, nc=512),
        grid=(bsz, s // tm),
        in_specs=[
            pl.BlockSpec((1, tm, d), row),
            pl.BlockSpec((1, d), lambda b, i: (0, 0)),
            pl.BlockSpec((1, 1, d), per_b),
            pl.BlockSpec((1, 1, d), per_b),
            _resident(w.shape),
        ],
        out_specs=[pl.BlockSpec((1, tm, na), row), pl.BlockSpec((1, tm, nb), row)],
        out_shape=[jax.ShapeDtypeStruct((bsz, s, na), bf16), jax.ShapeDtypeStruct((bsz, s, nb), f32)],
        compiler_params=_params("parallel", "parallel"),
        name="inproj_even",
    )(x, g, sc, sh, w)


def _inproj_odd_kernel(x_ref, g_ref, sc_ref, sh_ref, w_ref, wg1_ref, wg2_ref, bg2_ref,
                       oqk_ref, ovr_ref, ola_ref, *, nc):
    hb = _modulated(x_ref, g_ref, sc_ref, sh_ref)
    nqk = oqk_ref.shape[-1]
    for c in range(0, nqk, nc):
        oqk_ref[0, :, c:c + nc] = jnp.dot(hb, w_ref[:, c:c + nc], preferred_element_type=f32).astype(bf16)
    for c in range(0, ovr_ref.shape[-1], nc):
        ovr_ref[0, :, c:c + nc] = jnp.dot(hb, w_ref[:, nqk + c:nqk + c + nc], preferred_element_type=f32).astype(bf16)
    glr = jnp.dot(hb, wg1_ref[...], preferred_element_type=f32)
    z = jnp.dot(glr.astype(bf16), wg2_ref[...], preferred_element_type=f32) + bg2_ref[...]
    ola_ref[0] = (jnp.minimum(z, 0.0) - jnp.log(1.0 + jnp.exp(-jnp.abs(z)))) * (1.0 / C_TAU)


def _inproj_odd(x, g, sc, sh, w, wg1, wg2, bg2, tm=512):
    bsz, s, d = x.shape
    nqk = 2 * C_HEADS * C_DK
    nvr = 2 * C_HEADS * C_DV
    nla = C_HEADS * C_DK
    row = lambda b, i: (b, i, 0)
    per_b = lambda b, i: (b, 0, 0)
    return pl.pallas_call(
        functools.partial(_inproj_odd_kernel, nc=512),
        grid=(bsz, s // tm),
        in_specs=[
            pl.BlockSpec((1, tm, d), row),
            pl.BlockSpec((1, d), lambda b, i: (0, 0)),
            pl.BlockSpec((1, 1, d), per_b),
            pl.BlockSpec((1, 1, d), per_b),
            _resident(w.shape),
            _resident(wg1.shape),
            _resident(wg2.shape),
            _resident(bg2.shape),
        ],
        out_specs=[pl.BlockSpec((1, tm, nqk), row), pl.BlockSpec((1, tm, nvr), row), pl.BlockSpec((1, tm, nla), row)],
        out_shape=[jax.ShapeDtypeStruct((bsz, s, nqk), bf16), jax.ShapeDtypeStruct((bsz, s, nvr), bf16),
                   jax.ShapeDtypeStruct((bsz, s, nla), f32)],
        compiler_params=_params("parallel", "parallel"),
        name="inproj_odd",
    )(x, g, sc, sh, w, wg1, wg2, bg2)


def _diff_attn_kernel(slope_ref, lam_ref, q_ref, k_ref, v_ref, g_ref, o_ref, m_sc, l_sc, acc_sc,
                      *, t, lam_init):
    h = pl.program_id(1)
    i = pl.program_id(2)
    slope = slope_ref[h]
    lane = lax.broadcasted_iota(jnp.int32, (t, LANES), 1)
    first_half = lane < HEAD_DIM
    q = q_ref[0] * (HEAD_DIM ** -0.5)
    zero = jnp.zeros_like(q)
    qs = (jnp.where(first_half, q, zero), jnp.where(first_half, zero, q))
    row = lax.broadcasted_iota(jnp.int32, (t, t), 0)
    col = lax.broadcasted_iota(jnp.int32, (t, t), 1)
    bias = slope * (row - col).astype(f32)
    bias_diag = jnp.where(row >= col, bias, jnp.inf)

    def scores(j_start, b):
        kb = k_ref[0, pl.ds(j_start, t), :]
        return [lax.dot_general(qs[c], kb, NT_DIMS, preferred_element_type=f32) - b for c in range(2)]

    start = pl.multiple_of(i * t, t)
    vb = v_ref[0, pl.ds(start, t), :]
    for c, s in enumerate(scores(start, bias_diag)):
        m = jnp.max(s, axis=-1, keepdims=True)
        p = jnp.exp(s - m)
        m_sc[c] = m
        l_sc[c] = jnp.sum(p, axis=-1, keepdims=True)
        acc_sc[c] = jnp.dot(p.astype(bf16), vb, preferred_element_type=f32)

    def body(j, carry):
        j_start = pl.multiple_of(j * t, t)
        vb = v_ref[0, pl.ds(j_start, t), :]
        shift = slope * ((i - j) * t).astype(f32)
        for c, s in enumerate(scores(j_start, bias)):
            m_prev = m_sc[c]
            m_new = jnp.maximum(m_prev, jnp.max(s, axis=-1, keepdims=True) - shift)
            alpha = jnp.exp(m_prev - m_new)
            p = jnp.exp(s - (m_new + shift))
            l_sc[c] = alpha * l_sc[c] + jnp.sum(p, axis=-1, keepdims=True)
            acc_sc[c] = alpha * acc_sc[c] + jnp.dot(p.astype(bf16), vb, preferred_element_type=f32)
            m_sc[c] = m_new
        return carry

    lax.fori_loop(0, i, body, 0)

    lp = lam_ref[...]
    lam = (jnp.exp(jnp.sum(lp[0:1] * lp[1:2], axis=-1, keepdims=True))
           - jnp.exp(jnp.sum(lp[2:3] * lp[3:4], axis=-1, keepdims=True)) + lam_init)
    o = acc_sc[0] / l_sc[0] - lam * (acc_sc[1] / l_sc[1])
    o_ref[0] = (_rms(o, g_ref[...]) * (1.0 - lam_init)).astype(bf16)


def _diff_attn(pa, slopes, lam_p, subln_g, lam_init, t=256):
    bsz, s, _ = pa.shape
    width = 2 * HEAD_DIM
    return pl.pallas_call(
        functools.partial(_diff_attn_kernel, t=t, lam_init=lam_init),
        grid=(bsz, A_HEADS, s // t),
        in_specs=[
            pl.BlockSpec(memory_space=pltpu.SMEM),
            pl.BlockSpec(lam_p.shape, lambda b, h, i: (0, 0)),
            pl.BlockSpec((1, t, width), lambda b, h, i: (b, i, h)),
            pl.BlockSpec((1, s, width), lambda b, h, i: (b, 0, A_HEADS + h)),
            pl.BlockSpec((1, s, width), lambda b, h, i: (b, 0, 2 * A_HEADS + h)),
            pl.BlockSpec((1, width), lambda b, h, i: (0, 0)),
        ],
        out_specs=pl.BlockSpec((1, t, width), lambda b, h, i: (b, i, h)),
        out_shape=jax.ShapeDtypeStruct((bsz, s, A_HEADS * width), bf16),
        scratch_shapes=[pltpu.VMEM((2, t, 1), f32), pltpu.VMEM((2, t, 1), f32), pltpu.VMEM((2, t, width), f32)],
        compiler_params=_params("parallel", "parallel", "parallel"),
        name="diff_attn",
    )(slopes, lam_p, pa, pa, pa, subln_g)


def _dilated_kernel(slope_ref, q_ref, kp_ref, kc_ref, vp_ref, vc_ref, o_ref,
                    kcat, vcat, acc_sc, m_sc, l_sc, *, span):
    hp = pl.program_id(1)
    sidx = pl.program_id(2)
    blk = BAND
    nblocks = span // blk
    kcat[0:span] = kp_ref[0]
    kcat[span:2 * span] = kc_ref[0]
    vcat[0:span] = vp_ref[0]
    vcat[span:2 * span] = vc_ref[0]

    lane = lax.broadcasted_iota(jnp.int32, (blk, LANES), 1)
    head_a = lane < HEAD_DIM
    a = lax.broadcasted_iota(jnp.int32, (blk, 2 * blk), 0)
    bi = lax.broadcasted_iota(jnp.int32, (blk, 2 * blk), 1)
    rel = blk + a - bi
    valid = (rel >= 0) & (rel <= BAND)
    before_start = bi < blk
    relf = rel.astype(f32)
    slopes = (slope_ref[2 * hp], slope_ref[2 * hp + 1])

    for br, d in enumerate(DILATIONS):
        per_stream = nblocks // d
        biases = [jnp.where(valid, (sl * d) * relf, jnp.inf) for sl in slopes]

        def body(it, carry, br=br, d=d, per_stream=per_stream, biases=biases):
            r = it // per_stream
            n = it % per_stream
            q_start = n * (blk * d) + r
            k_start = span + (n - 1) * (blk * d) + r
            qb = (q_ref[0, pl.ds(q_start, blk, stride=d), :] * (HEAD_DIM ** -0.5)).astype(bf16)
            kk = kcat[pl.ds(k_start, 2 * blk, stride=d), :].astype(bf16)
            vv = vcat[pl.ds(k_start, 2 * blk, stride=d), :].astype(bf16)
            zero = jnp.zeros_like(qb)
            qh = (jnp.where(head_a, qb, zero), jnp.where(head_a, zero, qb))
            dead = jnp.logical_and(jnp.logical_and(sidx == 0, n == 0), before_start)
            outs, ms, ls = [], [], []
            for c in range(2):
                s = lax.dot_general(qh[c], kk, NT_DIMS, preferred_element_type=f32) - biases[c]
                s = jnp.where(dead, -jnp.inf, s)
                m = jnp.max(s, axis=-1, keepdims=True)
                p = jnp.exp(s - m)
                ls.append(jnp.sum(p, axis=-1, keepdims=True))
                ms.append(m)
                outs.append(jnp.dot(p.astype(bf16), vv, preferred_element_type=f32))
            dst = pl.ds(q_start, blk, stride=d)
            acc_sc[br, dst, :] = jnp.where(head_a, outs[0], outs[1])
            m_sc[br, dst, :] = jnp.where(head_a, ms[0], ms[1])
            l_sc[br, dst, :] = jnp.where(head_a, ls[0], ls[1])
            return carry

        lax.fori_loop(0, nblocks, body, 0)

    m_all = jnp.maximum(jnp.maximum(m_sc[0], m_sc[1]), m_sc[2])
    num = jnp.zeros((span, LANES), f32)
    den = jnp.zeros((span, LANES), f32)
    for br in range(len(DILATIONS)):
        w = jnp.exp(m_sc[br] - m_all)
        num = num + w * acc_sc[br]
        den = den + w * l_sc[br]
    o_ref[0] = (num / den).astype(bf16)


def _dilated_attn(pb, slopes, span=2048):
    bsz, s, _ = pb.shape
    pairs = B_HEADS // 2
    cur = lambda off: (lambda b, h, i: (b, i, off + h))
    prev = lambda off: (lambda b, h, i: (b, jnp.maximum(i - 1, 0), off + h))
    blk = (1, span, LANES)
    nbr = len(DILATIONS)
    return pl.pallas_call(
        functools.partial(_dilated_kernel, span=span),
        grid=(bsz, pairs, s // span),
        in_specs=[
            pl.BlockSpec(memory_space=pltpu.SMEM),
            pl.BlockSpec(blk, cur(0)),
            pl.BlockSpec(blk, prev(pairs)),
            pl.BlockSpec(blk, cur(pairs)),
            pl.BlockSpec(blk, prev(2 * pairs)),
            pl.BlockSpec(blk, cur(2 * pairs)),
        ],
        out_specs=pl.BlockSpec(blk, cur(0)),
        out_shape=jax.ShapeDtypeStruct((bsz, s, B_HEADS * HEAD_DIM), bf16),
        scratch_shapes=[pltpu.VMEM((2 * span, LANES), f32), pltpu.VMEM((2 * span, LANES), f32),
                        pltpu.VMEM((nbr, span, LANES), f32), pltpu.VMEM((nbr, span, LANES), f32),
                        pltpu.VMEM((nbr, span, LANES), f32)],
        compiler_params=_params("parallel", "parallel", "parallel"),
        name="dilated_attn",
    )(slopes, pb, pb, pb, pb, pb)


def _gla_kernel(q_ref, k_ref, v_ref, r_ref, la_ref, hg_ref, o_ref, state_sc, a_sc, b_sc, q_sc, *, chunk):
    tb = q_ref.shape[1]

    @pl.when(pl.program_id(2) == 0)
    def _():
        state_sc[...] = jnp.zeros_like(state_sc)

    row = lax.broadcasted_iota(jnp.int32, (chunk, chunk), 0)
    col = lax.broadcasted_iota(jnp.int32, (chunk, chunk), 1)
    causal = row >= col
    tril = jnp.where(causal, 1.0, 0.0).astype(bf16)
    steep = jnp.min(la_ref[0]) * chunk < -80.0

    for ci in range(tb // chunk):
        sl = pl.ds(ci * chunk, chunk)
        la = la_ref[0, sl, :]
        hi = la.astype(bf16)
        lo = (la - hi.astype(f32)).astype(bf16)
        b = jnp.dot(tril, hi, preferred_element_type=f32) + jnp.dot(tril, lo, preferred_element_type=f32)
        q = q_ref[0, sl, :].astype(f32) * (C_DK ** -0.5)
        k = k_ref[0, sl, :].astype(f32)
        v = v_ref[0, sl, :]
        qd = q * jnp.exp(b)
        qt = qd.astype(bf16)

        @pl.when(jnp.logical_not(steep))
        def _():
            kt = (k * jnp.exp(-b)).astype(bf16)
            a_sc[...] = jnp.where(causal, lax.dot_general(qt, kt, NT_DIMS, preferred_element_type=f32), 0.0)

        @pl.when(steep)
        def _():
            b_sc[...] = b
            q_sc[...] = q

            def one_row(i, at):
                bi = b_sc[pl.ds(i, 1), :]
                qi = q_sc[pl.ds(i, 1), :]
                g = jnp.exp(jnp.minimum(bi - b, 0.0)) * k * qi
                colv = jnp.sum(g, axis=-1, keepdims=True)
                return jnp.where(col == i, colv, at)

            at = lax.fori_loop(0, chunk, one_row, jnp.zeros((chunk, chunk), f32))
            a_sc[...] = jnp.where(causal, at.T, 0.0)

        st = state_sc[...]
        o = (jnp.dot(a_sc[...].astype(bf16), v, preferred_element_type=f32)
             + lax.dot_general(qt, st.astype(bf16), NT_DIMS, preferred_element_type=f32))
        b_last = b[chunk - 1:chunk, :]
        kd = (k * jnp.exp(b_last - b)).astype(bf16)
        state_sc[...] = st * jnp.exp(b_last) + lax.dot_general(v, kd, TN_DIMS, preferred_element_type=f32)
        rr = r_ref[0, sl, :].astype(f32)
        o_ref[0, sl, :] = (_rms(o, hg_ref[...]) * (rr * jax.nn.sigmoid(rr))).astype(bf16)


def _gla(pqk, pvr, pla, head_g, tb=512, chunk=128):
    bsz, s, _ = pqk.shape
    return pl.pallas_call(
        functools.partial(_gla_kernel, chunk=chunk),
        grid=(bsz, C_HEADS, s // tb),
        in_specs=[
            pl.BlockSpec((1, tb, C_DK), lambda b, h, i: (b, i, h)),
            pl.BlockSpec((1, tb, C_DK), lambda b, h, i: (b, i, C_HEADS + h)),
            pl.BlockSpec((1, tb, C_DV), lambda b, h, i: (b, i, h)),
            pl.BlockSpec((1, tb, C_DV), lambda b, h, i: (b, i, C_HEADS + h)),
            pl.BlockSpec((1, tb, C_DK), lambda b, h, i: (b, i, h)),
            pl.BlockSpec((1, C_DV), lambda b, h, i: (0, 0)),
        ],
        out_specs=pl.BlockSpec((1, tb, C_DV), lambda b, h, i: (b, i, h)),
        out_shape=jax.ShapeDtypeStruct((bsz, s, C_HEADS * C_DV), bf16),
        scratch_shapes=[pltpu.VMEM((C_DV, C_DK), f32), pltpu.VMEM((chunk, chunk), f32),
                        pltpu.VMEM((chunk, C_DK), f32), pltpu.VMEM((chunk, C_DK), f32)],
        compiler_params=_params("parallel", "parallel", "arbitrary"),
        name="gla",
    )(pqk, pqk, pvr, pvr, pla, head_g)


def _post_kernel(x_ref, ya_ref, yb_ref, wo_ref, gate1_ref, sc2_ref, sh2_ref, gate2_ref, ng_ref,
                 wg_ref, wu_ref, wd_ref, o_ref, *, hc):
    half = ya_ref.shape[-1]
    y = (jnp.dot(ya_ref[0], wo_ref[0:half, :], preferred_element_type=f32)
         + jnp.dot(yb_ref[0], wo_ref[half:2 * half, :], preferred_element_type=f32))
    x1 = x_ref[0] + (1.0 + gate1_ref[0]) * _rms(y, ng_ref[0:1, :])
    hb = (_rms(x1, ng_ref[1:2, :]) * (1.0 + sc2_ref[0]) + sh2_ref[0]).astype(bf16)
    acc = jnp.zeros(x1.shape, f32)
    for c in range(0, wg_ref.shape[1], hc):
        gt = jnp.dot(hb, wg_ref[:, c:c + hc], preferred_element_type=f32)
        up = jnp.dot(hb, wu_ref[:, c:c + hc], preferred_element_type=f32)
        act = (gt * jax.nn.sigmoid(gt) * up).astype(bf16)
        acc = acc + jnp.dot(act, wd_ref[c:c + hc, :], preferred_element_type=f32)
    o_ref[0] = x1 + (1.0 + gate2_ref[0]) * _rms(acc, ng_ref[2:3, :])


def _post(x, ya, yb, cols, wo, gate1, sc2, sh2, gate2, ng, wg, wu, wd, tm=256, hc=1408):
    bsz, s, d = x.shape
    half = wo.shape[0] // 2
    row = lambda b, i: (b, i, 0)
    per_b = lambda b, i: (b, 0, 0)
    mod = pl.BlockSpec((1, 1, d), per_b)
    return pl.pallas_call(
        functools.partial(_post_kernel, hc=hc),
        grid=(bsz, s // tm),
        in_specs=[
            pl.BlockSpec((1, tm, d), row),
            pl.BlockSpec((1, tm, half), lambda b, i: (b, i, cols[0])),
            pl.BlockSpec((1, tm, half), lambda b, i: (b, i, cols[1])),
            _resident(wo.shape),
            mod, mod, mod, mod,
            _resident(ng.shape),
            _resident(wg.shape), _resident(wu.shape), _resident(wd.shape),
        ],
        out_specs=pl.BlockSpec((1, tm, d), row),
        out_shape=jax.ShapeDtypeStruct((bsz, s, d), f32),
        compiler_params=_params("parallel", "parallel"),
        name="post",
    )(x, ya, yb, wo, gate1, sc2, sh2, gate2, ng, wg, wu, wd)


def _alibi_slopes(n):
    return 2.0 ** (-8.0 * jnp.arange(1, n + 1, dtype=f32) / n)


def kernel(x, c, ada_w, ada_b, norm_g, ev_w_in, ev_lambda, ev_subln_g, ev_w_out, od_w_in, od_w_g2,
           od_b_g2, od_head_g, od_w_out, ffn_w_gate, ffn_w_up, ffn_w_down):
    depth = ada_w.shape[0]
    mod = _adaln(c, ada_w, ada_b)
    for l in range(depth):
        sh1, sc1, g1, sh2, sc2, g2 = (mod[l, :, j] for j in range(6))
        pre_g = norm_g[l, 0:1]
        if l % 2 == 0:
            e = l // 2
            lam_init = 0.8 - 0.6 * math.exp(-0.3 * l)
            pa, pb = _inproj_even(x, pre_g, sc1, sh1, ev_w_in[e].astype(bf16))
            ya = _diff_attn(pa, _alibi_slopes(A_HEADS), ev_lambda[e], ev_subln_g[e][None, :], lam_init)
            yb = _dilated_attn(pb, _alibi_slopes(B_HEADS))
            cols = (0, 0)
            wo = ev_w_out[e]
        else:
            o = l // 2
            w = od_w_in[o]
            nmain = w.shape[1] - C_GATE_RANK
            wg1 = jnp.zeros((w.shape[0], LANES), bf16).at[:, :C_GATE_RANK].set(w[:, nmain:].astype(bf16))
            wg2 = jnp.zeros((LANES, od_w_g2.shape[-1]), bf16).at[:C_GATE_RANK].set(od_w_g2[o].astype(bf16))
            pqk, pvr, pla = _inproj_odd(x, pre_g, sc1, sh1, w[:, :nmain].astype(bf16), wg1, wg2, od_b_g2[o][None, :])
            ya = yb = _gla(pqk, pvr, pla, od_head_g[o][None, :])
            cols = (0, 1)
            wo = od_w_out[o]
        x = _post(x, ya, yb, cols, wo.astype(bf16), g1, sc2, sh2, g2, norm_g[l, 1:4],
                  ffn_w_gate[l].astype(bf16), ffn_w_up[l].astype(bf16), ffn_w_down[l].astype(bf16))
    return x
```

```python
import functools
import math

import jax
import jax.numpy as jnp
from jax import lax
from jax.experimental import pallas as pl
from jax.experimental.pallas import tpu as pltpu

f32 = jnp.float32
bf16 = jnp.bfloat16

EPS = 1e-6
LANES = 128
HEAD_DIM = 64
A_HEADS = 4
B_HEADS = 8
DILATIONS = (1, 4, 16)
BAND = 128
C_HEADS = 4
C_DK = 128
C_DV = 256
C_GATE_RANK = 16
C_TAU = 16.0
VMEM_LIMIT = 56 * 1024 * 1024
LOG2E = math.log2(math.e)
ONES_ROWS = 16

NT_DIMS = (((1,), (1,)), ((), ()))
TN_DIMS = (((0,), (0,)), ((), ()))


def _rms(x, g):
    return x * lax.rsqrt(jnp.mean(x * x, axis=-1, keepdims=True) + EPS) * g


def _params(*sem):
    return pltpu.CompilerParams(dimension_semantics=sem, vmem_limit_bytes=VMEM_LIMIT)


def _resident(shape):
    nd = len(shape)
    return pl.BlockSpec(shape, lambda *_: (0,) * nd, pipeline_mode=pl.Buffered(1))


def _adaln_kernel(c_ref, w_ref, b_ref, o_ref):
    c = c_ref[...]
    a = (c * jax.nn.sigmoid(c)).astype(bf16)
    o_ref[0] = jnp.dot(a, w_ref[0].astype(bf16), preferred_element_type=f32) + b_ref[0]


def _adaln(c, ada_w, ada_b):
    depth, d, n = ada_w.shape
    bsz = c.shape[0]
    rows = 8
    cp = jnp.zeros((rows, d), f32).at[:bsz].set(c)
    tn = 768
    out = pl.pallas_call(
        _adaln_kernel,
        grid=(depth, n // tn),
        in_specs=[
            pl.BlockSpec((rows, d), lambda l, j: (0, 0)),
            pl.BlockSpec((1, d, tn), lambda l, j: (l, 0, j)),
            pl.BlockSpec((1, 1, tn), lambda l, j: (l, 0, j)),
        ],
        out_specs=pl.BlockSpec((1, rows, tn), lambda l, j: (l, 0, j)),
        out_shape=jax.ShapeDtypeStruct((depth, rows, n), f32),
        compiler_params=_params("parallel", "parallel"),
        name="adaln",
    )(cp, ada_w, ada_b.reshape(depth, 1, n))
    return out[:, :bsz].reshape(depth, bsz, 6, 1, d)


def _modulated(x_ref, g_ref, sc_ref, sh_ref):
    h = _rms(x_ref[0], g_ref[...])
    return (h * (1.0 + sc_ref[0]) + sh_ref[0]).astype(bf16)


def _inproj_even_kernel(x_ref, g_ref, sc_ref, sh_ref, w_ref, oa_ref, ob_ref, *, nc):
    hb = _modulated(x_ref, g_ref, sc_ref, sh_ref)
    na = oa_ref.shape[-1]
    for c in range(0, na, nc):
        oa_ref[0, :, c:c + nc] = jnp.dot(hb, w_ref[:, c:c + nc], preferred_element_type=f32).astype(bf16)
    for c in range(0, ob_ref.shape[-1], nc):
        ob_ref[0, :, c:c + nc] = jnp.dot(hb, w_ref[:, na + c:na + c + nc], preferred_element_type=f32)


def _inproj_even(x, g, sc, sh, w, tm=512):
    bsz, s, d = x.shape
    na = 3 * A_HEADS * 2 * HEAD_DIM
    nb = w.shape[1] - na
    row = lambda b, i: (b, i, 0)
    per_b = lambda b, i: (b, 0, 0)
    return pl.pallas_call(
        functools.partial(_inproj_even_kernel, nc=512),
        grid=(bsz, s // tm),
        in_specs=[
            pl.BlockSpec((1, tm, d), row),
            pl.BlockSpec((1, d), lambda b, i: (0, 0)),
            pl.BlockSpec((1, 1, d), per_b),
            pl.BlockSpec((1, 1, d), per_b),
            _resident(w.shape),
        ],
        out_specs=[pl.BlockSpec((1, tm, na), row), pl.BlockSpec((1, tm, nb), row)],
        out_shape=[jax.ShapeDtypeStruct((bsz, s, na), bf16), jax.ShapeDtypeStruct((bsz, s, nb), f32)],
        compiler_params=_params("parallel", "parallel"),
        name="inproj_even",
    )(x, g, sc, sh, w)


def _inproj_odd_kernel(x_ref, g_ref, sc_ref, sh_ref, w_ref, wg1_ref, wg2_ref, bg2_ref,
                       oqk_ref, ovr_ref, ola_ref, *, nc):
    hb = _modulated(x_ref, g_ref, sc_ref, sh_ref)
    nqk = oqk_ref.shape[-1]
    for c in range(0, nqk, nc):
        oqk_ref[0, :, c:c + nc] = jnp.dot(hb, w_ref[:, c:c + nc], preferred_element_type=f32).astype(bf16)
    for c in range(0, ovr_ref.shape[-1], nc):
        ovr_ref[0, :, c:c + nc] = jnp.dot(hb, w_ref[:, nqk + c:nqk + c + nc], preferred_element_type=f32).astype(bf16)
    glr = jnp.dot(hb, wg1_ref[...], preferred_element_type=f32)
    z = jnp.dot(glr.astype(bf16), wg2_ref[...], preferred_element_type=f32) + bg2_ref[...]
    ola_ref[0] = (jnp.minimum(z, 0.0) - jnp.log(1.0 + jnp.exp(-jnp.abs(z)))) * (1.0 / C_TAU)


def _inproj_odd(x, g, sc, sh, w, wg1, wg2, bg2, tm=512):
    bsz, s, d = x.shape
    nqk = 2 * C_HEADS * C_DK
    nvr = 2 * C_HEADS * C_DV
    nla = C_HEADS * C_DK
    row = lambda b, i: (b, i, 0)
    per_b = lambda b, i: (b, 0, 0)
    return pl.pallas_call(
        functools.partial(_inproj_odd_kernel, nc=512),
        grid=(bsz, s // tm),
        in_specs=[
            pl.BlockSpec((1, tm, d), row),
            pl.BlockSpec((1, d), lambda b, i: (0, 0)),
            pl.BlockSpec((1, 1, d), per_b),
            pl.BlockSpec((1, 1, d), per_b),
            _resident(w.shape),
            _resident(wg1.shape),
            _resident(wg2.shape),
            _resident(bg2.shape),
        ],
        out_specs=[pl.BlockSpec((1, tm, nqk), row), pl.BlockSpec((1, tm, nvr), row), pl.BlockSpec((1, tm, nla), row)],
        out_shape=[jax.ShapeDtypeStruct((bsz, s, nqk), bf16), jax.ShapeDtypeStruct((bsz, s, nvr), bf16),
                   jax.ShapeDtypeStruct((bsz, s, nla), f32)],
        compiler_params=_params("parallel", "parallel"),
        name="inproj_odd",
    )(x, g, sc, sh, w, wg1, wg2, bg2)


def _diff_attn_kernel(slope_ref, lam_ref, q_ref, k_ref, v_ref, g_ref, o_ref,
                      ks_sc, vt_sc, bias_sc, m_sc, acc_sc, *, t, lam_init):
    h = pl.program_id(1)
    i = pl.program_id(2)
    nblk = k_ref.shape[1] // t
    width = 2 * HEAD_DIM
    slope2 = slope_ref[h] * LOG2E

    @pl.when(i == 0)
    def _():
        lane = lax.broadcasted_iota(jnp.int32, (t, LANES), 1)
        first_half = lane < HEAD_DIM
        ones_row = jnp.where(lax.broadcasted_iota(jnp.int32, (ONES_ROWS, t), 0) == 0, 1.0, 0.0).astype(bf16)
        for c in range(nblk):
            kb = k_ref[0, c * t:(c + 1) * t, :]
            zero = jnp.zeros_like(kb)
            ks_sc[c, 0:t, :] = jnp.where(first_half, kb, zero)
            ks_sc[c, t:2 * t, :] = jnp.where(first_half, zero, kb)
            vt_sc[c, 0:width, :] = v_ref[0, c * t:(c + 1) * t, :].astype(f32).T.astype(bf16)
            vt_sc[c, width:width + ONES_ROWS, :] = ones_row
        kr = lax.broadcasted_iota(jnp.int32, (t, t), 0)
        qc = lax.broadcasted_iota(jnp.int32, (t, t), 1)
        bias = slope2 * (qc - kr).astype(f32)
        bias_sc[0] = bias
        bias_sc[1] = jnp.where(kr <= qc, bias, jnp.inf)

    qt = (q_ref[0].astype(f32) * (HEAD_DIM ** -0.5 * LOG2E)).T.astype(bf16)

    def block(j, bias, first):
        st = jnp.dot(ks_sc[j], qt, preferred_element_type=f32)
        vt = vt_sc[j]
        shift = slope2 * ((i - j) * t).astype(f32)
        for c in range(2):
            s = st[c * t:(c + 1) * t] - bias
            cmax = jnp.max(s, axis=0, keepdims=True)
            m_new = cmax if first else jnp.maximum(m_sc[c], cmax - shift)
            p = jnp.exp2(s - (m_new + shift)).astype(bf16)
            pv = jnp.dot(vt, p, preferred_element_type=f32)
            if first:
                acc_sc[c] = pv
            else:
                acc_sc[c] = jnp.exp2(m_sc[c] - m_new) * acc_sc[c] + pv
            m_sc[c] = m_new

    block(i, bias_sc[1], True)

    def body(j, carry):
        block(j, bias_sc[0], False)
        return carry

    lax.fori_loop(0, i, body, 0)

    lp = lam_ref[...]
    lam = (jnp.exp(jnp.sum(lp[0:1] * lp[1:2], axis=-1, keepdims=True))
           - jnp.exp(jnp.sum(lp[2:3] * lp[3:4], axis=-1, keepdims=True)) + lam_init)
    outs = [acc_sc[c, 0:width, :] / acc_sc[c, width:width + 1, :] for c in range(2)]
    ot = outs[0] - lam * outs[1]
    ot = ot * lax.rsqrt(jnp.mean(ot * ot, axis=0, keepdims=True) + EPS) * (g_ref[...] * (1.0 - lam_init))
    o_ref[0] = ot.T.astype(bf16)


def _diff_attn(pa, slopes, lam_p, subln_g, lam_init, t=512):
    bsz, s, _ = pa.shape
    width = 2 * HEAD_DIM
    nblk = s // t
    return pl.pallas_call(
        functools.partial(_diff_attn_kernel, t=t, lam_init=lam_init),
        grid=(bsz, A_HEADS, nblk),
        in_specs=[
            pl.BlockSpec(memory_space=pltpu.SMEM),
            pl.BlockSpec(lam_p.shape, lambda b, h, i: (0, 0)),
            pl.BlockSpec((1, t, width), lambda b, h, i: (b, i, h)),
            pl.BlockSpec((1, s, width), lambda b, h, i: (b, 0, A_HEADS + h)),
            pl.BlockSpec((1, s, width), lambda b, h, i: (b, 0, 2 * A_HEADS + h)),
            pl.BlockSpec((width, 1), lambda b, h, i: (0, 0)),
        ],
        out_specs=pl.BlockSpec((1, t, width), lambda b, h, i: (b, i, h)),
        out_shape=jax.ShapeDtypeStruct((bsz, s, A_HEADS * width), bf16),
        scratch_shapes=[pltpu.VMEM((nblk, 2 * t, width), bf16),
                        pltpu.VMEM((nblk, width + ONES_ROWS, t), bf16),
                        pltpu.VMEM((2, t, t), f32),
                        pltpu.VMEM((2, 1, t), f32),
                        pltpu.VMEM((2, width + ONES_ROWS, t), f32)],
        compiler_params=_params("parallel", "parallel", "arbitrary"),
        name="diff_attn",
    )(slopes, lam_p, pa, pa, pa, subln_g)


def _dilated_kernel(slope_ref, q_ref, kp_ref, kc_ref, vp_ref, vc_ref, o_ref,
                    kcat, vcat, acc_sc, m_sc, l_sc, *, span):
    hp = pl.program_id(1)
    sidx = pl.program_id(2)
    blk = BAND
    nblocks = span // blk
    kcat[0:span] = kp_ref[0]
    kcat[span:2 * span] = kc_ref[0]
    vcat[0:span] = vp_ref[0]
    vcat[span:2 * span] = vc_ref[0]

    lane = lax.broadcasted_iota(jnp.int32, (blk, LANES), 1)
    head_a = lane < HEAD_DIM
    a = lax.broadcasted_iota(jnp.int32, (blk, 2 * blk), 0)
    bi = lax.broadcasted_iota(jnp.int32, (blk, 2 * blk), 1)
    rel = blk + a - bi
    valid = (rel >= 0) & (rel <= BAND)
    before_start = bi < blk
    relf = rel.astype(f32)
    slopes = (slope_ref[2 * hp], slope_ref[2 * hp + 1])

    for br, d in enumerate(DILATIONS):
        per_stream = nblocks // d
        biases = [jnp.where(valid, (sl * d) * relf, jnp.inf) for sl in slopes]

        def body(it, carry, br=br, d=d, per_stream=per_stream, biases=biases):
            r = it // per_stream
            n = it % per_stream
            q_start = n * (blk * d) + r
            k_start = span + (n - 1) * (blk * d) + r
            qb = (q_ref[0, pl.ds(q_start, blk, stride=d), :] * (HEAD_DIM ** -0.5)).astype(bf16)
            kk = kcat[pl.ds(k_start, 2 * blk, stride=d), :].astype(bf16)
            vv = vcat[pl.ds(k_start, 2 * blk, stride=d), :].astype(bf16)
            zero = jnp.zeros_like(qb)
            qh = (jnp.where(head_a, qb, zero), jnp.where(head_a, zero, qb))
            dead = jnp.logical_and(jnp.logical_and(sidx == 0, n == 0), before_start)
            outs, ms, ls = [], [], []
            for c in range(2):
                s = lax.dot_general(qh[c], kk, NT_DIMS, preferred_element_type=f32) - biases[c]
                s = jnp.where(dead, -jnp.inf, s)
                m = jnp.max(s, axis=-1, keepdims=True)
                p = jnp.exp(s - m)
                ls.append(jnp.sum(p, axis=-1, keepdims=True))
                ms.append(m)
                outs.append(jnp.dot(p.astype(bf16), vv, preferred_element_type=f32))
            dst = pl.ds(q_start, blk, stride=d)
            acc_sc[br, dst, :] = jnp.where(head_a, outs[0], outs[1])
            m_sc[br, dst, :] = jnp.where(head_a, ms[0], ms[1])
            l_sc[br, dst, :] = jnp.where(head_a, ls[0], ls[1])
            return carry

        lax.fori_loop(0, nblocks, body, 0)

    m_all = jnp.maximum(jnp.maximum(m_sc[0], m_sc[1]), m_sc[2])
    num = jnp.zeros((span, LANES), f32)
    den = jnp.zeros((span, LANES), f32)
    for br in range(len(DILATIONS)):
        w = jnp.exp(m_sc[br] - m_all)
        num = num + w * acc_sc[br]
        den = den + w * l_sc[br]
    o_ref[0] = (num / den).astype(bf16)


def _dilated_attn(pb, slopes, span=2048):
    bsz, s, _ = pb.shape
    pairs = B_HEADS // 2
    cur = lambda off: (lambda b, h, i: (b, i, off + h))
    prev = lambda off: (lambda b, h, i: (b, jnp.maximum(i - 1, 0), off + h))
    blk = (1, span, LANES)
    nbr = len(DILATIONS)
    return pl.pallas_call(
        functools.partial(_dilated_kernel, span=span),
        grid=(bsz, pairs, s // span),
        in_specs=[
            pl.BlockSpec(memory_space=pltpu.SMEM),
            pl.BlockSpec(blk, cur(0)),
            pl.BlockSpec(blk, prev(pairs)),
            pl.BlockSpec(blk, cur(pairs)),
            pl.BlockSpec(blk, prev(2 * pairs)),
            pl.BlockSpec(blk, cur(2 * pairs)),
        ],
        out_specs=pl.BlockSpec(blk, cur(0)),
        out_shape=jax.ShapeDtypeStruct((bsz, s, B_HEADS * HEAD_DIM), bf16),
        scratch_shapes=[pltpu.VMEM((2 * span, LANES), f32), pltpu.VMEM((2 * span, LANES), f32),
                        pltpu.VMEM((nbr, span, LANES), f32), pltpu.VMEM((nbr, span, LANES), f32),
                        pltpu.VMEM((nbr, span, LANES), f32)],
        compiler_params=_params("parallel", "parallel", "parallel"),
        name="dilated_attn",
    )(slopes, pb, pb, pb, pb, pb)


def _gla_kernel(q_ref, k_ref, v_ref, r_ref, la_ref, hg_ref, o_ref, state_sc, a_sc, b_sc, q_sc, *, chunk):
    tb = q_ref.shape[1]

    @pl.when(pl.program_id(2) == 0)
    def _():
        state_sc[...] = jnp.zeros_like(state_sc)

    row = lax.broadcasted_iota(jnp.int32, (chunk, chunk), 0)
    col = lax.broadcasted_iota(jnp.int32, (chunk, chunk), 1)
    causal = row >= col
    tril = jnp.where(causal, 1.0, 0.0).astype(bf16)
    steep = jnp.min(la_ref[0]) * chunk < -80.0

    for ci in range(tb // chunk):
        sl = pl.ds(ci * chunk, chunk)
        la = la_ref[0, sl, :]
        hi = la.astype(bf16)
        lo = (la - hi.astype(f32)).astype(bf16)
        b = jnp.dot(tril, hi, preferred_element_type=f32) + jnp.dot(tril, lo, preferred_element_type=f32)
        q = q_ref[0, sl, :].astype(f32) * (C_DK ** -0.5)
        k = k_ref[0, sl, :].astype(f32)
        v = v_ref[0, sl, :]
        qd = q * jnp.exp(b)
        qt = qd.astype(bf16)

        @pl.when(jnp.logical_not(steep))
        def _():
            kt = (k * jnp.exp(-b)).astype(bf16)
            a_sc[...] = jnp.where(causal, lax.dot_general(qt, kt, NT_DIMS, preferred_element_type=f32), 0.0)

        @pl.when(steep)
        def _():
            b_sc[...] = b
            q_sc[...] = q

            def one_row(i, at):
                bi = b_sc[pl.ds(i, 1), :]
                qi = q_sc[pl.ds(i, 1), :]
                g = jnp.exp(jnp.minimum(bi - b, 0.0)) * k * qi
                colv = jnp.sum(g, axis=-1, keepdims=True)
                return jnp.where(col == i, colv, at)

            at = lax.fori_loop(0, chunk, one_row, jnp.zeros((chunk, chunk), f32))
            a_sc[...] = jnp.where(causal, at.T, 0.0)

        st = state_sc[...]
        o = (jnp.dot(a_sc[...].astype(bf16), v, preferred_element_type=f32)
             + lax.dot_general(qt, st.astype(bf16), NT_DIMS, preferred_element_type=f32))
        b_last = b[chunk - 1:chunk, :]
        kd = (k * jnp.exp(b_last - b)).astype(bf16)
        state_sc[...] = st * jnp.exp(b_last) + lax.dot_general(v, kd, TN_DIMS, preferred_element_type=f32)
        rr = r_ref[0, sl, :].astype(f32)
        o_ref[0, sl, :] = (_rms(o, hg_ref[...]) * (rr * jax.nn.sigmoid(rr))).astype(bf16)


def _gla(pqk, pvr, pla, head_g, tb=512, chunk=128):
    bsz, s, _ = pqk.shape
    return pl.pallas_call(
        functools.partial(_gla_kernel, chunk=chunk),
        grid=(bsz, C_HEADS, s // tb),
        in_specs=[
            pl.BlockSpec((1, tb, C_DK), lambda b, h, i: (b, i, h)),
            pl.BlockSpec((1, tb, C_DK), lambda b, h, i: (b, i, C_HEADS + h)),
            pl.BlockSpec((1, tb, C_DV), lambda b, h, i: (b, i, h)),
            pl.BlockSpec((1, tb, C_DV), lambda b, h, i: (b, i, C_HEADS + h)),
            pl.BlockSpec((1, tb, C_DK), lambda b, h, i: (b, i, h)),
            pl.BlockSpec((1, C_DV), lambda b, h, i: (0, 0)),
        ],
        out_specs=pl.BlockSpec((1, tb, C_DV), lambda b, h, i: (b, i, h)),
        out_shape=jax.ShapeDtypeStruct((bsz, s, C_HEADS * C_DV), bf16),
        scratch_shapes=[pltpu.VMEM((C_DV, C_DK), f32), pltpu.VMEM((chunk, chunk), f32),
                        pltpu.VMEM((chunk, C_DK), f32), pltpu.VMEM((chunk, C_DK), f32)],
        compiler_params=_params("parallel", "parallel", "arbitrary"),
        name="gla",
    )(pqk, pqk, pvr, pvr, pla, head_g)


def _post_kernel(x_ref, ya_ref, yb_ref, wo_ref, gate1_ref, sc2_ref, sh2_ref, gate2_ref, ng_ref,
                 wg_ref, wu_ref, wd_ref, o_ref, *, hc):
    half = ya_ref.shape[-1]
    y = (jnp.dot(ya_ref[0], wo_ref[0:half, :], preferred_element_type=f32)
         + jnp.dot(yb_ref[0], wo_ref[half:2 * half, :], preferred_element_type=f32))
    x1 = x_ref[0] + (1.0 + gate1_ref[0]) * _rms(y, ng_ref[0:1, :])
    hb = (_rms(x1, ng_ref[1:2, :]) * (1.0 + sc2_ref[0]) + sh2_ref[0]).astype(bf16)
    acc = jnp.zeros(x1.shape, f32)
    for c in range(0, wg_ref.shape[1], hc):
        gt = jnp.dot(hb, wg_ref[:, c:c + hc], preferred_element_type=f32)
        up = jnp.dot(hb, wu_ref[:, c:c + hc], preferred_element_type=f32)
        act = (gt * jax.nn.sigmoid(gt) * up).astype(bf16)
        acc = acc + jnp.dot(act, wd_ref[c:c + hc, :], preferred_element_type=f32)
    o_ref[0] = x1 + (1.0 + gate2_ref[0]) * _rms(acc, ng_ref[2:3, :])


def _post(x, ya, yb, cols, wo, gate1, sc2, sh2, gate2, ng, wg, wu, wd, tm=256, hc=1408):
    bsz, s, d = x.shape
    half = wo.shape[0] // 2
    row = lambda b, i: (b, i, 0)
    per_b = lambda b, i: (b, 0, 0)
    mod = pl.BlockSpec((1, 1, d), per_b)
    return pl.pallas_call(
        functools.partial(_post_kernel, hc=hc),
        grid=(bsz, s // tm),
        in_specs=[
            pl.BlockSpec((1, tm, d), row),
            pl.BlockSpec((1, tm, half), lambda b, i: (b, i, cols[0])),
            pl.BlockSpec((1, tm, half), lambda b, i: (b, i, cols[1])),
            _resident(wo.shape),
            mod, mod, mod, mod,
            _resident(ng.shape),
            _resident(wg.shape), _resident(wu.shape), _resident(wd.shape),
        ],
        out_specs=pl.BlockSpec((1, tm, d), row),
        out_shape=jax.ShapeDtypeStruct((bsz, s, d), f32),
        compiler_params=_params("parallel", "parallel"),
        name="post",
    )(x, ya, yb, wo, gate1, sc2, sh2, gate2, ng, wg, wu, wd)


def _alibi_slopes(n):
    return 2.0 ** (-8.0 * jnp.arange(1, n + 1, dtype=f32) / n)


def kernel(x, c, ada_w, ada_b, norm_g, ev_w_in, ev_lambda, ev_subln_g, ev_w_out, od_w_in, od_w_g2,
           od_b_g2, od_head_g, od_w_out, ffn_w_gate, ffn_w_up, ffn_w_down):
    depth = ada_w.shape[0]
    mod = _adaln(c, ada_w, ada_b)
    for l in range(depth):
        sh1, sc1, g1, sh2, sc2, g2 = (mod[l, :, j] for j in range(6))
        pre_g = norm_g[l, 0:1]
        if l % 2 == 0:
            e = l // 2
            lam_init = 0.8 - 0.6 * math.exp(-0.3 * l)
            pa, pb = _inproj_even(x, pre_g, sc1, sh1, ev_w_in[e].astype(bf16))
            ya = _diff_attn(pa, _alibi_slopes(A_HEADS), ev_lambda[e], ev_subln_g[e][:, None], lam_init)
            yb = _dilated_attn(pb, _alibi_slopes(B_HEADS))
            cols = (0, 0)
            wo = ev_w_out[e]
        else:
            o = l // 2
            w = od_w_in[o]
            nmain = w.shape[1] - C_GATE_RANK
            wg1 = jnp.zeros((w.shape[0], LANES), bf16).at[:, :C_GATE_RANK].set(w[:, nmain:].astype(bf16))
            wg2 = jnp.zeros((LANES, od_w_g2.shape[-1]), bf16).at[:C_GATE_RANK].set(od_w_g2[o].astype(bf16))
            pqk, pvr, pla = _inproj_odd(x, pre_g, sc1, sh1, w[:, :nmain].astype(bf16), wg1, wg2, od_b_g2[o][None, :])
            ya = yb = _gla(pqk, pvr, pla, od_head_g[o][None, :])
            cols = (0, 1)
            wo = od_w_out[o]
        x = _post(x, ya, yb, cols, wo.astype(bf16), g1, sc2, sh2, g2, norm_g[l, 1:4],
                  ffn_w_gate[l].astype(bf16), ffn_w_up[l].astype(bf16), ffn_w_down[l].astype(bf16))
    return x
```

```python
import functools
import math

import jax
import jax.numpy as jnp
from jax import lax
from jax.experimental import pallas as pl
from jax.experimental.pallas import tpu as pltpu

f32 = jnp.float32
bf16 = jnp.bfloat16

EPS = 1e-6
LANES = 128
HEAD_DIM = 64
A_HEADS = 4
B_HEADS = 8
DILATIONS = (1, 4, 16)
BAND = 128
BLOCKS_IN_FLIGHT = 8
C_HEADS = 4
C_DK = 128
C_DV = 256
C_GATE_RANK = 16
C_TAU = 16.0
VMEM_LIMIT = 56 * 1024 * 1024
LOG2E = math.log2(math.e)
ONES_ROWS = 16

NT_DIMS = (((1,), (1,)), ((), ()))
TN_DIMS = (((0,), (0,)), ((), ()))


def _rms(x, g):
    return x * lax.rsqrt(jnp.mean(x * x, axis=-1, keepdims=True) + EPS) * g


def _params(*sem):
    return pltpu.CompilerParams(dimension_semantics=sem, vmem_limit_bytes=VMEM_LIMIT)


def _resident(shape):
    nd = len(shape)
    return pl.BlockSpec(shape, lambda *_: (0,) * nd, pipeline_mode=pl.Buffered(1))


def _adaln_kernel(c_ref, w_ref, b_ref, o_ref):
    c = c_ref[...]
    a = (c * jax.nn.sigmoid(c)).astype(bf16)
    o_ref[0] = jnp.dot(a, w_ref[0].astype(bf16), preferred_element_type=f32) + b_ref[0]


def _adaln(c, ada_w, ada_b):
    depth, d, n = ada_w.shape
    bsz = c.shape[0]
    rows = 8
    cp = jnp.zeros((rows, d), f32).at[:bsz].set(c)
    tn = 768
    out = pl.pallas_call(
        _adaln_kernel,
        grid=(depth, n // tn),
        in_specs=[
            pl.BlockSpec((rows, d), lambda l, j: (0, 0)),
            pl.BlockSpec((1, d, tn), lambda l, j: (l, 0, j)),
            pl.BlockSpec((1, 1, tn), lambda l, j: (l, 0, j)),
        ],
        out_specs=pl.BlockSpec((1, rows, tn), lambda l, j: (l, 0, j)),
        out_shape=jax.ShapeDtypeStruct((depth, rows, n), f32),
        compiler_params=_params("parallel", "parallel"),
        name="adaln",
    )(cp, ada_w, ada_b.reshape(depth, 1, n))
    return out[:, :bsz].reshape(depth, bsz, 6, 1, d)


def _modulated(x_ref, g_ref, sc_ref, sh_ref):
    h = _rms(x_ref[0], g_ref[...])
    return (h * (1.0 + sc_ref[0]) + sh_ref[0]).astype(bf16)


def _inproj_even_kernel(x_ref, g_ref, sc_ref, sh_ref, w_ref, oa_ref, ob_ref, *, nc):
    hb = _modulated(x_ref, g_ref, sc_ref, sh_ref)
    na = oa_ref.shape[-1]
    for c in range(0, na, nc):
        oa_ref[0, :, c:c + nc] = jnp.dot(hb, w_ref[:, c:c + nc], preferred_element_type=f32).astype(bf16)
    for c in range(0, ob_ref.shape[-1], nc):
        ob_ref[0, :, c:c + nc] = jnp.dot(hb, w_ref[:, na + c:na + c + nc], preferred_element_type=f32)


def _inproj_even(x, g, sc, sh, w, tm=512):
    bsz, s, d = x.shape
    na = 3 * A_HEADS * 2 * HEAD_DIM
    nb = w.shape[1] - na
    row = lambda b, i: (b, i, 0)
    per_b = lambda b, i: (b, 0, 0)
    return pl.pallas_call(
        functools.partial(_inproj_even_kernel, nc=512),
        grid=(bsz, s // tm),
        in_specs=[
            pl.BlockSpec((1, tm, d), row),
            pl.BlockSpec((1, d), lambda b, i: (0, 0)),
            pl.BlockSpec((1, 1, d), per_b),
            pl.BlockSpec((1, 1, d), per_b),
            _resident(w.shape),
        ],
        out_specs=[pl.BlockSpec((1, tm, na), row), pl.BlockSpec((1, tm, nb), row)],
        out_shape=[jax.ShapeDtypeStruct((bsz, s, na), bf16), jax.ShapeDtypeStruct((bsz, s, nb), f32)],
        compiler_params=_params("parallel", "parallel"),
        name="inproj_even",
    )(x, g, sc, sh, w)


def _inproj_odd_kernel(x_ref, g_ref, sc_ref, sh_ref, w_ref, wg1_ref, wg2_ref, bg2_ref,
                       oqk_ref, ovr_ref, ola_ref, *, nc):
    hb = _modulated(x_ref, g_ref, sc_ref, sh_ref)
    nqk = oqk_ref.shape[-1]
    for c in range(0, nqk, nc):
        oqk_ref[0, :, c:c + nc] = jnp.dot(hb, w_ref[:, c:c + nc], preferred_element_type=f32).astype(bf16)
    for c in range(0, ovr_ref.shape[-1], nc):
        ovr_ref[0, :, c:c + nc] = jnp.dot(hb, w_ref[:, nqk + c:nqk + c + nc], preferred_element_type=f32).astype(bf16)
    glr = jnp.dot(hb, wg1_ref[...], preferred_element_type=f32)
    z = jnp.dot(glr.astype(bf16), wg2_ref[...], preferred_element_type=f32) + bg2_ref[...]
    ola_ref[0] = (jnp.minimum(z, 0.0) - jnp.log(1.0 + jnp.exp(-jnp.abs(z)))) * (1.0 / C_TAU)


def _inproj_odd(x, g, sc, sh, w, wg1, wg2, bg2, tm=512):
    bsz, s, d = x.shape
    nqk = 2 * C_HEADS * C_DK
    nvr = 2 * C_HEADS * C_DV
    nla = C_HEADS * C_DK
    row = lambda b, i: (b, i, 0)
    per_b = lambda b, i: (b, 0, 0)
    return pl.pallas_call(
        functools.partial(_inproj_odd_kernel, nc=512),
        grid=(bsz, s // tm),
        in_specs=[
            pl.BlockSpec((1, tm, d), row),
            pl.BlockSpec((1, d), lambda b, i: (0, 0)),
            pl.BlockSpec((1, 1, d), per_b),
            pl.BlockSpec((1, 1, d), per_b),
            _resident(w.shape),
            _resident(wg1.shape),
            _resident(wg2.shape),
            _resident(bg2.shape),
        ],
        out_specs=[pl.BlockSpec((1, tm, nqk), row), pl.BlockSpec((1, tm, nvr), row), pl.BlockSpec((1, tm, nla), row)],
        out_shape=[jax.ShapeDtypeStruct((bsz, s, nqk), bf16), jax.ShapeDtypeStruct((bsz, s, nvr), bf16),
                   jax.ShapeDtypeStruct((bsz, s, nla), f32)],
        compiler_params=_params("parallel", "parallel"),
        name="inproj_odd",
    )(x, g, sc, sh, w, wg1, wg2, bg2)


def _diff_attn_kernel(slope_ref, lam_ref, q_ref, k_ref, v_ref, g_ref, o_ref,
                      ks_sc, vt_sc, bias_sc, sta_sc, stb_sc, m_sc, acc_sc, *, t, lam_init):
    h = pl.program_id(1)
    i = pl.program_id(2)
    nblk = k_ref.shape[1] // t
    width = 2 * HEAD_DIM
    slope2 = slope_ref[h] * LOG2E

    @pl.when(i == 0)
    def _():
        lane = lax.broadcasted_iota(jnp.int32, (t, LANES), 1)
        first_half = lane < HEAD_DIM
        ones_row = jnp.where(lax.broadcasted_iota(jnp.int32, (ONES_ROWS, t), 0) == 0, 1.0, 0.0).astype(bf16)
        for c in range(nblk):
            kb = k_ref[0, c * t:(c + 1) * t, :]
            zero = jnp.zeros_like(kb)
            ks_sc[c, 0:t, :] = jnp.where(first_half, kb, zero)
            ks_sc[c, t:2 * t, :] = jnp.where(first_half, zero, kb)
            vt_sc[c, 0:width, :] = v_ref[0, c * t:(c + 1) * t, :].astype(f32).T.astype(bf16)
            vt_sc[c, width:width + ONES_ROWS, :] = ones_row
        kr = lax.broadcasted_iota(jnp.int32, (t, t), 0)
        qc = lax.broadcasted_iota(jnp.int32, (t, t), 1)
        bias = slope2 * (qc - kr).astype(f32)
        bias_sc[0] = bias
        bias_sc[1] = jnp.where(kr <= qc, bias, jnp.inf)

    qt = (q_ref[0].astype(f32) * (HEAD_DIM ** -0.5 * LOG2E)).T.astype(bf16)

    def scores_into(st_ref, j):
        st_ref[...] = jnp.dot(ks_sc[j], qt, preferred_element_type=f32)

    def softmax_pv(st_ref, j, bias):
        vt = vt_sc[j]
        shift = slope2 * ((i - j) * t).astype(f32)
        for c in range(2):
            s = st_ref[c * t:(c + 1) * t, :] - bias
            m_old = m_sc[c]
            m_new = jnp.maximum(m_old, jnp.max(s, axis=0, keepdims=True) - shift)
            p = jnp.exp2(s - (m_new + shift)).astype(bf16)
            pv = jnp.dot(vt, p, preferred_element_type=f32)
            acc_sc[c] = jnp.exp2(m_old - m_new) * acc_sc[c] + pv
            m_sc[c] = m_new

    m_sc[...] = jnp.full(m_sc.shape, -jnp.inf, f32)
    acc_sc[...] = jnp.zeros(acc_sc.shape, f32)
    scores_into(sta_sc, 0)

    def body(jj, carry):
        j = 2 * jj
        scores_into(stb_sc, j + 1)
        softmax_pv(sta_sc, j, bias_sc[0])
        scores_into(sta_sc, j + 2)
        softmax_pv(stb_sc, j + 1, bias_sc[0])
        return carry

    lax.fori_loop(0, i // 2, body, 0)

    @pl.when(i % 2 == 0)
    def _():
        softmax_pv(sta_sc, i, bias_sc[1])

    @pl.when(i % 2 == 1)
    def _():
        scores_into(stb_sc, i)
        softmax_pv(sta_sc, i - 1, bias_sc[0])
        softmax_pv(stb_sc, i, bias_sc[1])

    lp = lam_ref[...]
    lam = (jnp.exp(jnp.sum(lp[0:1] * lp[1:2], axis=-1, keepdims=True))
           - jnp.exp(jnp.sum(lp[2:3] * lp[3:4], axis=-1, keepdims=True)) + lam_init)
    outs = [acc_sc[c, 0:width, :] / acc_sc[c, width:width + 1, :] for c in range(2)]
    ot = outs[0] - lam * outs[1]
    ot = ot * lax.rsqrt(jnp.mean(ot * ot, axis=0, keepdims=True) + EPS) * (g_ref[...] * (1.0 - lam_init))
    o_ref[0] = ot.T.astype(bf16)


def _diff_attn(pa, slopes, lam_p, subln_g, lam_init, t=512):
    bsz, s, _ = pa.shape
    width = 2 * HEAD_DIM
    nblk = s // t
    return pl.pallas_call(
        functools.partial(_diff_attn_kernel, t=t, lam_init=lam_init),
        grid=(bsz, A_HEADS, nblk),
        in_specs=[
            pl.BlockSpec(memory_space=pltpu.SMEM),
            pl.BlockSpec(lam_p.shape, lambda b, h, i: (0, 0)),
            pl.BlockSpec((1, t, width), lambda b, h, i: (b, i, h)),
            pl.BlockSpec((1, s, width), lambda b, h, i: (b, 0, A_HEADS + h)),
            pl.BlockSpec((1, s, width), lambda b, h, i: (b, 0, 2 * A_HEADS + h)),
            pl.BlockSpec((width, 1), lambda b, h, i: (0, 0)),
        ],
        out_specs=pl.BlockSpec((1, t, width), lambda b, h, i: (b, i, h)),
        out_shape=jax.ShapeDtypeStruct((bsz, s, A_HEADS * width), bf16),
        scratch_shapes=[pltpu.VMEM((nblk, 2 * t, width), bf16),
                        pltpu.VMEM((nblk, width + ONES_ROWS, t), bf16),
                        pltpu.VMEM((2, t, t), f32),
                        pltpu.VMEM((2 * t, t), f32),
                        pltpu.VMEM((2 * t, t), f32),
                        pltpu.VMEM((2, 1, t), f32),
                        pltpu.VMEM((2, width + ONES_ROWS, t), f32)],
        compiler_params=_params("parallel", "parallel", "arbitrary"),
        name="diff_attn",
    )(slopes, lam_p, pa, pa, pa, subln_g)


def _dilated_kernel(slope_ref, q_ref, kp_ref, kc_ref, vp_ref, vc_ref, o_ref,
                    kcat, vcat, acc_sc, m_sc, l_sc, *, span):
    hp = pl.program_id(1)
    sidx = pl.program_id(2)
    blk = BAND
    nblocks = span // blk
    kcat[0:span] = kp_ref[0]
    kcat[span:2 * span] = kc_ref[0]
    vcat[0:span] = vp_ref[0]
    vcat[span:2 * span] = vc_ref[0]

    lane = lax.broadcasted_iota(jnp.int32, (blk, LANES), 1)
    head_a = lane < HEAD_DIM
    a = lax.broadcasted_iota(jnp.int32, (blk, 2 * blk), 0)
    bi = lax.broadcasted_iota(jnp.int32, (blk, 2 * blk), 1)
    rel = blk + a - bi
    valid = (rel >= 0) & (rel <= BAND)
    before_start = bi < blk
    relf = rel.astype(f32)
    slopes = (slope_ref[2 * hp], slope_ref[2 * hp + 1])

    for br, d in enumerate(DILATIONS):
        per_stream = nblocks // d
        biases = [jnp.where(valid, (sl * (LOG2E * d)) * relf, jnp.inf) for sl in slopes]

        def one_block(r, n, br=br, d=d, biases=biases):
            q_start = n * (blk * d) + r
            k_start = span + (n - 1) * (blk * d) + r
            qb = (q_ref[0, pl.ds(q_start, blk, stride=d), :] * (HEAD_DIM ** -0.5 * LOG2E)).astype(bf16)
            kk = kcat[pl.ds(k_start, 2 * blk, stride=d), :].astype(bf16)
            vv = vcat[pl.ds(k_start, 2 * blk, stride=d), :].astype(bf16)
            zero = jnp.zeros_like(qb)
            qh = (jnp.where(head_a, qb, zero), jnp.where(head_a, zero, qb))
            outs, ms, ls = [], [], []
            for c in range(2):
                s = lax.dot_general(qh[c], kk, NT_DIMS, preferred_element_type=f32) - biases[c]
                if n == 0:
                    s = jnp.where(jnp.logical_and(sidx == 0, before_start), -jnp.inf, s)
                m = jnp.max(s, axis=-1, keepdims=True)
                p = jnp.exp2(s - m)
                ls.append(jnp.sum(p, axis=-1, keepdims=True))
                ms.append(m)
                outs.append(jnp.dot(p.astype(bf16), vv, preferred_element_type=f32))
            dst = pl.ds(q_start, blk, stride=d)
            acc_sc[br, dst, :] = jnp.where(head_a, outs[0], outs[1])
            m_sc[br, dst, :] = jnp.where(head_a, ms[0], ms[1])
            l_sc[br, dst, :] = jnp.where(head_a, ls[0], ls[1])

        def one_stream(r, carry, per_stream=per_stream, one_block=one_block):
            for n in range(per_stream):
                one_block(r, n)
            return carry

        lax.fori_loop(0, d, one_stream, 0, unroll=max(1, min(d, BLOCKS_IN_FLIGHT // per_stream)))

    m_all = jnp.maximum(jnp.maximum(m_sc[0], m_sc[1]), m_sc[2])
    num = jnp.zeros((span, LANES), f32)
    den = jnp.zeros((span, LANES), f32)
    for br in range(len(DILATIONS)):
        w = jnp.exp2(m_sc[br] - m_all)
        num = num + w * acc_sc[br]
        den = den + w * l_sc[br]
    o_ref[0] = (num / den).astype(bf16)


def _dilated_attn(pb, slopes, span=2048):
    bsz, s, _ = pb.shape
    pairs = B_HEADS // 2
    cur = lambda off: (lambda b, h, i: (b, i, off + h))
    prev = lambda off: (lambda b, h, i: (b, jnp.maximum(i - 1, 0), off + h))
    blk = (1, span, LANES)
    nbr = len(DILATIONS)
    return pl.pallas_call(
        functools.partial(_dilated_kernel, span=span),
        grid=(bsz, pairs, s // span),
        in_specs=[
            pl.BlockSpec(memory_space=pltpu.SMEM),
            pl.BlockSpec(blk, cur(0)),
            pl.BlockSpec(blk, prev(pairs)),
            pl.BlockSpec(blk, cur(pairs)),
            pl.BlockSpec(blk, prev(2 * pairs)),
            pl.BlockSpec(blk, cur(2 * pairs)),
        ],
        out_specs=pl.BlockSpec(blk, cur(0)),
        out_shape=jax.ShapeDtypeStruct((bsz, s, B_HEADS * HEAD_DIM), bf16),
        scratch_shapes=[pltpu.VMEM((2 * span, LANES), f32), pltpu.VMEM((2 * span, LANES), f32),
                        pltpu.VMEM((nbr, span, LANES), f32), pltpu.VMEM((nbr, span, LANES), f32),
                        pltpu.VMEM((nbr, span, LANES), f32)],
        compiler_params=_params("parallel", "parallel", "parallel"),
        name="dilated_attn",
    )(slopes, pb, pb, pb, pb, pb)


def _gla_kernel(q_ref, k_ref, v_ref, r_ref, la_ref, hg_ref, o_ref, state_sc, b_sc, q_sc, *, chunk):
    tb = q_ref.shape[1]

    @pl.when(pl.program_id(2) == 0)
    def _():
        state_sc[...] = jnp.zeros_like(state_sc)

    row = lax.broadcasted_iota(jnp.int32, (chunk, chunk), 0)
    col = lax.broadcasted_iota(jnp.int32, (chunk, chunk), 1)
    causal = row >= col
    tril = jnp.where(causal, 1.0, 0.0).astype(bf16)
    steep = jnp.min(la_ref[0]) * chunk < -80.0

    def intra_exact(b, q, k):
        b_sc[...] = b
        q_sc[...] = q

        def one_row(i, at):
            bi = b_sc[pl.ds(i, 1), :]
            qi = q_sc[pl.ds(i, 1), :]
            g = jnp.exp(jnp.minimum(bi - b, 0.0)) * k * qi
            colv = jnp.sum(g, axis=-1, keepdims=True)
            return jnp.where(col == i, colv, at)

        return lax.fori_loop(0, chunk, one_row, jnp.zeros((chunk, chunk), f32)).T

    def step(exact):
        sls = [pl.ds(ci * chunk, chunk) for ci in range(tb // chunk)]
        bs = []
        for sl in sls:
            la = la_ref[0, sl, :]
            hi = la.astype(bf16)
            lo = (la - hi.astype(f32)).astype(bf16)
            bs.append(jnp.dot(tril, hi, preferred_element_type=f32)
                      + jnp.dot(tril, lo, preferred_element_type=f32))
        qs = [q_ref[0, sl, :].astype(f32) * (C_DK ** -0.5) for sl in sls]
        ks = [k_ref[0, sl, :].astype(f32) for sl in sls]
        qts = [(q * jnp.exp(b)).astype(bf16) for q, b in zip(qs, bs)]
        lasts = [b[chunk - 1:chunk, :] for b in bs]
        kds = [(k * jnp.exp(bl - b)).astype(bf16) for k, b, bl in zip(ks, bs, lasts)]
        if exact:
            ats = [intra_exact(b, q, k) for b, q, k in zip(bs, qs, ks)]
        else:
            kts = [(k * jnp.exp(-b)).astype(bf16) for k, b in zip(ks, bs)]
            ats = [lax.dot_general(qt, kt, NT_DIMS, preferred_element_type=f32) for qt, kt in zip(qts, kts)]
        updates = [lax.dot_general(v_ref[0, sl, :], kd, TN_DIMS, preferred_element_type=f32)
                   for sl, kd in zip(sls, kds)]
        intras = [jnp.dot(jnp.where(causal, a, 0.0).astype(bf16), v_ref[0, sl, :], preferred_element_type=f32)
                  for sl, a in zip(sls, ats)]
        states = [state_sc[...]]
        for bl, update in zip(lasts, updates):
            states.append(states[-1] * jnp.exp(bl) + update)
        state_sc[...] = states[-1]
        inters = [lax.dot_general(qt, st.astype(bf16), NT_DIMS, preferred_element_type=f32)
                  for qt, st in zip(qts, states[:-1])]
        for sl, o_intra, o_inter in zip(sls, intras, inters):
            rr = r_ref[0, sl, :].astype(f32)
            o_ref[0, sl, :] = (_rms(o_intra + o_inter, hg_ref[...]) * (rr * jax.nn.sigmoid(rr))).astype(bf16)

    @pl.when(jnp.logical_not(steep))
    def _():
        step(False)

    @pl.when(steep)
    def _():
        step(True)


def _gla(pqk, pvr, pla, head_g, tb=1024, chunk=128):
    bsz, s, _ = pqk.shape
    return pl.pallas_call(
        functools.partial(_gla_kernel, chunk=chunk),
        grid=(bsz, C_HEADS, s // tb),
        in_specs=[
            pl.BlockSpec((1, tb, C_DK), lambda b, h, i: (b, i, h)),
            pl.BlockSpec((1, tb, C_DK), lambda b, h, i: (b, i, C_HEADS + h)),
            pl.BlockSpec((1, tb, C_DV), lambda b, h, i: (b, i, h)),
            pl.BlockSpec((1, tb, C_DV), lambda b, h, i: (b, i, C_HEADS + h)),
            pl.BlockSpec((1, tb, C_DK), lambda b, h, i: (b, i, h)),
            pl.BlockSpec((1, C_DV), lambda b, h, i: (0, 0)),
        ],
        out_specs=pl.BlockSpec((1, tb, C_DV), lambda b, h, i: (b, i, h)),
        out_shape=jax.ShapeDtypeStruct((bsz, s, C_HEADS * C_DV), bf16),
        scratch_shapes=[pltpu.VMEM((C_DV, C_DK), f32),
                        pltpu.VMEM((chunk, C_DK), f32), pltpu.VMEM((chunk, C_DK), f32)],
        compiler_params=_params("parallel", "parallel", "arbitrary"),
        name="gla",
    )(pqk, pqk, pvr, pvr, pla, head_g)


def _post_kernel(x_ref, ya_ref, yb_ref, wo_ref, gate1_ref, sc2_ref, sh2_ref, gate2_ref, ng_ref,
                 wg_ref, wu_ref, wd_ref, o_ref, *, hc):
    half = ya_ref.shape[-1]
    y = (jnp.dot(ya_ref[0], wo_ref[0:half, :], preferred_element_type=f32)
         + jnp.dot(yb_ref[0], wo_ref[half:2 * half, :], preferred_element_type=f32))
    x1 = x_ref[0] + (1.0 + gate1_ref[0]) * _rms(y, ng_ref[0:1, :])
    hb = (_rms(x1, ng_ref[1:2, :]) * (1.0 + sc2_ref[0]) + sh2_ref[0]).astype(bf16)
    acc = jnp.zeros(x1.shape, f32)
    hidden = wg_ref.shape[1]
    for c in range(0, hidden, hc):
        e = min(c + hc, hidden)
        gt = jnp.dot(hb, wg_ref[:, c:e], preferred_element_type=f32)
        up = jnp.dot(hb, wu_ref[:, c:e], preferred_element_type=f32)
        act = (gt * jax.nn.sigmoid(gt) * up).astype(bf16)
        acc = acc + jnp.dot(act, wd_ref[c:e, :], preferred_element_type=f32)
    o_ref[0] = x1 + (1.0 + gate2_ref[0]) * _rms(acc, ng_ref[2:3, :])


def _post(x, ya, yb, cols, wo, gate1, sc2, sh2, gate2, ng, wg, wu, wd, tm=512, hc=512):
    bsz, s, d = x.shape
    half = wo.shape[0] // 2
    row = lambda b, i: (b, i, 0)
    per_b = lambda b, i: (b, 0, 0)
    mod = pl.BlockSpec((1, 1, d), per_b)
    return pl.pallas_call(
        functools.partial(_post_kernel, hc=hc),
        grid=(bsz, s // tm),
        in_specs=[
            pl.BlockSpec((1, tm, d), row),
            pl.BlockSpec((1, tm, half), lambda b, i: (b, i, cols[0])),
            pl.BlockSpec((1, tm, half), lambda b, i: (b, i, cols[1])),
            _resident(wo.shape),
            mod, mod, mod, mod,
            _resident(ng.shape),
            _resident(wg.shape), _resident(wu.shape), _resident(wd.shape),
        ],
        out_specs=pl.BlockSpec((1, tm, d), row),
        out_shape=jax.ShapeDtypeStruct((bsz, s, d), f32),
        compiler_params=_params("parallel", "parallel"),
        name="post",
    )(x, ya, yb, wo, gate1, sc2, sh2, gate2, ng, wg, wu, wd)


def _alibi_slopes(n):
    return 2.0 ** (-8.0 * jnp.arange(1, n + 1, dtype=f32) / n)


def kernel(x, c, ada_w, ada_b, norm_g, ev_w_in, ev_lambda, ev_subln_g, ev_w_out, od_w_in, od_w_g2,
           od_b_g2, od_head_g, od_w_out, ffn_w_gate, ffn_w_up, ffn_w_down):
    depth = ada_w.shape[0]
    mod = _adaln(c, ada_w, ada_b)
    for l in range(depth):
        sh1, sc1, g1, sh2, sc2, g2 = (mod[l, :, j] for j in range(6))
        pre_g = norm_g[l, 0:1]
        if l % 2 == 0:
            e = l // 2
            lam_init = 0.8 - 0.6 * math.exp(-0.3 * l)
            pa, pb = _inproj_even(x, pre_g, sc1, sh1, ev_w_in[e].astype(bf16))
            ya = _diff_attn(pa, _alibi_slopes(A_HEADS), ev_lambda[e], ev_subln_g[e][:, None], lam_init)
            yb = _dilated_attn(pb, _alibi_slopes(B_HEADS))
            cols = (0, 0)
            wo = ev_w_out[e]
        else:
            o = l // 2
            w = od_w_in[o]
            nmain = w.shape[1] - C_GATE_RANK
            wg1 = jnp.zeros((w.shape[0], LANES), bf16).at[:, :C_GATE_RANK].set(w[:, nmain:].astype(bf16))
            wg2 = jnp.zeros((LANES, od_w_g2.shape[-1]), bf16).at[:C_GATE_RANK].set(od_w_g2[o].astype(bf16))
            pqk, pvr, pla = _inproj_odd(x, pre_g, sc1, sh1, w[:, :nmain].astype(bf16), wg1, wg2, od_b_g2[o][None, :])
            ya = yb = _gla(pqk, pvr, pla, od_head_g[o][None, :])
            cols = (0, 1)
            wo = od_w_out[o]
        x = _post(x, ya, yb, cols, wo.astype(bf16), g1, sc2, sh2, g2, norm_g[l, 1:4],
                  ffn_w_gate[l].astype(bf16), ffn_w_up[l].astype(bf16), ffn_w_down[l].astype(bf16))
    return x
```

```python
import functools
import math

import jax
import jax.numpy as jnp
from jax import lax
from jax.experimental import pallas as pl
from jax.experimental.pallas import tpu as pltpu

f32 = jnp.float32
bf16 = jnp.bfloat16

EPS = 1e-6
LANES = 128
HEAD_DIM = 64
A_HEADS = 4
B_HEADS = 8
DILATIONS = (1, 4, 16)
BAND = 128
BLOCKS_IN_FLIGHT = 16
C_HEADS = 4
C_DK = 128
C_DV = 256
C_GATE_RANK = 16
C_TAU = 16.0
VMEM_LIMIT = 56 * 1024 * 1024
LOG2E = math.log2(math.e)
ONES_ROWS = 16

NT_DIMS = (((1,), (1,)), ((), ()))
TN_DIMS = (((0,), (0,)), ((), ()))


def _rms(x, g):
    return x * lax.rsqrt(jnp.mean(x * x, axis=-1, keepdims=True) + EPS) * g


def _params(*sem):
    return pltpu.CompilerParams(dimension_semantics=sem, vmem_limit_bytes=VMEM_LIMIT)


def _resident(shape):
    nd = len(shape)
    return pl.BlockSpec(shape, lambda *_: (0,) * nd, pipeline_mode=pl.Buffered(1))


def _adaln_kernel(c_ref, w_ref, b_ref, o_ref):
    c = c_ref[...]
    a = (c * jax.nn.sigmoid(c)).astype(bf16)
    o_ref[0] = jnp.dot(a, w_ref[0].astype(bf16), preferred_element_type=f32) + b_ref[0]


def _adaln(c, ada_w, ada_b):
    depth, d, n = ada_w.shape
    bsz = c.shape[0]
    rows = 8
    cp = jnp.zeros((rows, d), f32).at[:bsz].set(c)
    tn = 768
    out = pl.pallas_call(
        _adaln_kernel,
        grid=(depth, n // tn),
        in_specs=[
            pl.BlockSpec((rows, d), lambda l, j: (0, 0)),
            pl.BlockSpec((1, d, tn), lambda l, j: (l, 0, j)),
            pl.BlockSpec((1, 1, tn), lambda l, j: (l, 0, j)),
        ],
        out_specs=pl.BlockSpec((1, rows, tn), lambda l, j: (l, 0, j)),
        out_shape=jax.ShapeDtypeStruct((depth, rows, n), f32),
        compiler_params=_params("parallel", "parallel"),
        name="adaln",
    )(cp, ada_w, ada_b.reshape(depth, 1, n))
    return out[:, :bsz].reshape(depth, bsz, 6, 1, d)


def _modulated(x_ref, g_ref, sc_ref, sh_ref):
    h = _rms(x_ref[0], g_ref[...])
    return (h * (1.0 + sc_ref[0]) + sh_ref[0]).astype(bf16)


def _inproj_even_kernel(x_ref, g_ref, sc_ref, sh_ref, w_ref, oa_ref, ob_ref, *, nc):
    hb = _modulated(x_ref, g_ref, sc_ref, sh_ref)
    na = oa_ref.shape[-1]
    for c in range(0, na, nc):
        oa_ref[0, :, c:c + nc] = jnp.dot(hb, w_ref[:, c:c + nc], preferred_element_type=f32).astype(bf16)
    for c in range(0, ob_ref.shape[-1], nc):
        ob_ref[0, :, c:c + nc] = jnp.dot(hb, w_ref[:, na + c:na + c + nc], preferred_element_type=f32)


def _inproj_even(x, g, sc, sh, w, tm=512):
    bsz, s, d = x.shape
    na = 3 * A_HEADS * 2 * HEAD_DIM
    nb = w.shape[1] - na
    row = lambda b, i: (b, i, 0)
    per_b = lambda b, i: (b, 0, 0)
    return pl.pallas_call(
        functools.partial(_inproj_even_kernel, nc=512),
        grid=(bsz, s // tm),
        in_specs=[
            pl.BlockSpec((1, tm, d), row),
            pl.BlockSpec((1, d), lambda b, i: (0, 0)),
            pl.BlockSpec((1, 1, d), per_b),
            pl.BlockSpec((1, 1, d), per_b),
            _resident(w.shape),
        ],
        out_specs=[pl.BlockSpec((1, tm, na), row), pl.BlockSpec((1, tm, nb), row)],
        out_shape=[jax.ShapeDtypeStruct((bsz, s, na), bf16), jax.ShapeDtypeStruct((bsz, s, nb), f32)],
        compiler_params=_params("parallel", "parallel"),
        name="inproj_even",
    )(x, g, sc, sh, w)


def _inproj_odd_kernel(x_ref, g_ref, sc_ref, sh_ref, w_ref, wg1_ref, wg2_ref, bg2_ref,
                       oqk_ref, ovr_ref, ola_ref, *, nc):
    hb = _modulated(x_ref, g_ref, sc_ref, sh_ref)
    nqk = oqk_ref.shape[-1]
    for c in range(0, nqk, nc):
        oqk_ref[0, :, c:c + nc] = jnp.dot(hb, w_ref[:, c:c + nc], preferred_element_type=f32).astype(bf16)
    for c in range(0, ovr_ref.shape[-1], nc):
        ovr_ref[0, :, c:c + nc] = jnp.dot(hb, w_ref[:, nqk + c:nqk + c + nc], preferred_element_type=f32).astype(bf16)
    glr = jnp.dot(hb, wg1_ref[...], preferred_element_type=f32)
    z = jnp.dot(glr.astype(bf16), wg2_ref[...], preferred_element_type=f32) + bg2_ref[...]
    ola_ref[0] = (jnp.minimum(z, 0.0) - jnp.log(1.0 + jnp.exp(-jnp.abs(z)))) * (1.0 / C_TAU)


def _inproj_odd(x, g, sc, sh, w, wg1, wg2, bg2, tm=512):
    bsz, s, d = x.shape
    nqk = 2 * C_HEADS * C_DK
    nvr = 2 * C_HEADS * C_DV
    nla = C_HEADS * C_DK
    row = lambda b, i: (b, i, 0)
    per_b = lambda b, i: (b, 0, 0)
    return pl.pallas_call(
        functools.partial(_inproj_odd_kernel, nc=512),
        grid=(bsz, s // tm),
        in_specs=[
            pl.BlockSpec((1, tm, d), row),
            pl.BlockSpec((1, d), lambda b, i: (0, 0)),
            pl.BlockSpec((1, 1, d), per_b),
            pl.BlockSpec((1, 1, d), per_b),
            _resident(w.shape),
            _resident(wg1.shape),
            _resident(wg2.shape),
            _resident(bg2.shape),
        ],
        out_specs=[pl.BlockSpec((1, tm, nqk), row), pl.BlockSpec((1, tm, nvr), row), pl.BlockSpec((1, tm, nla), row)],
        out_shape=[jax.ShapeDtypeStruct((bsz, s, nqk), bf16), jax.ShapeDtypeStruct((bsz, s, nvr), bf16),
                   jax.ShapeDtypeStruct((bsz, s, nla), f32)],
        compiler_params=_params("parallel", "parallel"),
        name="inproj_odd",
    )(x, g, sc, sh, w, wg1, wg2, bg2)


def _diff_attn_kernel(slope_ref, lam_ref, q_ref, k_ref, v_ref, g_ref, o_ref,
                      ks_sc, vt_sc, bias_sc, sta_sc, stb_sc, cma_sc, cmb_sc, m_sc, acc_sc, *, t, lam_init):
    h = pl.program_id(1)
    i = pl.program_id(2)
    nblk = k_ref.shape[1] // t
    width = 2 * HEAD_DIM
    slope2 = slope_ref[h] * LOG2E

    @pl.when(i == 0)
    def _():
        lane = lax.broadcasted_iota(jnp.int32, (t, LANES), 1)
        first_half = lane < HEAD_DIM
        ones_row = jnp.where(lax.broadcasted_iota(jnp.int32, (ONES_ROWS, t), 0) == 0, 1.0, 0.0).astype(bf16)
        for c in range(nblk):
            kb = k_ref[0, c * t:(c + 1) * t, :]
            zero = jnp.zeros_like(kb)
            ks_sc[c, 0:t, :] = jnp.where(first_half, kb, zero)
            ks_sc[c, t:2 * t, :] = jnp.where(first_half, zero, kb)
            vt_sc[c, 0:width, :] = v_ref[0, c * t:(c + 1) * t, :].astype(f32).T.astype(bf16)
            vt_sc[c, width:width + ONES_ROWS, :] = ones_row
        kr = lax.broadcasted_iota(jnp.int32, (t, t), 0)
        qc = lax.broadcasted_iota(jnp.int32, (t, t), 1)
        bias_sc[...] = slope2 * (qc - kr).astype(f32)

    qt = (q_ref[0].astype(f32) * (HEAD_DIM ** -0.5 * LOG2E)).T.astype(bf16)

    def scores_into(st_ref, cm_ref, j):
        st = jnp.dot(ks_sc[j], qt, preferred_element_type=f32)
        for c in range(2):
            s = st[c * t:(c + 1) * t] - bias_sc[...]
            st_ref[c * t:(c + 1) * t, :] = s
            cm_ref[c] = jnp.max(s, axis=0, keepdims=True)

    def softmax_pv(st_ref, cm_ref, j, diagonal):
        vt = vt_sc[j]
        shift = slope2 * ((i - j) * t).astype(f32)
        for c in range(2):
            s = st_ref[c * t:(c + 1) * t, :]
            if diagonal:
                kr = lax.broadcasted_iota(jnp.int32, (t, t), 0)
                qc = lax.broadcasted_iota(jnp.int32, (t, t), 1)
                s = jnp.where(kr <= qc, s, -jnp.inf)
                cmax = jnp.max(s, axis=0, keepdims=True)
            else:
                cmax = cm_ref[c]
            m_old = m_sc[c]
            m_new = jnp.maximum(m_old, cmax - shift)
            p = jnp.exp2(s - (m_new + shift)).astype(bf16)
            pv = jnp.dot(vt, p, preferred_element_type=f32)
            acc_sc[c] = jnp.exp2(m_old - m_new) * acc_sc[c] + pv
            m_sc[c] = m_new

    m_sc[...] = jnp.full(m_sc.shape, -jnp.inf, f32)
    acc_sc[...] = jnp.zeros(acc_sc.shape, f32)
    scores_into(sta_sc, cma_sc, 0)

    def body(jj, carry):
        j = 2 * jj
        scores_into(stb_sc, cmb_sc, j + 1)
        softmax_pv(sta_sc, cma_sc, j, False)
        scores_into(sta_sc, cma_sc, j + 2)
        softmax_pv(stb_sc, cmb_sc, j + 1, False)
        return carry

    lax.fori_loop(0, i // 2, body, 0)

    @pl.when(i % 2 == 0)
    def _():
        softmax_pv(sta_sc, cma_sc, i, True)

    @pl.when(i % 2 == 1)
    def _():
        scores_into(stb_sc, cmb_sc, i)
        softmax_pv(sta_sc, cma_sc, i - 1, False)
        softmax_pv(stb_sc, cmb_sc, i, True)

    lp = lam_ref[...]
    lam = (jnp.exp(jnp.sum(lp[0:1] * lp[1:2], axis=-1, keepdims=True))
           - jnp.exp(jnp.sum(lp[2:3] * lp[3:4], axis=-1, keepdims=True)) + lam_init)
    outs = [acc_sc[c, 0:width, :] / acc_sc[c, width:width + 1, :] for c in range(2)]
    ot = outs[0] - lam * outs[1]
    ot = ot * lax.rsqrt(jnp.mean(ot * ot, axis=0, keepdims=True) + EPS) * (g_ref[...] * (1.0 - lam_init))
    o_ref[0] = ot.T.astype(bf16)


def _diff_attn(pa, slopes, lam_p, subln_g, lam_init, t=512):
    bsz, s, _ = pa.shape
    width = 2 * HEAD_DIM
    nblk = s // t
    return pl.pallas_call(
        functools.partial(_diff_attn_kernel, t=t, lam_init=lam_init),
        grid=(bsz, A_HEADS, nblk),
        in_specs=[
            pl.BlockSpec(memory_space=pltpu.SMEM),
            pl.BlockSpec(lam_p.shape, lambda b, h, i: (0, 0)),
            pl.BlockSpec((1, t, width), lambda b, h, i: (b, i, h)),
            pl.BlockSpec((1, s, width), lambda b, h, i: (b, 0, A_HEADS + h)),
            pl.BlockSpec((1, s, width), lambda b, h, i: (b, 0, 2 * A_HEADS + h)),
            pl.BlockSpec((width, 1), lambda b, h, i: (0, 0)),
        ],
        out_specs=pl.BlockSpec((1, t, width), lambda b, h, i: (b, i, h)),
        out_shape=jax.ShapeDtypeStruct((bsz, s, A_HEADS * width), bf16),
        scratch_shapes=[pltpu.VMEM((nblk, 2 * t, width), bf16),
                        pltpu.VMEM((nblk, width + ONES_ROWS, t), bf16),
                        pltpu.VMEM((t, t), f32),
                        pltpu.VMEM((2 * t, t), f32),
                        pltpu.VMEM((2 * t, t), f32),
                        pltpu.VMEM((2, 1, t), f32),
                        pltpu.VMEM((2, 1, t), f32),
                        pltpu.VMEM((2, 1, t), f32),
                        pltpu.VMEM((2, width + ONES_ROWS, t), f32)],
        compiler_params=_params("parallel", "parallel", "arbitrary"),
        name="diff_attn",
    )(slopes, lam_p, pa, pa, pa, subln_g)


def _dilated_kernel(slope_ref, q_ref, kp_ref, kc_ref, vp_ref, vc_ref, o_ref,
                    kcat, vcat, acc_sc, m_sc, l_sc, *, span):
    hp = pl.program_id(1)
    sidx = pl.program_id(2)
    blk = BAND
    nblocks = span // blk
    kcat[0:span] = kp_ref[0]
    kcat[span:2 * span] = kc_ref[0]
    vcat[0:span] = vp_ref[0]
    vcat[span:2 * span] = vc_ref[0]

    lane = lax.broadcasted_iota(jnp.int32, (blk, LANES), 1)
    head_a = lane < HEAD_DIM
    a = lax.broadcasted_iota(jnp.int32, (blk, 2 * blk), 0)
    bi = lax.broadcasted_iota(jnp.int32, (blk, 2 * blk), 1)
    rel = blk + a - bi
    valid = (rel >= 0) & (rel <= BAND)
    before_start = bi < blk
    relf = rel.astype(f32)
    slopes = (slope_ref[2 * hp], slope_ref[2 * hp + 1])
    for br, d in enumerate(DILATIONS):
        per_stream = nblocks // d
        biases = [jnp.where(valid, (sl * (LOG2E * d)) * relf, jnp.inf) for sl in slopes]

        def one_block(r, n, br=br, d=d, biases=biases):
            q_start = n * (blk * d) + r
            k_start = span + (n - 1) * (blk * d) + r
            qb = (q_ref[0, pl.ds(q_start, blk, stride=d), :] * (HEAD_DIM ** -0.5 * LOG2E)).astype(bf16)
            kk = kcat[pl.ds(k_start, 2 * blk, stride=d), :].astype(bf16)
            vv = vcat[pl.ds(k_start, 2 * blk, stride=d), :].astype(bf16)
            zero = jnp.zeros_like(qb)
            qh = (jnp.where(head_a, qb, zero), jnp.where(head_a, zero, qb))
            outs, ms, ls = [], [], []
            for c in range(2):
                s = lax.dot_general(qh[c], kk, NT_DIMS, preferred_element_type=f32) - biases[c]
                if n == 0:
                    s = jnp.where(jnp.logical_and(sidx == 0, before_start), -jnp.inf, s)
                m = jnp.max(s, axis=-1, keepdims=True)
                p = jnp.exp2(s - m)
                ls.append(jnp.sum(p, axis=-1, keepdims=True))
                ms.append(m)
                outs.append(jnp.dot(p.astype(bf16), vv, preferred_element_type=f32))
            dst = pl.ds(q_start, blk, stride=d)
            acc_sc[br, dst, :] = jnp.where(head_a, outs[0], outs[1])
            m_sc[br, dst, :] = jnp.where(head_a, ms[0], ms[1])
            l_sc[br, dst, :] = jnp.where(head_a, ls[0], ls[1])

        def one_stream(r, carry, per_stream=per_stream, one_block=one_block):
            for n in range(per_stream):
                one_block(r, n)
            return carry

        lax.fori_loop(0, d, one_stream, 0, unroll=max(1, min(d, BLOCKS_IN_FLIGHT // per_stream)))

    m_all = jnp.maximum(jnp.maximum(m_sc[0], m_sc[1]), m_sc[2])
    num = jnp.zeros((span, LANES), f32)
    den = jnp.zeros((span, LANES), f32)
    for br in range(len(DILATIONS)):
        w = jnp.exp2(m_sc[br] - m_all)
        num = num + w * acc_sc[br]
        den = den + w * l_sc[br]
    o_ref[0] = (num / den).astype(bf16)


def _dilated_attn(pb, slopes, span=2048):
    bsz, s, _ = pb.shape
    pairs = B_HEADS // 2
    cur = lambda off: (lambda b, h, i: (b, i, off + h))
    prev = lambda off: (lambda b, h, i: (b, jnp.maximum(i - 1, 0), off + h))
    blk = (1, span, LANES)
    nbr = len(DILATIONS)
    return pl.pallas_call(
        functools.partial(_dilated_kernel, span=span),
        grid=(bsz, pairs, s // span),
        in_specs=[
            pl.BlockSpec(memory_space=pltpu.SMEM),
            pl.BlockSpec(blk, cur(0)),
            pl.BlockSpec(blk, prev(pairs)),
            pl.BlockSpec(blk, cur(pairs)),
            pl.BlockSpec(blk, prev(2 * pairs)),
            pl.BlockSpec(blk, cur(2 * pairs)),
        ],
        out_specs=pl.BlockSpec(blk, cur(0)),
        out_shape=jax.ShapeDtypeStruct((bsz, s, B_HEADS * HEAD_DIM), bf16),
        scratch_shapes=[pltpu.VMEM((2 * span, LANES), f32), pltpu.VMEM((2 * span, LANES), f32),
                        pltpu.VMEM((nbr, span, LANES), f32), pltpu.VMEM((nbr, span, LANES), f32),
                        pltpu.VMEM((nbr, span, LANES), f32)],
        compiler_params=_params("parallel", "parallel", "parallel"),
        name="dilated_attn",
    )(slopes, pb, pb, pb, pb, pb)


def _gla_kernel(q_ref, k_ref, v_ref, r_ref, la_ref, hg_ref, o_ref, state_sc, b_sc, q_sc, *, chunk):
    tb = q_ref.shape[1]

    @pl.when(pl.program_id(2) == 0)
    def _():
        state_sc[...] = jnp.zeros_like(state_sc)

    row = lax.broadcasted_iota(jnp.int32, (chunk, chunk), 0)
    col = lax.broadcasted_iota(jnp.int32, (chunk, chunk), 1)
    causal = row >= col
    tril = jnp.where(causal, 1.0, 0.0).astype(bf16)
    steep = jnp.min(la_ref[0]) * chunk < -80.0

    def intra_exact(b, q, k):
        b_sc[...] = b
        q_sc[...] = q

        def one_row(i, at):
            bi = b_sc[pl.ds(i, 1), :]
            qi = q_sc[pl.ds(i, 1), :]
            g = jnp.exp(jnp.minimum(bi - b, 0.0)) * k * qi
            colv = jnp.sum(g, axis=-1, keepdims=True)
            return jnp.where(col == i, colv, at)

        return lax.fori_loop(0, chunk, one_row, jnp.zeros((chunk, chunk), f32)).T

    def step(exact):
        sls = [pl.ds(ci * chunk, chunk) for ci in range(tb // chunk)]
        bs = []
        for sl in sls:
            la = la_ref[0, sl, :]
            hi = la.astype(bf16)
            lo = (la - hi.astype(f32)).astype(bf16)
            bs.append(jnp.dot(tril, hi, preferred_element_type=f32)
                      + jnp.dot(tril, lo, preferred_element_type=f32))
        qs = [q_ref[0, sl, :].astype(f32) * (C_DK ** -0.5) for sl in sls]
        ks = [k_ref[0, sl, :].astype(f32) for sl in sls]
        qts = [(q * jnp.exp(b)).astype(bf16) for q, b in zip(qs, bs)]
        lasts = [b[chunk - 1:chunk, :] for b in bs]
        kds = [(k * jnp.exp(bl - b)).astype(bf16) for k, b, bl in zip(ks, bs, lasts)]
        if exact:
            ats = [intra_exact(b, q, k) for b, q, k in zip(bs, qs, ks)]
        else:
            kts = [(k * jnp.exp(-b)).astype(bf16) for k, b in zip(ks, bs)]
            ats = [lax.dot_general(qt, kt, NT_DIMS, preferred_element_type=f32) for qt, kt in zip(qts, kts)]
        updates = [lax.dot_general(v_ref[0, sl, :], kd, TN_DIMS, preferred_element_type=f32)
                   for sl, kd in zip(sls, kds)]
        intras = [jnp.dot(jnp.where(causal, a, 0.0).astype(bf16), v_ref[0, sl, :], preferred_element_type=f32)
                  for sl, a in zip(sls, ats)]
        states = [state_sc[...]]
        for bl, update in zip(lasts, updates):
            states.append(states[-1] * jnp.exp(bl) + update)
        state_sc[...] = states[-1]
        inters = [lax.dot_general(qt, st.astype(bf16), NT_DIMS, preferred_element_type=f32)
                  for qt, st in zip(qts, states[:-1])]
        for sl, o_intra, o_inter in zip(sls, intras, inters):
            rr = r_ref[0, sl, :].astype(f32)
            o_ref[0, sl, :] = (_rms(o_intra + o_inter, hg_ref[...]) * (rr * jax.nn.sigmoid(rr))).astype(bf16)

    @pl.when(jnp.logical_not(steep))
    def _():
        step(False)

    @pl.when(steep)
    def _():
        step(True)


def _gla(pqk, pvr, pla, head_g, tb=1024, chunk=128):
    bsz, s, _ = pqk.shape
    return pl.pallas_call(
        functools.partial(_gla_kernel, chunk=chunk),
        grid=(bsz, C_HEADS, s // tb),
        in_specs=[
            pl.BlockSpec((1, tb, C_DK), lambda b, h, i: (b, i, h)),
            pl.BlockSpec((1, tb, C_DK), lambda b, h, i: (b, i, C_HEADS + h)),
            pl.BlockSpec((1, tb, C_DV), lambda b, h, i: (b, i, h)),
            pl.BlockSpec((1, tb, C_DV), lambda b, h, i: (b, i, C_HEADS + h)),
            pl.BlockSpec((1, tb, C_DK), lambda b, h, i: (b, i, h)),
            pl.BlockSpec((1, C_DV), lambda b, h, i: (0, 0)),
        ],
        out_specs=pl.BlockSpec((1, tb, C_DV), lambda b, h, i: (b, i, h)),
        out_shape=jax.ShapeDtypeStruct((bsz, s, C_HEADS * C_DV), bf16),
        scratch_shapes=[pltpu.VMEM((C_DV, C_DK), f32),
                        pltpu.VMEM((chunk, C_DK), f32), pltpu.VMEM((chunk, C_DK), f32)],
        compiler_params=_params("parallel", "parallel", "arbitrary"),
        name="gla",
    )(pqk, pqk, pvr, pvr, pla, head_g)


def _post_kernel(x_ref, ya_ref, yb_ref, wo_ref, gate1_ref, sc2_ref, sh2_ref, gate2_ref, ng_ref,
                 wg_ref, wu_ref, wd_ref, o_ref, *, hc):
    half = ya_ref.shape[-1]
    y = (jnp.dot(ya_ref[0], wo_ref[0:half, :], preferred_element_type=f32)
         + jnp.dot(yb_ref[0], wo_ref[half:2 * half, :], preferred_element_type=f32))
    x1 = x_ref[0] + (1.0 + gate1_ref[0]) * _rms(y, ng_ref[0:1, :])
    hb = (_rms(x1, ng_ref[1:2, :]) * (1.0 + sc2_ref[0]) + sh2_ref[0]).astype(bf16)
    acc = jnp.zeros(x1.shape, f32)
    hidden = wg_ref.shape[1]
    for c in range(0, hidden, hc):
        e = min(c + hc, hidden)
        gt = jnp.dot(hb, wg_ref[:, c:e], preferred_element_type=f32)
        up = jnp.dot(hb, wu_ref[:, c:e], preferred_element_type=f32)
        act = (gt * jax.nn.sigmoid(gt) * up).astype(bf16)
        acc = acc + jnp.dot(act, wd_ref[c:e, :], preferred_element_type=f32)
    o_ref[0] = x1 + (1.0 + gate2_ref[0]) * _rms(acc, ng_ref[2:3, :])


def _post(x, ya, yb, cols, wo, gate1, sc2, sh2, gate2, ng, wg, wu, wd, tm=512, hc=512):
    bsz, s, d = x.shape
    half = wo.shape[0] // 2
    row = lambda b, i: (b, i, 0)
    per_b = lambda b, i: (b, 0, 0)
    mod = pl.BlockSpec((1, 1, d), per_b)
    return pl.pallas_call(
        functools.partial(_post_kernel, hc=hc),
        grid=(bsz, s // tm),
        in_specs=[
            pl.BlockSpec((1, tm, d), row),
            pl.BlockSpec((1, tm, half), lambda b, i: (b, i, cols[0])),
            pl.BlockSpec((1, tm, half), lambda b, i: (b, i, cols[1])),
            _resident(wo.shape),
            mod, mod, mod, mod,
            _resident(ng.shape),
            _resident(wg.shape), _resident(wu.shape), _resident(wd.shape),
        ],
        out_specs=pl.BlockSpec((1, tm, d), row),
        out_shape=jax.ShapeDtypeStruct((bsz, s, d), f32),
        compiler_params=_params("parallel", "parallel"),
        name="post",
    )(x, ya, yb, wo, gate1, sc2, sh2, gate2, ng, wg, wu, wd)


def _alibi_slopes(n):
    return 2.0 ** (-8.0 * jnp.arange(1, n + 1, dtype=f32) / n)


def kernel(x, c, ada_w, ada_b, norm_g, ev_w_in, ev_lambda, ev_subln_g, ev_w_out, od_w_in, od_w_g2,
           od_b_g2, od_head_g, od_w_out, ffn_w_gate, ffn_w_up, ffn_w_down):
    depth = ada_w.shape[0]
    mod = _adaln(c, ada_w, ada_b)
    for l in range(depth):
        sh1, sc1, g1, sh2, sc2, g2 = (mod[l, :, j] for j in range(6))
        pre_g = norm_g[l, 0:1]
        if l % 2 == 0:
            e = l // 2
            lam_init = 0.8 - 0.6 * math.exp(-0.3 * l)
            pa, pb = _inproj_even(x, pre_g, sc1, sh1, ev_w_in[e].astype(bf16))
            ya = _diff_attn(pa, _alibi_slopes(A_HEADS), ev_lambda[e], ev_subln_g[e][:, None], lam_init)
            yb = _dilated_attn(pb, _alibi_slopes(B_HEADS))
            cols = (0, 0)
            wo = ev_w_out[e]
        else:
            o = l // 2
            w = od_w_in[o]
            nmain = w.shape[1] - C_GATE_RANK
            wg1 = jnp.zeros((w.shape[0], LANES), bf16).at[:, :C_GATE_RANK].set(w[:, nmain:].astype(bf16))
            wg2 = jnp.zeros((LANES, od_w_g2.shape[-1]), bf16).at[:C_GATE_RANK].set(od_w_g2[o].astype(bf16))
            pqk, pvr, pla = _inproj_odd(x, pre_g, sc1, sh1, w[:, :nmain].astype(bf16), wg1, wg2, od_b_g2[o][None, :])
            ya = yb = _gla(pqk, pvr, pla, od_head_g[o][None, :])
            cols = (0, 1)
            wo = od_w_out[o]
        x = _post(x, ya, yb, cols, wo.astype(bf16), g1, sc2, sh2, g2, norm_g[l, 1:4],
                  ffn_w_gate[l].astype(bf16), ffn_w_up[l].astype(bf16), ffn_w_down[l].astype(bf16))
    return x
```

```python
import functools
import math

import jax
import jax.numpy as jnp
from jax import lax
from jax.experimental import pallas as pl
from jax.experimental.pallas import tpu as pltpu

f32 = jnp.float32
bf16 = jnp.bfloat16

EPS = 1e-6
LANES = 128
HEAD_DIM = 64
A_HEADS = 4
B_HEADS = 8
DILATIONS = (1, 4, 16)
BAND = 128
BLOCKS_IN_FLIGHT = 16
C_HEADS = 4
C_DK = 128
C_DV = 256
C_GATE_RANK = 16
C_TAU = 16.0
VMEM_LIMIT = 56 * 1024 * 1024
LOG2E = math.log2(math.e)
ONES_ROWS = 16

NT_DIMS = (((1,), (1,)), ((), ()))
TN_DIMS = (((0,), (0,)), ((), ()))


def _rms(x, g):
    return x * lax.rsqrt(jnp.mean(x * x, axis=-1, keepdims=True) + EPS) * g


def _params(*sem):
    return pltpu.CompilerParams(dimension_semantics=sem, vmem_limit_bytes=VMEM_LIMIT)


def _resident(shape):
    nd = len(shape)
    return pl.BlockSpec(shape, lambda *_: (0,) * nd, pipeline_mode=pl.Buffered(1))


def _adaln_kernel(c_ref, w_ref, b_ref, o_ref):
    c = c_ref[...]
    a = (c * jax.nn.sigmoid(c)).astype(bf16)
    o_ref[0] = jnp.dot(a, w_ref[0].astype(bf16), preferred_element_type=f32) + b_ref[0]


def _adaln(c, ada_w, ada_b):
    depth, d, n = ada_w.shape
    bsz = c.shape[0]
    rows = 8
    cp = jnp.zeros((rows, d), f32).at[:bsz].set(c)
    tn = 3072
    out = pl.pallas_call(
        _adaln_kernel,
        grid=(depth, n // tn),
        in_specs=[
            pl.BlockSpec((rows, d), lambda l, j: (0, 0)),
            pl.BlockSpec((1, d, tn), lambda l, j: (l, 0, j)),
            pl.BlockSpec((1, 1, tn), lambda l, j: (l, 0, j)),
        ],
        out_specs=pl.BlockSpec((1, rows, tn), lambda l, j: (l, 0, j)),
        out_shape=jax.ShapeDtypeStruct((depth, rows, n), f32),
        compiler_params=_params("parallel", "parallel"),
        name="adaln",
    )(cp, ada_w, ada_b.reshape(depth, 1, n))
    return out[:, :bsz].reshape(depth, bsz, 6, 1, d)


def _modulated(x_ref, g_ref, sc_ref, sh_ref):
    h = _rms(x_ref[0], g_ref[...])
    return (h * (1.0 + sc_ref[0]) + sh_ref[0]).astype(bf16)


def _inproj_even_kernel(x_ref, g_ref, sc_ref, sh_ref, w_ref, oa_ref, ob_ref, *, nc):
    hb = _modulated(x_ref, g_ref, sc_ref, sh_ref)
    na = oa_ref.shape[-1]
    for c in range(0, na, nc):
        oa_ref[0, :, c:c + nc] = jnp.dot(hb, w_ref[:, c:c + nc], preferred_element_type=f32).astype(bf16)
    for c in range(0, ob_ref.shape[-1], nc):
        ob_ref[0, :, c:c + nc] = jnp.dot(hb, w_ref[:, na + c:na + c + nc], preferred_element_type=f32)


def _inproj_even(x, g, sc, sh, w, tm=512):
    bsz, s, d = x.shape
    na = 3 * A_HEADS * 2 * HEAD_DIM
    nb = w.shape[1] - na
    row = lambda b, i: (b, i, 0)
    per_b = lambda b, i: (b, 0, 0)
    return pl.pallas_call(
        functools.partial(_inproj_even_kernel, nc=512),
        grid=(bsz, s // tm),
        in_specs=[
            pl.BlockSpec((1, tm, d), row),
            pl.BlockSpec((1, d), lambda b, i: (0, 0)),
            pl.BlockSpec((1, 1, d), per_b),
            pl.BlockSpec((1, 1, d), per_b),
            _resident(w.shape),
        ],
        out_specs=[pl.BlockSpec((1, tm, na), row), pl.BlockSpec((1, tm, nb), row)],
        out_shape=[jax.ShapeDtypeStruct((bsz, s, na), bf16), jax.ShapeDtypeStruct((bsz, s, nb), f32)],
        compiler_params=_params("parallel", "parallel"),
        name="inproj_even",
    )(x, g, sc, sh, w)


def _inproj_odd_kernel(x_ref, g_ref, sc_ref, sh_ref, w_ref, wg1_ref, wg2_ref, bg2_ref,
                       oqk_ref, ovr_ref, ola_ref, *, nc):
    hb = _modulated(x_ref, g_ref, sc_ref, sh_ref)
    glr = jnp.dot(hb, wg1_ref[...], preferred_element_type=f32)
    z = jnp.dot(glr.astype(bf16), wg2_ref[...], preferred_element_type=f32) + bg2_ref[...]
    ola_ref[0] = (jnp.minimum(z, 0.0) - jnp.log(1.0 + jnp.exp(-jnp.abs(z)))) * (1.0 / C_TAU)
    nqk = oqk_ref.shape[-1]
    for c in range(0, nqk, nc):
        oqk_ref[0, :, c:c + nc] = jnp.dot(hb, w_ref[:, c:c + nc], preferred_element_type=f32).astype(bf16)
    for c in range(0, ovr_ref.shape[-1], nc):
        ovr_ref[0, :, c:c + nc] = jnp.dot(hb, w_ref[:, nqk + c:nqk + c + nc], preferred_element_type=f32).astype(bf16)


def _inproj_odd(x, g, sc, sh, w, wg1, wg2, bg2, tm=512):
    bsz, s, d = x.shape
    nqk = 2 * C_HEADS * C_DK
    nvr = 2 * C_HEADS * C_DV
    nla = C_HEADS * C_DK
    row = lambda b, i: (b, i, 0)
    per_b = lambda b, i: (b, 0, 0)
    return pl.pallas_call(
        functools.partial(_inproj_odd_kernel, nc=512),
        grid=(bsz, s // tm),
        in_specs=[
            pl.BlockSpec((1, tm, d), row),
            pl.BlockSpec((1, d), lambda b, i: (0, 0)),
            pl.BlockSpec((1, 1, d), per_b),
            pl.BlockSpec((1, 1, d), per_b),
            _resident((d, nqk + nvr)),
            _resident(wg1.shape),
            _resident(wg2.shape),
            _resident(bg2.shape),
        ],
        out_specs=[pl.BlockSpec((1, tm, nqk), row), pl.BlockSpec((1, tm, nvr), row), pl.BlockSpec((1, tm, nla), row)],
        out_shape=[jax.ShapeDtypeStruct((bsz, s, nqk), bf16), jax.ShapeDtypeStruct((bsz, s, nvr), bf16),
                   jax.ShapeDtypeStruct((bsz, s, nla), f32)],
        compiler_params=_params("parallel", "parallel"),
        name="inproj_odd",
    )(x, g, sc, sh, w, wg1, wg2, bg2)


def _diff_attn_kernel(slope_ref, lam_ref, q_ref, k_ref, v_ref, g_ref, o_ref,
                      ks_sc, vt_sc, bias_sc, sta_sc, stb_sc, cma_sc, cmb_sc, m_sc, acc_sc, *, t, lam_init):
    h = pl.program_id(1)
    i = pl.program_id(2)
    nblk = k_ref.shape[1] // t
    width = 2 * HEAD_DIM
    slope2 = slope_ref[h] * LOG2E

    @pl.when(i == 0)
    def _():
        lane = lax.broadcasted_iota(jnp.int32, (t, LANES), 1)
        first_half = lane < HEAD_DIM
        ones_row = jnp.where(lax.broadcasted_iota(jnp.int32, (ONES_ROWS, t), 0) == 0, 1.0, 0.0).astype(bf16)
        for c in range(nblk):
            kb = k_ref[0, c * t:(c + 1) * t, :]
            zero = jnp.zeros_like(kb)
            ks_sc[c, 0:t, :] = jnp.where(first_half, kb, zero)
            ks_sc[c, t:2 * t, :] = jnp.where(first_half, zero, kb)
            vt_sc[c, 0:width, :] = v_ref[0, c * t:(c + 1) * t, :].astype(f32).T.astype(bf16)
            vt_sc[c, width:width + ONES_ROWS, :] = ones_row
        kr = lax.broadcasted_iota(jnp.int32, (t, 2 * t), 0)
        qc = lax.broadcasted_iota(jnp.int32, (t, 2 * t), 1) % t
        bias_sc[...] = slope2 * (qc - kr).astype(f32)

    qt = (q_ref[0].astype(f32) * (HEAD_DIM ** -0.5 * LOG2E)).T.astype(bf16)
    lane = lax.broadcasted_iota(jnp.int32, (1, 2 * t), 1)
    one_block_further = jnp.where(lane >= t, slope2 * t, 0.0)

    def scores_into(st_ref, cm_ref, j, lo):
        st = jnp.dot(ks_sc[j], qt[:, lo:], preferred_element_type=f32)
        for c in range(2):
            s = st[c * t:(c + 1) * t] - bias_sc[:, lo:]
            st_ref[c * t:(c + 1) * t, lo:] = s
            cm_ref[c, :, lo:] = jnp.max(s, axis=0, keepdims=True)

    def softmax_pv(st_ref, cm_ref, j, lo, diagonal):
        vt = vt_sc[j]
        shift = slope2 * ((2 * i - j) * t).astype(f32) + one_block_further[:, lo:]
        for c in range(2):
            s = st_ref[c * t:(c + 1) * t, lo:]
            if diagonal:
                kr = lax.broadcasted_iota(jnp.int32, s.shape, 0)
                qc = lax.broadcasted_iota(jnp.int32, s.shape, 1)
                s = jnp.where(jnp.logical_or(qc >= t, kr <= qc), s, -jnp.inf)
                cmax = jnp.max(s, axis=0, keepdims=True)
            else:
                cmax = cm_ref[c, :, lo:]
            m_old = m_sc[c, :, lo:]
            m_new = jnp.maximum(m_old, cmax - shift)
            p = jnp.exp2(s - (m_new + shift)).astype(bf16)
            pv = jnp.dot(vt, p, preferred_element_type=f32)
            acc_sc[c, :, lo:] = jnp.exp2(m_old - m_new) * acc_sc[c, :, lo:] + pv
            m_sc[c, :, lo:] = m_new

    m_sc[...] = jnp.full(m_sc.shape, -jnp.inf, f32)
    acc_sc[...] = jnp.zeros(acc_sc.shape, f32)
    scores_into(sta_sc, cma_sc, 0, 0)

    def body(jj, carry):
        j = 2 * jj
        scores_into(stb_sc, cmb_sc, j + 1, 0)
        softmax_pv(sta_sc, cma_sc, j, 0, False)
        scores_into(sta_sc, cma_sc, j + 2, 0)
        softmax_pv(stb_sc, cmb_sc, j + 1, 0, False)
        return carry

    lax.fori_loop(0, i, body, 0)

    scores_into(stb_sc, cmb_sc, 2 * i + 1, t)
    softmax_pv(sta_sc, cma_sc, 2 * i, 0, True)
    softmax_pv(stb_sc, cmb_sc, 2 * i + 1, t, True)

    lp = lam_ref[...]
    lam = (jnp.exp(jnp.sum(lp[0:1] * lp[1:2], axis=-1, keepdims=True))
           - jnp.exp(jnp.sum(lp[2:3] * lp[3:4], axis=-1, keepdims=True)) + lam_init)
    outs = [acc_sc[c, 0:width, :] / acc_sc[c, width:width + 1, :] for c in range(2)]
    ot = outs[0] - lam * outs[1]
    ot = ot * lax.rsqrt(jnp.mean(ot * ot, axis=0, keepdims=True) + EPS) * (g_ref[...] * (1.0 - lam_init))
    o_ref[0] = ot.T.astype(bf16)


def _diff_attn(pa, slopes, lam_p, subln_g, lam_init, t=512):
    bsz, s, _ = pa.shape
    width = 2 * HEAD_DIM
    nblk = s // t
    return pl.pallas_call(
        functools.partial(_diff_attn_kernel, t=t, lam_init=lam_init),
        grid=(bsz, A_HEADS, nblk // 2),
        in_specs=[
            pl.BlockSpec(memory_space=pltpu.SMEM),
            pl.BlockSpec(lam_p.shape, lambda b, h, i: (0, 0)),
            pl.BlockSpec((1, 2 * t, width), lambda b, h, i: (b, i, h)),
            pl.BlockSpec((1, s, width), lambda b, h, i: (b, 0, A_HEADS + h)),
            pl.BlockSpec((1, s, width), lambda b, h, i: (b, 0, 2 * A_HEADS + h)),
            pl.BlockSpec((width, 1), lambda b, h, i: (0, 0)),
        ],
        out_specs=pl.BlockSpec((1, 2 * t, width), lambda b, h, i: (b, i, h)),
        out_shape=jax.ShapeDtypeStruct((bsz, s, A_HEADS * width), bf16),
        scratch_shapes=[pltpu.VMEM((nblk, 2 * t, width), bf16),
                        pltpu.VMEM((nblk, width + ONES_ROWS, t), bf16),
                        pltpu.VMEM((t, 2 * t), f32),
                        pltpu.VMEM((2 * t, 2 * t), f32),
                        pltpu.VMEM((2 * t, 2 * t), f32),
                        pltpu.VMEM((2, 1, 2 * t), f32),
                        pltpu.VMEM((2, 1, 2 * t), f32),
                        pltpu.VMEM((2, 1, 2 * t), f32),
                        pltpu.VMEM((2, width + ONES_ROWS, 2 * t), f32)],
        compiler_params=_params("parallel", "parallel", "arbitrary"),
        name="diff_attn",
    )(slopes, lam_p, pa, pa, pa, subln_g)


def _dilated_kernel(slope_ref, q_ref, kp_ref, kc_ref, vp_ref, vc_ref, o_ref,
                    kcat, vcat, acc_sc, m_sc, l_sc, *, span):
    hp = pl.program_id(1)
    sidx = pl.program_id(2)
    blk = BAND
    nblocks = span // blk
    kcat[0:span] = kp_ref[0]
    kcat[span:2 * span] = kc_ref[0]
    vcat[0:span] = vp_ref[0]
    vcat[span:2 * span] = vc_ref[0]

    lane = lax.broadcasted_iota(jnp.int32, (blk, LANES), 1)
    head_a = lane < HEAD_DIM
    a = lax.broadcasted_iota(jnp.int32, (blk, 2 * blk), 0)
    bi = lax.broadcasted_iota(jnp.int32, (blk, 2 * blk), 1)
    rel = blk + a - bi
    valid = (rel >= 0) & (rel <= BAND)
    before_start = bi < blk
    steps = jnp.where(valid, rel.astype(f32), jnp.inf)
    slopes = (slope_ref[2 * hp], slope_ref[2 * hp + 1])
    for br, d in enumerate(DILATIONS):
        per_stream = nblocks // d
        rates = [sl * (LOG2E * d) for sl in slopes]

        def one_block(r, n, br=br, d=d, rates=rates):
            q_start = n * (blk * d) + r
            k_start = span + (n - 1) * (blk * d) + r
            qb = (q_ref[0, pl.ds(q_start, blk, stride=d), :] * (HEAD_DIM ** -0.5 * LOG2E)).astype(bf16)
            kk = kcat[pl.ds(k_start, 2 * blk, stride=d), :].astype(bf16)
            vv = vcat[pl.ds(k_start, 2 * blk, stride=d), :].astype(bf16)
            zero = jnp.zeros_like(qb)
            qh = (jnp.where(head_a, qb, zero), jnp.where(head_a, zero, qb))
            outs, ms, ls = [], [], []
            for c in range(2):
                s = lax.dot_general(qh[c], kk, NT_DIMS, preferred_element_type=f32) - rates[c] * steps
                if n == 0:
                    s = jnp.where(jnp.logical_and(sidx == 0, before_start), -jnp.inf, s)
                m = jnp.max(s, axis=-1, keepdims=True)
                p = jnp.exp2(s - m)
                ls.append(jnp.sum(p, axis=-1, keepdims=True))
                ms.append(m)
                outs.append(jnp.dot(p.astype(bf16), vv, preferred_element_type=f32))
            dst = pl.ds(q_start, blk, stride=d)
            acc_sc[br, dst, :] = jnp.where(head_a, outs[0], outs[1])
            m_sc[br, dst, :] = jnp.where(head_a, ms[0], ms[1])
            l_sc[br, dst, :] = jnp.where(head_a, ls[0], ls[1])

        def one_stream(r, carry, per_stream=per_stream, one_block=one_block):
            for n in range(per_stream):
                one_block(r, n)
            return carry

        lax.fori_loop(0, d, one_stream, 0, unroll=max(1, min(d, BLOCKS_IN_FLIGHT // per_stream)))

    m_all = jnp.maximum(jnp.maximum(m_sc[0], m_sc[1]), m_sc[2])
    num = jnp.zeros((span, LANES), f32)
    den = jnp.zeros((span, LANES), f32)
    for br in range(len(DILATIONS)):
        w = jnp.exp2(m_sc[br] - m_all)
        num = num + w * acc_sc[br]
        den = den + w * l_sc[br]
    o_ref[0] = (num / den).astype(bf16)


def _dilated_attn(pb, slopes, span=2048):
    bsz, s, _ = pb.shape
    pairs = B_HEADS // 2
    cur = lambda off: (lambda b, h, i: (b, i, off + h))
    prev = lambda off: (lambda b, h, i: (b, jnp.maximum(i - 1, 0), off + h))
    blk = (1, span, LANES)
    nbr = len(DILATIONS)
    return pl.pallas_call(
        functools.partial(_dilated_kernel, span=span),
        grid=(bsz, pairs, s // span),
        in_specs=[
            pl.BlockSpec(memory_space=pltpu.SMEM),
            pl.BlockSpec(blk, cur(0)),
            pl.BlockSpec(blk, prev(pairs)),
            pl.BlockSpec(blk, cur(pairs)),
            pl.BlockSpec(blk, prev(2 * pairs)),
            pl.BlockSpec(blk, cur(2 * pairs)),
        ],
        out_specs=pl.BlockSpec(blk, cur(0)),
        out_shape=jax.ShapeDtypeStruct((bsz, s, B_HEADS * HEAD_DIM), bf16),
        scratch_shapes=[pltpu.VMEM((2 * span, LANES), f32), pltpu.VMEM((2 * span, LANES), f32),
                        pltpu.VMEM((nbr, span, LANES), f32), pltpu.VMEM((nbr, span, LANES), f32),
                        pltpu.VMEM((nbr, span, LANES), f32)],
        compiler_params=_params("parallel", "parallel", "parallel"),
        name="dilated_attn",
    )(slopes, pb, pb, pb, pb, pb)


def _gla_kernel(q_ref, k_ref, v_ref, r_ref, la_ref, hg_ref, o_ref, state_sc, b_sc, q_sc, *, chunk):
    tb = q_ref.shape[1]

    @pl.when(pl.program_id(2) == 0)
    def _():
        state_sc[...] = jnp.zeros_like(state_sc)

    row = lax.broadcasted_iota(jnp.int32, (chunk, chunk), 0)
    col = lax.broadcasted_iota(jnp.int32, (chunk, chunk), 1)
    causal = row >= col
    tril = jnp.where(causal, 1.0, 0.0).astype(bf16)
    steep = jnp.min(la_ref[0]) * chunk < -80.0

    def intra_exact(b, q, k):
        b_sc[...] = b
        q_sc[...] = q

        def one_row(i, at):
            bi = b_sc[pl.ds(i, 1), :]
            qi = q_sc[pl.ds(i, 1), :]
            g = jnp.exp(jnp.minimum(bi - b, 0.0)) * k * qi
            colv = jnp.sum(g, axis=-1, keepdims=True)
            return jnp.where(col == i, colv, at)

        return lax.fori_loop(0, chunk, one_row, jnp.zeros((chunk, chunk), f32)).T

    def step(exact):
        sls = [pl.ds(ci * chunk, chunk) for ci in range(tb // chunk)]
        bs = []
        for sl in sls:
            la = la_ref[0, sl, :]
            hi = la.astype(bf16)
            lo = (la - hi.astype(f32)).astype(bf16)
            bs.append(jnp.dot(tril, hi, preferred_element_type=f32)
                      + jnp.dot(tril, lo, preferred_element_type=f32))
        qs = [q_ref[0, sl, :].astype(f32) * (C_DK ** -0.5) for sl in sls]
        ks = [k_ref[0, sl, :].astype(f32) for sl in sls]
        qts = [(q * jnp.exp(b)).astype(bf16) for q, b in zip(qs, bs)]
        lasts = [b[chunk - 1:chunk, :] for b in bs]
        kds = [(k * jnp.exp(bl - b)).astype(bf16) for k, b, bl in zip(ks, bs, lasts)]
        if exact:
            ats = [intra_exact(b, q, k) for b, q, k in zip(bs, qs, ks)]
        else:
            kts = [(k * jnp.exp(-b)).astype(bf16) for k, b in zip(ks, bs)]
            ats = [lax.dot_general(qt, kt, NT_DIMS, preferred_element_type=f32) for qt, kt in zip(qts, kts)]
        updates = [lax.dot_general(v_ref[0, sl, :], kd, TN_DIMS, preferred_element_type=f32)
                   for sl, kd in zip(sls, kds)]
        intras = [jnp.dot(jnp.where(causal, a, 0.0).astype(bf16), v_ref[0, sl, :], preferred_element_type=f32)
                  for sl, a in zip(sls, ats)]
        states = [state_sc[...]]
        for bl, update in zip(lasts, updates):
            states.append(states[-1] * jnp.exp(bl) + update)
        state_sc[...] = states[-1]
        inters = [lax.dot_general(qt, st.astype(bf16), NT_DIMS, preferred_element_type=f32)
                  for qt, st in zip(qts, states[:-1])]
        for sl, o_intra, o_inter in zip(sls, intras, inters):
            rr = r_ref[0, sl, :].astype(f32)
            o_ref[0, sl, :] = (_rms(o_intra + o_inter, hg_ref[...]) * (rr * jax.nn.sigmoid(rr))).astype(bf16)

    @pl.when(jnp.logical_not(steep))
    def _():
        step(False)

    @pl.when(steep)
    def _():
        step(True)


def _gla(pqk, pvr, pla, head_g, tb=1024, chunk=128):
    bsz, s, _ = pqk.shape
    return pl.pallas_call(
        functools.partial(_gla_kernel, chunk=chunk),
        grid=(bsz, C_HEADS, s // tb),
        in_specs=[
            pl.BlockSpec((1, tb, C_DK), lambda b, h, i: (b, i, h)),
            pl.BlockSpec((1, tb, C_DK), lambda b, h, i: (b, i, C_HEADS + h)),
            pl.BlockSpec((1, tb, C_DV), lambda b, h, i: (b, i, h)),
            pl.BlockSpec((1, tb, C_DV), lambda b, h, i: (b, i, C_HEADS + h)),
            pl.BlockSpec((1, tb, C_DK), lambda b, h, i: (b, i, h)),
            pl.BlockSpec((1, C_DV), lambda b, h, i: (0, 0)),
        ],
        out_specs=pl.BlockSpec((1, tb, C_DV), lambda b, h, i: (b, i, h)),
        out_shape=jax.ShapeDtypeStruct((bsz, s, C_HEADS * C_DV), bf16),
        scratch_shapes=[pltpu.VMEM((C_DV, C_DK), f32),
                        pltpu.VMEM((chunk, C_DK), f32), pltpu.VMEM((chunk, C_DK), f32)],
        compiler_params=_params("parallel", "parallel", "arbitrary"),
        name="gla",
    )(pqk, pqk, pvr, pvr, pla, head_g)


def _post_kernel(x_ref, ya_ref, yb_ref, wo_ref, gate1_ref, sc2_ref, sh2_ref, gate2_ref, ng_ref,
                 wg_ref, wu_ref, wd_ref, o_ref, *, hc):
    half = ya_ref.shape[-1]
    y = (jnp.dot(ya_ref[0], wo_ref[0:half, :], preferred_element_type=f32)
         + jnp.dot(yb_ref[0], wo_ref[half:2 * half, :], preferred_element_type=f32))
    x1 = x_ref[0] + (1.0 + gate1_ref[0]) * _rms(y, ng_ref[0:1, :])
    hb = (_rms(x1, ng_ref[1:2, :]) * (1.0 + sc2_ref[0]) + sh2_ref[0]).astype(bf16)
    acc = jnp.zeros(x1.shape, f32)
    hidden = wg_ref.shape[1]
    for c in range(0, hidden, hc):
        e = min(c + hc, hidden)
        gt = jnp.dot(hb, wg_ref[:, c:e], preferred_element_type=f32)
        up = jnp.dot(hb, wu_ref[:, c:e], preferred_element_type=f32)
        act = (gt * jax.nn.sigmoid(gt) * up).astype(bf16)
        acc = acc + jnp.dot(act, wd_ref[c:e, :], preferred_element_type=f32)
    o_ref[0] = x1 + (1.0 + gate2_ref[0]) * _rms(acc, ng_ref[2:3, :])


def _post(x, ya, yb, cols, wo, gate1, sc2, sh2, gate2, ng, wg, wu, wd, tm=512, hc=512):
    bsz, s, d = x.shape
    half = wo.shape[0] // 2
    row = lambda b, i: (b, i, 0)
    per_b = lambda b, i: (b, 0, 0)
    mod = pl.BlockSpec((1, 1, d), per_b)
    return pl.pallas_call(
        functools.partial(_post_kernel, hc=hc),
        grid=(bsz, s // tm),
        in_specs=[
            pl.BlockSpec((1, tm, d), row),
            pl.BlockSpec((1, tm, half), lambda b, i: (b, i, cols[0])),
            pl.BlockSpec((1, tm, half), lambda b, i: (b, i, cols[1])),
            _resident(wo.shape),
            mod, mod, mod, mod,
            _resident(ng.shape),
            _resident(wg.shape), _resident(wu.shape), _resident(wd.shape),
        ],
        out_specs=pl.BlockSpec((1, tm, d), row),
        out_shape=jax.ShapeDtypeStruct((bsz, s, d), f32),
        compiler_params=_params("parallel", "parallel"),
        name="post",
    )(x, ya, yb, wo, gate1, sc2, sh2, gate2, ng, wg, wu, wd)


def _alibi_slopes(n):
    return 2.0 ** (-8.0 * jnp.arange(1, n + 1, dtype=f32) / n)


def kernel(x, c, ada_w, ada_b, norm_g, ev_w_in, ev_lambda, ev_subln_g, ev_w_out, od_w_in, od_w_g2,
           od_b_g2, od_head_g, od_w_out, ffn_w_gate, ffn_w_up, ffn_w_down):
    depth = ada_w.shape[0]
    mod = _adaln(c, ada_w, ada_b)
    for l in range(depth):
        sh1, sc1, g1, sh2, sc2, g2 = (mod[l, :, j] for j in range(6))
        pre_g = norm_g[l, 0:1]
        if l % 2 == 0:
            e = l // 2
            lam_init = 0.8 - 0.6 * math.exp(-0.3 * l)
            pa, pb = _inproj_even(x, pre_g, sc1, sh1, ev_w_in[e].astype(bf16))
            ya = _diff_attn(pa, _alibi_slopes(A_HEADS), ev_lambda[e], ev_subln_g[e][:, None], lam_init)
            yb = _dilated_attn(pb, _alibi_slopes(B_HEADS))
            cols = (0, 0)
            wo = ev_w_out[e]
        else:
            o = l // 2
            w = od_w_in[o].astype(bf16)
            nmain = w.shape[1] - C_GATE_RANK
            wg1 = jnp.zeros((w.shape[0], LANES), bf16).at[:, :C_GATE_RANK].set(w[:, nmain:])
            wg2 = jnp.zeros((LANES, od_w_g2.shape[-1]), bf16).at[:C_GATE_RANK].set(od_w_g2[o].astype(bf16))
            pqk, pvr, pla = _inproj_odd(x, pre_g, sc1, sh1, w, wg1, wg2, od_b_g2[o][None, :])
            ya = yb = _gla(pqk, pvr, pla, od_head_g[o][None, :])
            cols = (0, 1)
            wo = od_w_out[o]
        x = _post(x, ya, yb, cols, wo.astype(bf16), g1, sc2, sh2, g2, norm_g[l, 1:4],
                  ffn_w_gate[l].astype(bf16), ffn_w_up[l].astype(bf16), ffn_w_down[l].astype(bf16))
    return x
```

```python
import functools
import math

import jax
import jax.numpy as jnp
from jax import lax
from jax.experimental import pallas as pl
from jax.experimental.pallas import tpu as pltpu

f32 = jnp.float32
bf16 = jnp.bfloat16

EPS = 1e-6
LANES = 128
HEAD_DIM = 64
A_HEADS = 4
B_HEADS = 8
DILATIONS = (1, 4, 16)
BAND = 128
BLOCKS_IN_FLIGHT = 16
C_HEADS = 4
C_DK = 128
C_DV = 256
C_GATE_RANK = 16
C_TAU = 16.0
VMEM_LIMIT = 56 * 1024 * 1024
LOG2E = math.log2(math.e)
ONES_ROWS = 16

NT_DIMS = (((1,), (1,)), ((), ()))
TN_DIMS = (((0,), (0,)), ((), ()))


def _rms(x, g):
    return x * lax.rsqrt(jnp.mean(x * x, axis=-1, keepdims=True) + EPS) * g


def _params(*sem):
    return pltpu.CompilerParams(dimension_semantics=sem, vmem_limit_bytes=VMEM_LIMIT)


def _resident(shape):
    nd = len(shape)
    return pl.BlockSpec(shape, lambda *_: (0,) * nd, pipeline_mode=pl.Buffered(1))


def _adaln_kernel(c_ref, w_ref, b_ref, o_ref):
    c = c_ref[...]
    a = (c * jax.nn.sigmoid(c)).astype(bf16)
    o_ref[0] = jnp.dot(a, w_ref[0].astype(bf16), preferred_element_type=f32) + b_ref[0]


def _adaln(c, ada_w, ada_b):
    depth, d, n = ada_w.shape
    bsz = c.shape[0]
    rows = 8
    cp = jnp.zeros((rows, d), f32).at[:bsz].set(c)
    tn = 3072
    out = pl.pallas_call(
        _adaln_kernel,
        grid=(depth, n // tn),
        in_specs=[
            pl.BlockSpec((rows, d), lambda l, j: (0, 0)),
            pl.BlockSpec((1, d, tn), lambda l, j: (l, 0, j)),
            pl.BlockSpec((1, 1, tn), lambda l, j: (l, 0, j)),
        ],
        out_specs=pl.BlockSpec((1, rows, tn), lambda l, j: (l, 0, j)),
        out_shape=jax.ShapeDtypeStruct((depth, rows, n), f32),
        compiler_params=_params("parallel", "parallel"),
        name="adaln",
    )(cp, ada_w, ada_b.reshape(depth, 1, n))
    return out[:, :bsz].reshape(depth, bsz, 6, 1, d)


def _modulated(x_ref, g_ref, sc_ref, sh_ref):
    h = _rms(x_ref[0], g_ref[...])
    return (h * (1.0 + sc_ref[0]) + sh_ref[0]).astype(bf16)


def _inproj_even_kernel(x_ref, g_ref, sc_ref, sh_ref, w_ref, oa_ref, ob_ref, *, nc):
    hb = _modulated(x_ref, g_ref, sc_ref, sh_ref)
    na = oa_ref.shape[-1]
    for c in range(0, na, nc):
        oa_ref[0, :, c:c + nc] = jnp.dot(hb, w_ref[:, c:c + nc], preferred_element_type=f32).astype(bf16)
    for c in range(0, ob_ref.shape[-1], nc):
        ob_ref[0, :, c:c + nc] = jnp.dot(hb, w_ref[:, na + c:na + c + nc], preferred_element_type=f32)


def _inproj_even(x, g, sc, sh, w, tm=512):
    bsz, s, d = x.shape
    na = 3 * A_HEADS * 2 * HEAD_DIM
    nb = w.shape[1] - na
    row = lambda b, i: (b, i, 0)
    per_b = lambda b, i: (b, 0, 0)
    return pl.pallas_call(
        functools.partial(_inproj_even_kernel, nc=512),
        grid=(bsz, s // tm),
        in_specs=[
            pl.BlockSpec((1, tm, d), row),
            pl.BlockSpec((1, d), lambda b, i: (0, 0)),
            pl.BlockSpec((1, 1, d), per_b),
            pl.BlockSpec((1, 1, d), per_b),
            _resident(w.shape),
        ],
        out_specs=[pl.BlockSpec((1, tm, na), row), pl.BlockSpec((1, tm, nb), row)],
        out_shape=[jax.ShapeDtypeStruct((bsz, s, na), bf16), jax.ShapeDtypeStruct((bsz, s, nb), f32)],
        compiler_params=_params("parallel", "parallel"),
        name="inproj_even",
    )(x, g, sc, sh, w)


def _inproj_odd_kernel(x_ref, g_ref, sc_ref, sh_ref, w_ref, wg1_ref, wg2_ref, bg2_ref,
                       oqk_ref, ovr_ref, ola_ref, *, nc):
    hb = _modulated(x_ref, g_ref, sc_ref, sh_ref)
    glr = jnp.dot(hb, wg1_ref[...], preferred_element_type=f32)
    z = jnp.dot(glr.astype(bf16), wg2_ref[...], preferred_element_type=f32) + bg2_ref[...]
    ola_ref[0] = (jnp.minimum(z, 0.0) - jnp.log(1.0 + jnp.exp(-jnp.abs(z)))) * (1.0 / C_TAU)
    nqk = oqk_ref.shape[-1]
    for c in range(0, nqk, nc):
        oqk_ref[0, :, c:c + nc] = jnp.dot(hb, w_ref[:, c:c + nc], preferred_element_type=f32).astype(bf16)
    for c in range(0, ovr_ref.shape[-1], nc):
        ovr_ref[0, :, c:c + nc] = jnp.dot(hb, w_ref[:, nqk + c:nqk + c + nc], preferred_element_type=f32).astype(bf16)


def _inproj_odd(x, g, sc, sh, w, wg1, wg2, bg2, tm=512):
    bsz, s, d = x.shape
    nqk = 2 * C_HEADS * C_DK
    nvr = 2 * C_HEADS * C_DV
    nla = C_HEADS * C_DK
    row = lambda b, i: (b, i, 0)
    per_b = lambda b, i: (b, 0, 0)
    return pl.pallas_call(
        functools.partial(_inproj_odd_kernel, nc=512),
        grid=(bsz, s // tm),
        in_specs=[
            pl.BlockSpec((1, tm, d), row),
            pl.BlockSpec((1, d), lambda b, i: (0, 0)),
            pl.BlockSpec((1, 1, d), per_b),
            pl.BlockSpec((1, 1, d), per_b),
            _resident((d, nqk + nvr)),
            _resident(wg1.shape),
            _resident(wg2.shape),
            _resident(bg2.shape),
        ],
        out_specs=[pl.BlockSpec((1, tm, nqk), row), pl.BlockSpec((1, tm, nvr), row), pl.BlockSpec((1, tm, nla), row)],
        out_shape=[jax.ShapeDtypeStruct((bsz, s, nqk), bf16), jax.ShapeDtypeStruct((bsz, s, nvr), bf16),
                   jax.ShapeDtypeStruct((bsz, s, nla), f32)],
        compiler_params=_params("parallel", "parallel"),
        name="inproj_odd",
    )(x, g, sc, sh, w, wg1, wg2, bg2)


def _diff_attn_kernel(slope_ref, lam_ref, q_ref, k_ref, v_ref, g_ref, o_ref,
                      ks_sc, vt_sc, bias_sc, sta_sc, stb_sc, cma_sc, cmb_sc, m_sc, acc_sc, *, t, lam_init):
    h = pl.program_id(1)
    i = pl.program_id(2)
    nblk = k_ref.shape[1] // t
    width = 2 * HEAD_DIM
    slope2 = slope_ref[h] * LOG2E

    @pl.when(i == 0)
    def _():
        lane = lax.broadcasted_iota(jnp.int32, (t, LANES), 1)
        first_half = lane < HEAD_DIM
        ones_row = jnp.where(lax.broadcasted_iota(jnp.int32, (ONES_ROWS, t), 0) == 0, 1.0, 0.0).astype(bf16)
        for c in range(nblk):
            kb = k_ref[0, c * t:(c + 1) * t, :]
            zero = jnp.zeros_like(kb)
            ks_sc[c, 0:t, :] = jnp.where(first_half, kb, zero)
            ks_sc[c, t:2 * t, :] = jnp.where(first_half, zero, kb)
            vt_sc[c, 0:width, :] = v_ref[0, c * t:(c + 1) * t, :].astype(f32).T.astype(bf16)
            vt_sc[c, width:width + ONES_ROWS, :] = ones_row
        kr = lax.broadcasted_iota(jnp.int32, (t, 2 * t), 0)
        qc = lax.broadcasted_iota(jnp.int32, (t, 2 * t), 1) % t
        bias_sc[...] = slope2 * (qc - kr).astype(f32)

    qt = (q_ref[0].astype(f32) * (HEAD_DIM ** -0.5 * LOG2E)).T.astype(bf16)
    lane = lax.broadcasted_iota(jnp.int32, (1, 2 * t), 1)
    one_block_further = jnp.where(lane >= t, slope2 * t, 0.0)

    def scores_into(st_ref, cm_ref, j, lo):
        st = jnp.dot(ks_sc[j], qt[:, lo:], preferred_element_type=f32)
        for c in range(2):
            s = st[c * t:(c + 1) * t] - bias_sc[:, lo:]
            st_ref[c * t:(c + 1) * t, lo:] = s
            cm_ref[c, :, lo:] = jnp.max(s, axis=0, keepdims=True)

    def softmax_pv(st_ref, cm_ref, j, lo, diagonal):
        vt = vt_sc[j]
        shift = slope2 * ((2 * i - j) * t).astype(f32) + one_block_further[:, lo:]
        for c in range(2):
            s = st_ref[c * t:(c + 1) * t, lo:]
            if diagonal:
                kr = lax.broadcasted_iota(jnp.int32, s.shape, 0)
                qc = lax.broadcasted_iota(jnp.int32, s.shape, 1)
                s = jnp.where(jnp.logical_or(qc >= t, kr <= qc), s, -jnp.inf)
                cmax = jnp.max(s, axis=0, keepdims=True)
            else:
                cmax = cm_ref[c, :, lo:]
            m_old = m_sc[c, :, lo:]
            m_new = jnp.maximum(m_old, cmax - shift)
            p = jnp.exp2(s - (m_new + shift)).astype(bf16)
            pv = jnp.dot(vt, p, preferred_element_type=f32)
            acc_sc[c, :, lo:] = jnp.exp2(m_old - m_new) * acc_sc[c, :, lo:] + pv
            m_sc[c, :, lo:] = m_new

    m_sc[...] = jnp.full(m_sc.shape, -jnp.inf, f32)
    acc_sc[...] = jnp.zeros(acc_sc.shape, f32)
    scores_into(sta_sc, cma_sc, 0, 0)

    def body(jj, carry):
        j = 2 * jj
        scores_into(stb_sc, cmb_sc, j + 1, 0)
        softmax_pv(sta_sc, cma_sc, j, 0, False)
        scores_into(sta_sc, cma_sc, j + 2, 0)
        softmax_pv(stb_sc, cmb_sc, j + 1, 0, False)
        return carry

    lax.fori_loop(0, i, body, 0)

    scores_into(stb_sc, cmb_sc, 2 * i + 1, t)
    softmax_pv(sta_sc, cma_sc, 2 * i, 0, True)
    softmax_pv(stb_sc, cmb_sc, 2 * i + 1, t, True)

    lp = lam_ref[...]
    lam = (jnp.exp(jnp.sum(lp[0:1] * lp[1:2], axis=-1, keepdims=True))
           - jnp.exp(jnp.sum(lp[2:3] * lp[3:4], axis=-1, keepdims=True)) + lam_init)
    outs = [acc_sc[c, 0:width, :] / acc_sc[c, width:width + 1, :] for c in range(2)]
    ot = outs[0] - lam * outs[1]
    ot = ot * lax.rsqrt(jnp.mean(ot * ot, axis=0, keepdims=True) + EPS) * (g_ref[...] * (1.0 - lam_init))
    o_ref[0] = ot.T.astype(bf16)


def _diff_attn(pa, slopes, lam_p, subln_g, lam_init, t=512):
    bsz, s, _ = pa.shape
    width = 2 * HEAD_DIM
    nblk = s // t
    return pl.pallas_call(
        functools.partial(_diff_attn_kernel, t=t, lam_init=lam_init),
        grid=(bsz, A_HEADS, nblk // 2),
        in_specs=[
            pl.BlockSpec(memory_space=pltpu.SMEM),
            pl.BlockSpec(lam_p.shape, lambda b, h, i: (0, 0)),
            pl.BlockSpec((1, 2 * t, width), lambda b, h, i: (b, i, h)),
            pl.BlockSpec((1, s, width), lambda b, h, i: (b, 0, A_HEADS + h)),
            pl.BlockSpec((1, s, width), lambda b, h, i: (b, 0, 2 * A_HEADS + h)),
            pl.BlockSpec((width, 1), lambda b, h, i: (0, 0)),
        ],
        out_specs=pl.BlockSpec((1, 2 * t, width), lambda b, h, i: (b, i, h)),
        out_shape=jax.ShapeDtypeStruct((bsz, s, A_HEADS * width), bf16),
        scratch_shapes=[pltpu.VMEM((nblk, 2 * t, width), bf16),
                        pltpu.VMEM((nblk, width + ONES_ROWS, t), bf16),
                        pltpu.VMEM((t, 2 * t), f32),
                        pltpu.VMEM((2 * t, 2 * t), f32),
                        pltpu.VMEM((2 * t, 2 * t), f32),
                        pltpu.VMEM((2, 1, 2 * t), f32),
                        pltpu.VMEM((2, 1, 2 * t), f32),
                        pltpu.VMEM((2, 1, 2 * t), f32),
                        pltpu.VMEM((2, width + ONES_ROWS, 2 * t), f32)],
        compiler_params=_params("parallel", "parallel", "arbitrary"),
        name="diff_attn",
    )(slopes, lam_p, pa, pa, pa, subln_g)


def _dilated_kernel(slope_ref, q_ref, kp_ref, kc_ref, vp_ref, vc_ref, o_ref,
                    kcat, vcat, acc_sc, m_sc, l_sc, *, span):
    hp = pl.program_id(1)
    sidx = pl.program_id(2)
    blk = BAND
    nblocks = span // blk
    kcat[0:span] = kp_ref[0]
    kcat[span:2 * span] = kc_ref[0]
    vcat[0:span] = vp_ref[0]
    vcat[span:2 * span] = vc_ref[0]

    lane = lax.broadcasted_iota(jnp.int32, (blk, LANES), 1)
    head_a = lane < HEAD_DIM
    a = lax.broadcasted_iota(jnp.int32, (blk, 2 * blk), 0)
    bi = lax.broadcasted_iota(jnp.int32, (blk, 2 * blk), 1)
    rel = blk + a - bi
    valid = (rel >= 0) & (rel <= BAND)
    before_start = bi < blk
    steps = jnp.where(valid, rel.astype(f32), jnp.inf)
    slopes = (slope_ref[2 * hp], slope_ref[2 * hp + 1])
    for br, d in enumerate(DILATIONS):
        per_stream = nblocks // d
        rates = [sl * (LOG2E * d) for sl in slopes]

        def one_block(r, n, br=br, d=d, rates=rates):
            q_start = n * (blk * d) + r
            k_start = span + (n - 1) * (blk * d) + r
            qb = (q_ref[0, pl.ds(q_start, blk, stride=d), :] * (HEAD_DIM ** -0.5 * LOG2E)).astype(bf16)
            kk = kcat[pl.ds(k_start, 2 * blk, stride=d), :].astype(bf16)
            vv = vcat[pl.ds(k_start, 2 * blk, stride=d), :].astype(bf16)
            zero = jnp.zeros_like(qb)
            qh = (jnp.where(head_a, qb, zero), jnp.where(head_a, zero, qb))
            outs, ms, ls = [], [], []
            for c in range(2):
                s = lax.dot_general(qh[c], kk, NT_DIMS, preferred_element_type=f32) - rates[c] * steps
                if n == 0:
                    s = jnp.where(jnp.logical_and(sidx == 0, before_start), -jnp.inf, s)
                m = jnp.max(s, axis=-1, keepdims=True)
                p = jnp.exp2(s - m)
                ls.append(jnp.sum(p, axis=-1, keepdims=True))
                ms.append(m)
                outs.append(jnp.dot(p.astype(bf16), vv, preferred_element_type=f32))
            dst = pl.ds(q_start, blk, stride=d)
            acc_sc[br, dst, :] = jnp.where(head_a, outs[0], outs[1])
            m_sc[br, dst, :] = jnp.where(head_a, ms[0], ms[1])
            l_sc[br, dst, :] = jnp.where(head_a, ls[0], ls[1])

        def one_stream(r, carry, per_stream=per_stream, one_block=one_block):
            for n in range(per_stream):
                one_block(r, n)
            return carry

        lax.fori_loop(0, d, one_stream, 0, unroll=max(1, min(d, BLOCKS_IN_FLIGHT // per_stream)))

    m_all = jnp.maximum(jnp.maximum(m_sc[0], m_sc[1]), m_sc[2])
    num = jnp.zeros((span, LANES), f32)
    den = jnp.zeros((span, LANES), f32)
    for br in range(len(DILATIONS)):
        w = jnp.exp2(m_sc[br] - m_all)
        num = num + w * acc_sc[br]
        den = den + w * l_sc[br]
    o_ref[0] = (num / den).astype(bf16)


def _dilated_attn(pb, slopes, span=2048):
    bsz, s, _ = pb.shape
    pairs = B_HEADS // 2
    cur = lambda off: (lambda b, h, i: (b, i, off + h))
    prev = lambda off: (lambda b, h, i: (b, jnp.maximum(i - 1, 0), off + h))
    blk = (1, span, LANES)
    nbr = len(DILATIONS)
    return pl.pallas_call(
        functools.partial(_dilated_kernel, span=span),
        grid=(bsz, pairs, s // span),
        in_specs=[
            pl.BlockSpec(memory_space=pltpu.SMEM),
            pl.BlockSpec(blk, cur(0)),
            pl.BlockSpec(blk, prev(pairs)),
            pl.BlockSpec(blk, cur(pairs)),
            pl.BlockSpec(blk, prev(2 * pairs)),
            pl.BlockSpec(blk, cur(2 * pairs)),
        ],
        out_specs=pl.BlockSpec(blk, cur(0)),
        out_shape=jax.ShapeDtypeStruct((bsz, s, B_HEADS * HEAD_DIM), bf16),
        scratch_shapes=[pltpu.VMEM((2 * span, LANES), f32), pltpu.VMEM((2 * span, LANES), f32),
                        pltpu.VMEM((nbr, span, LANES), f32), pltpu.VMEM((nbr, span, LANES), f32),
                        pltpu.VMEM((nbr, span, LANES), f32)],
        compiler_params=_params("parallel", "parallel", "parallel"),
        name="dilated_attn",
    )(slopes, pb, pb, pb, pb, pb)


def _gla_kernel(q_ref, k_ref, v_ref, r_ref, la_ref, hg_ref, o_ref, state_sc, b_sc, q_sc, *, chunk):
    tb = q_ref.shape[1]

    @pl.when(pl.program_id(2) == 0)
    def _():
        state_sc[...] = jnp.zeros_like(state_sc)

    row = lax.broadcasted_iota(jnp.int32, (chunk, chunk), 0)
    col = lax.broadcasted_iota(jnp.int32, (chunk, chunk), 1)
    causal = row >= col
    tril = jnp.where(causal, 1.0, 0.0).astype(bf16)
    steep = jnp.min(la_ref[0]) * chunk < -80.0

    def intra_exact(b, q, k):
        b_sc[...] = b
        q_sc[...] = q

        def one_row(i, at):
            bi = b_sc[pl.ds(i, 1), :]
            qi = q_sc[pl.ds(i, 1), :]
            g = jnp.exp(jnp.minimum(bi - b, 0.0)) * k * qi
            colv = jnp.sum(g, axis=-1, keepdims=True)
            return jnp.where(col == i, colv, at)

        return lax.fori_loop(0, chunk, one_row, jnp.zeros((chunk, chunk), f32)).T

    def step(exact):
        sls = [pl.ds(ci * chunk, chunk) for ci in range(tb // chunk)]
        bs = []
        for sl in sls:
            la = la_ref[0, sl, :]
            hi = la.astype(bf16)
            lo = (la - hi.astype(f32)).astype(bf16)
            bs.append(jnp.dot(tril, hi, preferred_element_type=f32)
                      + jnp.dot(tril, lo, preferred_element_type=f32))
        qs = [q_ref[0, sl, :].astype(f32) * (C_DK ** -0.5) for sl in sls]
        ks = [k_ref[0, sl, :].astype(f32) for sl in sls]
        qts = [(q * jnp.exp(b)).astype(bf16) for q, b in zip(qs, bs)]
        lasts = [b[chunk - 1:chunk, :] for b in bs]
        kds = [(k * jnp.exp(bl - b)).astype(bf16) for k, b, bl in zip(ks, bs, lasts)]
        if exact:
            ats = [intra_exact(b, q, k) for b, q, k in zip(bs, qs, ks)]
        else:
            kts = [(k * jnp.exp(-b)).astype(bf16) for k, b in zip(ks, bs)]
            ats = [lax.dot_general(qt, kt, NT_DIMS, preferred_element_type=f32) for qt, kt in zip(qts, kts)]
        updates = [lax.dot_general(v_ref[0, sl, :], kd, TN_DIMS, preferred_element_type=f32)
                   for sl, kd in zip(sls, kds)]
        intras = [jnp.dot(jnp.where(causal, a, 0.0).astype(bf16), v_ref[0, sl, :], preferred_element_type=f32)
                  for sl, a in zip(sls, ats)]
        states = [state_sc[...]]
        for bl, update in zip(lasts, updates):
            states.append(states[-1] * jnp.exp(bl) + update)
        state_sc[...] = states[-1]
        inters = [lax.dot_general(qt, st.astype(bf16), NT_DIMS, preferred_element_type=f32)
                  for qt, st in zip(qts, states[:-1])]
        for sl, o_intra, o_inter in zip(sls, intras, inters):
            rr = r_ref[0, sl, :].astype(f32)
            o_ref[0, sl, :] = (_rms(o_intra + o_inter, hg_ref[...]) * (rr * jax.nn.sigmoid(rr))).astype(bf16)

    @pl.when(jnp.logical_not(steep))
    def _():
        step(False)

    @pl.when(steep)
    def _():
        step(True)


def _gla(pqk, pvr, pla, head_g, tb=1024, chunk=128):
    bsz, s, _ = pqk.shape
    return pl.pallas_call(
        functools.partial(_gla_kernel, chunk=chunk),
        grid=(bsz, C_HEADS, s // tb),
        in_specs=[
            pl.BlockSpec((1, tb, C_DK), lambda b, h, i: (b, i, h)),
            pl.BlockSpec((1, tb, C_DK), lambda b, h, i: (b, i, C_HEADS + h)),
            pl.BlockSpec((1, tb, C_DV), lambda b, h, i: (b, i, h)),
            pl.BlockSpec((1, tb, C_DV), lambda b, h, i: (b, i, C_HEADS + h)),
            pl.BlockSpec((1, tb, C_DK), lambda b, h, i: (b, i, h)),
            pl.BlockSpec((1, C_DV), lambda b, h, i: (0, 0)),
        ],
        out_specs=pl.BlockSpec((1, tb, C_DV), lambda b, h, i: (b, i, h)),
        out_shape=jax.ShapeDtypeStruct((bsz, s, C_HEADS * C_DV), bf16),
        scratch_shapes=[pltpu.VMEM((C_DV, C_DK), f32),
                        pltpu.VMEM((chunk, C_DK), f32), pltpu.VMEM((chunk, C_DK), f32)],
        compiler_params=_params("parallel", "parallel", "arbitrary"),
        name="gla",
    )(pqk, pqk, pvr, pvr, pla, head_g)


def _post_kernel(x_ref, ya_ref, yb_ref, wo_ref, gate1_ref, sc2_ref, sh2_ref, gate2_ref, ng_ref,
                 wg_ref, wu_ref, wd_ref, o_ref, x1_sc, hb_sc, *, hc, groups, piece):
    half = ya_ref.shape[-1]
    tm = x_ref.shape[1]
    gr = tm // groups
    hidden = wg_ref.shape[1]
    chunks = [(c, min(c + hc, hidden)) for c in range(0, hidden, hc)]

    x1_sc[...] = (jnp.dot(ya_ref[0], wo_ref[0:half, :], preferred_element_type=f32)
                  + jnp.dot(yb_ref[0], wo_ref[half:2 * half, :], preferred_element_type=f32))

    def before_ffn(start):
        rows = pl.ds(start, piece)
        x1 = x_ref[0, rows, :] + (1.0 + gate1_ref[0]) * _rms(x1_sc[rows, :], ng_ref[0:1, :])
        x1_sc[rows, :] = x1
        hb_sc[rows, :] = (_rms(x1, ng_ref[1:2, :]) * (1.0 + sc2_ref[0]) + sh2_ref[0]).astype(bf16)

    def after_ffn(start, y2):
        rows = pl.ds(start, piece)
        o_ref[0, rows, :] = x1_sc[rows, :] + (1.0 + gate2_ref[0]) * _rms(y2, ng_ref[2:3, :])

    for p in range(0, gr, piece):
        before_ffn(p)
    done = None
    for g in range(groups):
        side = []
        if done is not None:
            side += [functools.partial(after_ffn, (g - 1) * gr + p, done[p:p + piece]) for p in range(0, gr, piece)]
        if g + 1 < groups:
            side += [functools.partial(before_ffn, (g + 1) * gr + p) for p in range(0, gr, piece)]
        hb = hb_sc[pl.ds(g * gr, gr), :]
        acc = jnp.zeros((gr, x1_sc.shape[1]), f32)
        for ci, (c, e) in enumerate(chunks):
            gt = jnp.dot(hb, wg_ref[:, c:e], preferred_element_type=f32)
            up = jnp.dot(hb, wu_ref[:, c:e], preferred_element_type=f32)
            act = (gt * jax.nn.sigmoid(gt) * up).astype(bf16)
            acc = acc + jnp.dot(act, wd_ref[c:e, :], preferred_element_type=f32)
            for task in side[ci * len(side) // len(chunks):(ci + 1) * len(side) // len(chunks)]:
                task()
        done = acc
    for p in range(0, gr, piece):
        after_ffn((groups - 1) * gr + p, done[p:p + piece])


def _post(x, ya, yb, cols, wo, gate1, sc2, sh2, gate2, ng, layer, wg, wu, wd, tm=1024, hc=512):
    bsz, s, d = x.shape
    half = wo.shape[0] // 2
    hidden = wg.shape[-1]
    row = lambda b, i: (b, i, 0)
    per_b = lambda b, i: (b, 0, 0)
    mod = pl.BlockSpec((1, 1, d), per_b)
    slab = lambda shape: pl.BlockSpec((None,) + shape, lambda b, i: (layer, 0, 0), pipeline_mode=pl.Buffered(1))
    return pl.pallas_call(
        functools.partial(_post_kernel, hc=hc, groups=2, piece=128),
        grid=(bsz, s // tm),
        in_specs=[
            pl.BlockSpec((1, tm, d), row),
            pl.BlockSpec((1, tm, half), lambda b, i: (b, i, cols[0])),
            pl.BlockSpec((1, tm, half), lambda b, i: (b, i, cols[1])),
            _resident(wo.shape),
            mod, mod, mod, mod,
            _resident(ng.shape),
            slab((d, hidden)), slab((d, hidden)), slab((hidden, d)),
        ],
        out_specs=pl.BlockSpec((1, tm, d), row),
        out_shape=jax.ShapeDtypeStruct((bsz, s, d), f32),
        scratch_shapes=[pltpu.VMEM((tm, d), f32), pltpu.VMEM((tm, d), bf16)],
        compiler_params=_params("parallel", "parallel"),
        name="post",
    )(x, ya, yb, wo, gate1, sc2, sh2, gate2, ng, wg, wu, wd)


def _alibi_slopes(n):
    return 2.0 ** (-8.0 * jnp.arange(1, n + 1, dtype=f32) / n)


def kernel(x, c, ada_w, ada_b, norm_g, ev_w_in, ev_lambda, ev_subln_g, ev_w_out, od_w_in, od_w_g2,
           od_b_g2, od_head_g, od_w_out, ffn_w_gate, ffn_w_up, ffn_w_down):
    depth = ada_w.shape[0]
    mod = _adaln(c, ada_w, ada_b)
    ffn_w = (ffn_w_gate.astype(bf16), ffn_w_up.astype(bf16), ffn_w_down.astype(bf16))
    for l in range(depth):
        sh1, sc1, g1, sh2, sc2, g2 = (mod[l, :, j] for j in range(6))
        pre_g = norm_g[l, 0:1]
        if l % 2 == 0:
            e = l // 2
            lam_init = 0.8 - 0.6 * math.exp(-0.3 * l)
            pa, pb = _inproj_even(x, pre_g, sc1, sh1, ev_w_in[e].astype(bf16))
            ya = _diff_attn(pa, _alibi_slopes(A_HEADS), ev_lambda[e], ev_subln_g[e][:, None], lam_init)
            yb = _dilated_attn(pb, _alibi_slopes(B_HEADS))
            cols = (0, 0)
            wo = ev_w_out[e]
        else:
            o = l // 2
            w = od_w_in[o].astype(bf16)
            nmain = w.shape[1] - C_GATE_RANK
            wg1 = jnp.zeros((w.shape[0], LANES), bf16).at[:, :C_GATE_RANK].set(w[:, nmain:])
            wg2 = jnp.zeros((LANES, od_w_g2.shape[-1]), bf16).at[:C_GATE_RANK].set(od_w_g2[o].astype(bf16))
            pqk, pvr, pla = _inproj_odd(x, pre_g, sc1, sh1, w, wg1, wg2, od_b_g2[o][None, :])
            ya = yb = _gla(pqk, pvr, pla, od_head_g[o][None, :])
            cols = (0, 1)
            wo = od_w_out[o]
        x = _post(x, ya, yb, cols, wo.astype(bf16), g1, sc2, sh2, g2, norm_g[l, 1:4], l, *ffn_w)
    return x
```

```python
import functools
import math

import jax
import jax.numpy as jnp
from jax import lax
from jax.experimental import pallas as pl
from jax.experimental.pallas import tpu as pltpu

f32 = jnp.float32
bf16 = jnp.bfloat16

EPS = 1e-6
LANES = 128
HEAD_DIM = 64
A_HEADS = 4
B_HEADS = 8
DILATIONS = (1, 4, 16)
BAND = 128
C_HEADS = 4
C_DK = 128
C_DV = 256
C_GATE_RANK = 16
C_TAU = 16.0
VMEM_LIMIT = 56 * 1024 * 1024
LOG2E = math.log2(math.e)
F8 = jnp.float8_e4m3fn
Q_SCALE = 0.5 * LOG2E
K_SCALE = 0.25
Q_TERM_SCALES = (1.0, 8.0, 1.0 / 16, 4.0)
K_TERM_SCALES = (1.0, 1.0 / 8, 16.0, 1.0 / 4)
ONES_ROWS = 16

NT_DIMS = (((1,), (1,)), ((), ()))
TN_DIMS = (((0,), (0,)), ((), ()))


def _rms(x, g):
    return x * lax.rsqrt(jnp.mean(x * x, axis=-1, keepdims=True) + EPS) * g


def _params(*sem):
    return pltpu.CompilerParams(dimension_semantics=sem, vmem_limit_bytes=VMEM_LIMIT)


def _resident(shape):
    nd = len(shape)
    return pl.BlockSpec(shape, lambda *_: (0,) * nd, pipeline_mode=pl.Buffered(1))


def _adaln_kernel(c_ref, w_ref, b_ref, o_ref):
    c = c_ref[...]
    a = (c * jax.nn.sigmoid(c)).astype(bf16)
    o_ref[0] = jnp.dot(a, w_ref[0].astype(bf16), preferred_element_type=f32) + b_ref[0]


def _adaln(c, ada_w, ada_b):
    depth, d, n = ada_w.shape
    bsz = c.shape[0]
    rows = 8
    cp = jnp.zeros((rows, d), f32).at[:bsz].set(c)
    tn = 3072
    out = pl.pallas_call(
        _adaln_kernel,
        grid=(depth, n // tn),
        in_specs=[
            pl.BlockSpec((rows, d), lambda l, j: (0, 0)),
            pl.BlockSpec((1, d, tn), lambda l, j: (l, 0, j)),
            pl.BlockSpec((1, 1, tn), lambda l, j: (l, 0, j)),
        ],
        out_specs=pl.BlockSpec((1, rows, tn), lambda l, j: (l, 0, j)),
        out_shape=jax.ShapeDtypeStruct((depth, rows, n), f32),
        compiler_params=_params("parallel", "parallel"),
        name="adaln",
    )(cp, ada_w, ada_b.reshape(depth, 1, n))
    return out[:, :bsz].reshape(depth, bsz, 6, 1, d)


def _modulated(x_ref, g_ref, sc_ref, sh_ref):
    h = _rms(x_ref[0], g_ref[...])
    return (h * (1.0 + sc_ref[0]) + sh_ref[0]).astype(bf16)


def _inproj_even_kernel(x_ref, g_ref, sc_ref, sh_ref, w_ref, oa_ref, ob_ref, *, nc):
    hb = _modulated(x_ref, g_ref, sc_ref, sh_ref)
    na = oa_ref.shape[-1]
    for c in range(0, na, nc):
        oa_ref[0, :, c:c + nc] = jnp.dot(hb, w_ref[:, c:c + nc], preferred_element_type=f32).astype(bf16)
    for c in range(0, ob_ref.shape[-1], nc):
        ob_ref[0, :, c:c + nc] = jnp.dot(hb, w_ref[:, na + c:na + c + nc], preferred_element_type=f32)


def _inproj_even(x, g, sc, sh, w, tm=512):
    bsz, s, d = x.shape
    na = 3 * A_HEADS * 2 * HEAD_DIM
    nb = w.shape[1] - na
    row = lambda b, i: (b, i, 0)
    per_b = lambda b, i: (b, 0, 0)
    return pl.pallas_call(
        functools.partial(_inproj_even_kernel, nc=512),
        grid=(bsz, s // tm),
        in_specs=[
            pl.BlockSpec((1, tm, d), row),
            pl.BlockSpec((1, d), lambda b, i: (0, 0)),
            pl.BlockSpec((1, 1, d), per_b),
            pl.BlockSpec((1, 1, d), per_b),
            _resident(w.shape),
        ],
        out_specs=[pl.BlockSpec((1, tm, na), row), pl.BlockSpec((1, tm, nb), row)],
        out_shape=[jax.ShapeDtypeStruct((bsz, s, na), bf16), jax.ShapeDtypeStruct((bsz, s, nb), f32)],
        compiler_params=_params("parallel", "parallel"),
        name="inproj_even",
    )(x, g, sc, sh, w)


def _inproj_odd_kernel(x_ref, g_ref, sc_ref, sh_ref, w_ref, wg1_ref, wg2_ref, bg2_ref,
                       oqk_ref, ovr_ref, ola_ref, *, nc):
    hb = _modulated(x_ref, g_ref, sc_ref, sh_ref)
    glr = jnp.dot(hb, wg1_ref[...], preferred_element_type=f32)
    z = jnp.dot(glr.astype(bf16), wg2_ref[...], preferred_element_type=f32) + bg2_ref[...]
    ola_ref[0] = (jnp.minimum(z, 0.0) - jnp.log(1.0 + jnp.exp(-jnp.abs(z)))) * (1.0 / C_TAU)
    nqk = oqk_ref.shape[-1]
    for c in range(0, nqk, nc):
        oqk_ref[0, :, c:c + nc] = jnp.dot(hb, w_ref[:, c:c + nc], preferred_element_type=f32).astype(bf16)
    for c in range(0, ovr_ref.shape[-1], nc):
        ovr_ref[0, :, c:c + nc] = jnp.dot(hb, w_ref[:, nqk + c:nqk + c + nc], preferred_element_type=f32).astype(bf16)


def _inproj_odd(x, g, sc, sh, w, wg1, wg2, bg2, tm=512):
    bsz, s, d = x.shape
    nqk = 2 * C_HEADS * C_DK
    nvr = 2 * C_HEADS * C_DV
    nla = C_HEADS * C_DK
    row = lambda b, i: (b, i, 0)
    per_b = lambda b, i: (b, 0, 0)
    return pl.pallas_call(
        functools.partial(_inproj_odd_kernel, nc=512),
        grid=(bsz, s // tm),
        in_specs=[
            pl.BlockSpec((1, tm, d), row),
            pl.BlockSpec((1, d), lambda b, i: (0, 0)),
            pl.BlockSpec((1, 1, d), per_b),
            pl.BlockSpec((1, 1, d), per_b),
            _resident((d, nqk + nvr)),
            _resident(wg1.shape),
            _resident(wg2.shape),
            _resident(bg2.shape),
        ],
        out_specs=[pl.BlockSpec((1, tm, nqk), row), pl.BlockSpec((1, tm, nvr), row), pl.BlockSpec((1, tm, nla), row)],
        out_shape=[jax.ShapeDtypeStruct((bsz, s, nqk), bf16), jax.ShapeDtypeStruct((bsz, s, nvr), bf16),
                   jax.ShapeDtypeStruct((bsz, s, nla), f32)],
        compiler_params=_params("parallel", "parallel"),
        name="inproj_odd",
    )(x, g, sc, sh, w, wg1, wg2, bg2)


def _split_key_terms(x, first_half):
    hi = x.astype(F8).astype(f32)
    lo = x - hi
    hi_swapped = pltpu.roll(hi, HEAD_DIM, 1)
    lo_swapped = pltpu.roll(lo, HEAD_DIM, 1)
    out = []
    for h in range(2):
        near_hi, far_hi = (hi, hi_swapped) if h == 0 else (hi_swapped, hi)
        near_lo, far_lo = (lo, lo_swapped) if h == 0 else (lo_swapped, lo)
        rows = jnp.concatenate(
            [jnp.where(first_half, near_hi * K_TERM_SCALES[0], far_hi * K_TERM_SCALES[1]),
             jnp.where(first_half, near_lo * K_TERM_SCALES[2], far_lo * K_TERM_SCALES[3])], axis=1)
        out.append(rows.astype(F8))
    return out


def _diff_attn_kernel(slope_ref, lam_ref, q_ref, k_ref, v_ref, g_ref, o_ref,
                      ks_sc, vt_sc, bias_sc, sta_sc, stb_sc, cma_sc, cmb_sc, m_sc, acc_sc, *, t, lam_init):
    h = pl.program_id(1)
    i = pl.program_id(2)
    nblk = k_ref.shape[1] // t
    width = 2 * HEAD_DIM
    slope2 = slope_ref[h] * LOG2E

    @pl.when(i == 0)
    def _():
        lane = lax.broadcasted_iota(jnp.int32, (t, LANES), 1)
        first_half = lane < HEAD_DIM
        ones_row = jnp.where(lax.broadcasted_iota(jnp.int32, (ONES_ROWS, t), 0) == 0, 1.0, 0.0).astype(bf16)
        for c in range(nblk):
            kb = k_ref[0, c * t:(c + 1) * t, :].astype(f32) * K_SCALE
            for h, rows in enumerate(_split_key_terms(kb, first_half)):
                ks_sc[c, h] = rows
            vt_sc[c, 0:width, :] = v_ref[0, c * t:(c + 1) * t, :].astype(f32).T.astype(bf16)
            vt_sc[c, width:width + ONES_ROWS, :] = ones_row
        kr = lax.broadcasted_iota(jnp.int32, (t, 2 * t), 0)
        qc = lax.broadcasted_iota(jnp.int32, (t, 2 * t), 1) % t
        bias_sc[...] = slope2 * (qc - kr).astype(f32)

    qt = (q_ref[0].astype(f32) * Q_SCALE).T
    qt_hi = qt.astype(F8).astype(f32)
    qt_lo = qt - qt_hi
    qts = []
    for h in range(2):
        hi, lo = (part[h * HEAD_DIM:(h + 1) * HEAD_DIM] for part in (qt_hi, qt_lo))
        qts.append(jnp.concatenate([hi * Q_TERM_SCALES[0], lo * Q_TERM_SCALES[1],
                                    hi * Q_TERM_SCALES[2], lo * Q_TERM_SCALES[3]], axis=0).astype(F8))
    lane = lax.broadcasted_iota(jnp.int32, (1, 2 * t), 1)
    one_block_further = jnp.where(lane >= t, slope2 * t, 0.0)

    def scores_into(st_ref, cm_ref, j, lo):
        for c in range(2):
            s = jnp.dot(ks_sc[j, c], qts[c][:, lo:], preferred_element_type=f32) - bias_sc[:, lo:]
            st_ref[c * t:(c + 1) * t, lo:] = s
            cm_ref[c, :, lo:] = jnp.max(s, axis=0, keepdims=True)

    def softmax_pv(st_ref, cm_ref, j, lo, diagonal):
        vt = vt_sc[j]
        shift = slope2 * ((2 * i - j) * t).astype(f32) + one_block_further[:, lo:]
        for c in range(2):
            s = st_ref[c * t:(c + 1) * t, lo:]
            if diagonal:
                kr = lax.broadcasted_iota(jnp.int32, s.shape, 0)
                qc = lax.broadcasted_iota(jnp.int32, s.shape, 1)
                s = jnp.where(jnp.logical_or(qc >= t, kr <= qc), s, -jnp.inf)
                cmax = jnp.max(s, axis=0, keepdims=True)
            else:
                cmax = cm_ref[c, :, lo:]
            m_old = m_sc[c, :, lo:]
            m_new = jnp.maximum(m_old, cmax - shift)
            p = jnp.exp2(s - (m_new + shift)).astype(bf16)
            pv = jnp.dot(vt, p, preferred_element_type=f32)
            acc_sc[c, :, lo:] = jnp.exp2(m_old - m_new) * acc_sc[c, :, lo:] + pv
            m_sc[c, :, lo:] = m_new

    m_sc[...] = jnp.full(m_sc.shape, -jnp.inf, f32)
    acc_sc[...] = jnp.zeros(acc_sc.shape, f32)
    scores_into(sta_sc, cma_sc, 0, 0)

    def body(jj, carry):
        j = 2 * jj
        scores_into(stb_sc, cmb_sc, j + 1, 0)
        softmax_pv(sta_sc, cma_sc, j, 0, False)
        scores_into(sta_sc, cma_sc, j + 2, 0)
        softmax_pv(stb_sc, cmb_sc, j + 1, 0, False)
        return carry

    lax.fori_loop(0, i, body, 0)

    scores_into(stb_sc, cmb_sc, 2 * i + 1, t)
    softmax_pv(sta_sc, cma_sc, 2 * i, 0, True)
    softmax_pv(stb_sc, cmb_sc, 2 * i + 1, t, True)

    lp = lam_ref[...]
    lam = (jnp.exp(jnp.sum(lp[0:1] * lp[1:2], axis=-1, keepdims=True))
           - jnp.exp(jnp.sum(lp[2:3] * lp[3:4], axis=-1, keepdims=True)) + lam_init)
    outs = [acc_sc[c, 0:width, :] / acc_sc[c, width:width + 1, :] for c in range(2)]
    ot = outs[0] - lam * outs[1]
    ot = ot * lax.rsqrt(jnp.mean(ot * ot, axis=0, keepdims=True) + EPS) * (g_ref[...] * (1.0 - lam_init))
    o_ref[0] = ot.T.astype(bf16)


def _diff_attn(pa, slopes, lam_p, subln_g, lam_init, t=512):
    bsz, s, _ = pa.shape
    width = 2 * HEAD_DIM
    nblk = s // t
    return pl.pallas_call(
        functools.partial(_diff_attn_kernel, t=t, lam_init=lam_init),
        grid=(bsz, A_HEADS, nblk // 2),
        in_specs=[
            pl.BlockSpec(memory_space=pltpu.SMEM),
            pl.BlockSpec(lam_p.shape, lambda b, h, i: (0, 0)),
            pl.BlockSpec((1, 2 * t, width), lambda b, h, i: (b, i, h)),
            pl.BlockSpec((1, s, width), lambda b, h, i: (b, 0, A_HEADS + h)),
            pl.BlockSpec((1, s, width), lambda b, h, i: (b, 0, 2 * A_HEADS + h)),
            pl.BlockSpec((width, 1), lambda b, h, i: (0, 0)),
        ],
        out_specs=pl.BlockSpec((1, 2 * t, width), lambda b, h, i: (b, i, h)),
        out_shape=jax.ShapeDtypeStruct((bsz, s, A_HEADS * width), bf16),
        scratch_shapes=[pltpu.VMEM((nblk, 2, t, 2 * width), F8),
                        pltpu.VMEM((nblk, width + ONES_ROWS, t), bf16),
                        pltpu.VMEM((t, 2 * t), f32),
                        pltpu.VMEM((2 * t, 2 * t), f32),
                        pltpu.VMEM((2 * t, 2 * t), f32),
                        pltpu.VMEM((2, 1, 2 * t), f32),
                        pltpu.VMEM((2, 1, 2 * t), f32),
                        pltpu.VMEM((2, 1, 2 * t), f32),
                        pltpu.VMEM((2, width + ONES_ROWS, 2 * t), f32)],
        compiler_params=_params("parallel", "parallel", "arbitrary"),
        name="diff_attn",
    )(slopes, lam_p, pa, pa, pa, subln_g)


def _dilated_kernel(slope_ref, q_ref, kp_ref, kc_ref, vp_ref, vc_ref, o_ref,
                    acc_sc, m_sc, l_sc, *, span):
    hp = pl.program_id(1)
    sidx = pl.program_id(2)
    blk = BAND
    nblocks = span // blk

    lane = lax.broadcasted_iota(jnp.int32, (blk, LANES), 1)
    head_a = lane < HEAD_DIM
    a = lax.broadcasted_iota(jnp.int32, (blk, 2 * blk), 0)
    bi = lax.broadcasted_iota(jnp.int32, (blk, 2 * blk), 1)
    rel = blk + a - bi
    valid = (rel >= 0) & (rel <= BAND)
    before_start = bi < blk
    steps = jnp.where(valid, rel.astype(f32), jnp.inf)
    slopes = (slope_ref[2 * hp], slope_ref[2 * hp + 1])
    for br, d in enumerate(DILATIONS):
        per_stream = nblocks // d
        rates = [sl * (LOG2E * d) for sl in slopes]

        def one_block(r, n, br=br, d=d, rates=rates):
            q_start = n * (blk * d) + r
            qb = (q_ref[0, pl.ds(q_start, blk, stride=d), :] * (HEAD_DIM ** -0.5 * LOG2E)).astype(bf16)
            if n == 0:
                tail = pl.ds(span - blk * d + r, blk, stride=d)
                own = pl.ds(r, blk, stride=d)
                kk = jnp.concatenate([kp_ref[0, tail, :], kc_ref[0, own, :]], axis=0).astype(bf16)
                vv = jnp.concatenate([vp_ref[0, tail, :], vc_ref[0, own, :]], axis=0).astype(bf16)
            else:
                both = pl.ds((n - 1) * (blk * d) + r, 2 * blk, stride=d)
                kk = kc_ref[0, both, :].astype(bf16)
                vv = vc_ref[0, both, :].astype(bf16)
            zero = jnp.zeros_like(qb)
            qh = (jnp.where(head_a, qb, zero), jnp.where(head_a, zero, qb))
            outs, ms, ls = [], [], []
            for c in range(2):
                s = lax.dot_general(qh[c], kk, NT_DIMS, preferred_element_type=f32) - rates[c] * steps
                if n == 0:
                    s = jnp.where(jnp.logical_and(sidx == 0, before_start), -jnp.inf, s)
                m = jnp.max(s, axis=-1, keepdims=True)
                p = jnp.exp2(s - m)
                ls.append(jnp.sum(p, axis=-1, keepdims=True))
                ms.append(m)
                outs.append(jnp.dot(p.astype(bf16), vv, preferred_element_type=f32))
            dst = pl.ds(q_start, blk, stride=d)
            acc_sc[br, dst, :] = jnp.where(head_a, outs[0], outs[1])
            m_sc[br, dst, :] = jnp.where(head_a, ms[0], ms[1])
            l_sc[br, dst, :] = jnp.where(head_a, ls[0], ls[1])

        for r in range(d):
            for n in range(per_stream):
                one_block(r, n)

    m_all = jnp.maximum(jnp.maximum(m_sc[0], m_sc[1]), m_sc[2])
    num = jnp.zeros((span, LANES), f32)
    den = jnp.zeros((span, LANES), f32)
    for br in range(len(DILATIONS)):
        w = jnp.exp2(m_sc[br] - m_all)
        num = num + w * acc_sc[br]
        den = den + w * l_sc[br]
    o_ref[0] = (num / den).astype(bf16)


def _dilated_attn(pb, slopes, span=2048):
    bsz, s, _ = pb.shape
    pairs = B_HEADS // 2
    cur = lambda off: (lambda b, h, i: (b, i, off + h))
    prev = lambda off: (lambda b, h, i: (b, jnp.maximum(i - 1, 0), off + h))
    blk = (1, span, LANES)
    nbr = len(DILATIONS)
    return pl.pallas_call(
        functools.partial(_dilated_kernel, span=span),
        grid=(bsz, pairs, s // span),
        in_specs=[
            pl.BlockSpec(memory_space=pltpu.SMEM),
            pl.BlockSpec(blk, cur(0)),
            pl.BlockSpec(blk, prev(pairs)),
            pl.BlockSpec(blk, cur(pairs)),
            pl.BlockSpec(blk, prev(2 * pairs)),
            pl.BlockSpec(blk, cur(2 * pairs)),
        ],
        out_specs=pl.BlockSpec(blk, cur(0)),
        out_shape=jax.ShapeDtypeStruct((bsz, s, B_HEADS * HEAD_DIM), bf16),
        scratch_shapes=[pltpu.VMEM((nbr, span, LANES), f32), pltpu.VMEM((nbr, span, LANES), f32),
                        pltpu.VMEM((nbr, span, LANES), f32)],
        compiler_params=_params("parallel", "parallel", "parallel"),
        name="dilated_attn",
    )(slopes, pb, pb, pb, pb, pb)


def _gla_kernel(q_ref, k_ref, v_ref, r_ref, la_ref, hg_ref, o_ref, state_sc, b_sc, q_sc, *, chunk):
    tb = q_ref.shape[1]

    @pl.when(pl.program_id(2) == 0)
    def _():
        state_sc[...] = jnp.zeros_like(state_sc)

    row = lax.broadcasted_iota(jnp.int32, (chunk, chunk), 0)
    col = lax.broadcasted_iota(jnp.int32, (chunk, chunk), 1)
    causal = row >= col
    tril = jnp.where(causal, 1.0, 0.0).astype(bf16)
    steep = jnp.min(la_ref[0]) * chunk < -80.0

    def intra_exact(b, q, k):
        b_sc[...] = b
        q_sc[...] = q

        def one_row(i, at):
            bi = b_sc[pl.ds(i, 1), :]
            qi = q_sc[pl.ds(i, 1), :]
            g = jnp.exp(jnp.minimum(bi - b, 0.0)) * k * qi
            colv = jnp.sum(g, axis=-1, keepdims=True)
            return jnp.where(col == i, colv, at)

        return lax.fori_loop(0, chunk, one_row, jnp.zeros((chunk, chunk), f32)).T

    def step(exact):
        sls = [pl.ds(ci * chunk, chunk) for ci in range(tb // chunk)]
        bs = []
        for sl in sls:
            la = la_ref[0, sl, :]
            hi = la.astype(bf16)
            lo = (la - hi.astype(f32)).astype(bf16)
            bs.append(jnp.dot(tril, hi, preferred_element_type=f32)
                      + jnp.dot(tril, lo, preferred_element_type=f32))
        qs = [q_ref[0, sl, :].astype(f32) * (C_DK ** -0.5) for sl in sls]
        ks = [k_ref[0, sl, :].astype(f32) for sl in sls]
        qts = [(q * jnp.exp(b)).astype(bf16) for q, b in zip(qs, bs)]
        lasts = [b[chunk - 1:chunk, :] for b in bs]
        kds = [(k * jnp.exp(bl - b)).astype(bf16) for k, b, bl in zip(ks, bs, lasts)]
        if exact:
            ats = [intra_exact(b, q, k) for b, q, k in zip(bs, qs, ks)]
        else:
            kts = [(k * jnp.exp(-b)).astype(bf16) for k, b in zip(ks, bs)]
            ats = [lax.dot_general(qt, kt, NT_DIMS, preferred_element_type=f32) for qt, kt in zip(qts, kts)]
        updates = [lax.dot_general(v_ref[0, sl, :], kd, TN_DIMS, preferred_element_type=f32)
                   for sl, kd in zip(sls, kds)]
        intras = [jnp.dot(jnp.where(causal, a, 0.0).astype(bf16), v_ref[0, sl, :], preferred_element_type=f32)
                  for sl, a in zip(sls, ats)]
        states = [state_sc[...]]
        for bl, update in zip(lasts, updates):
            states.append(states[-1] * jnp.exp(bl) + update)
        state_sc[...] = states[-1]
        inters = [lax.dot_general(qt, st.astype(bf16), NT_DIMS, preferred_element_type=f32)
                  for qt, st in zip(qts, states[:-1])]
        for sl, o_intra, o_inter in zip(sls, intras, inters):
            rr = r_ref[0, sl, :].astype(f32)
            o_ref[0, sl, :] = (_rms(o_intra + o_inter, hg_ref[...]) * (rr * jax.nn.sigmoid(rr))).astype(bf16)

    @pl.when(jnp.logical_not(steep))
    def _():
        step(False)

    @pl.when(steep)
    def _():
        step(True)


def _gla(pqk, pvr, pla, head_g, tb=1024, chunk=128):
    bsz, s, _ = pqk.shape
    return pl.pallas_call(
        functools.partial(_gla_kernel, chunk=chunk),
        grid=(bsz, C_HEADS, s // tb),
        in_specs=[
            pl.BlockSpec((1, tb, C_DK), lambda b, h, i: (b, i, h)),
            pl.BlockSpec((1, tb, C_DK), lambda b, h, i: (b, i, C_HEADS + h)),
            pl.BlockSpec((1, tb, C_DV), lambda b, h, i: (b, i, h)),
            pl.BlockSpec((1, tb, C_DV), lambda b, h, i: (b, i, C_HEADS + h)),
            pl.BlockSpec((1, tb, C_DK), lambda b, h, i: (b, i, h)),
            pl.BlockSpec((1, C_DV), lambda b, h, i: (0, 0)),
        ],
        out_specs=pl.BlockSpec((1, tb, C_DV), lambda b, h, i: (b, i, h)),
        out_shape=jax.ShapeDtypeStruct((bsz, s, C_HEADS * C_DV), bf16),
        scratch_shapes=[pltpu.VMEM((C_DV, C_DK), f32),
                        pltpu.VMEM((chunk, C_DK), f32), pltpu.VMEM((chunk, C_DK), f32)],
        compiler_params=_params("parallel", "parallel", "arbitrary"),
        name="gla",
    )(pqk, pqk, pvr, pvr, pla, head_g)


def _post_kernel(x_ref, ya_ref, yb_ref, wo_ref, gate1_ref, sc2_ref, sh2_ref, gate2_ref, ng_ref,
                 wg_ref, wu_ref, wd_ref, o_ref, x1_sc, hb_sc, *, hc, groups, piece):
    half = ya_ref.shape[-1]
    tm = x_ref.shape[1]
    gr = tm // groups
    hidden = wg_ref.shape[1]
    chunks = [(c, min(c + hc, hidden)) for c in range(0, hidden, hc)]

    x1_sc[...] = (jnp.dot(ya_ref[0], wo_ref[0:half, :], preferred_element_type=f32)
                  + jnp.dot(yb_ref[0], wo_ref[half:2 * half, :], preferred_element_type=f32))

    def before_ffn(start):
        rows = pl.ds(start, piece)
        x1 = x_ref[0, rows, :] + (1.0 + gate1_ref[0]) * _rms(x1_sc[rows, :], ng_ref[0:1, :])
        x1_sc[rows, :] = x1
        hb_sc[rows, :] = (_rms(x1, ng_ref[1:2, :]) * (1.0 + sc2_ref[0]) + sh2_ref[0]).astype(bf16)

    def after_ffn(start, y2):
        rows = pl.ds(start, piece)
        o_ref[0, rows, :] = x1_sc[rows, :] + (1.0 + gate2_ref[0]) * _rms(y2, ng_ref[2:3, :])

    for p in range(0, gr, piece):
        before_ffn(p)
    done = None
    for g in range(groups):
        side = []
        if done is not None:
            side += [functools.partial(after_ffn, (g - 1) * gr + p, done[p:p + piece]) for p in range(0, gr, piece)]
        if g + 1 < groups:
            side += [functools.partial(before_ffn, (g + 1) * gr + p) for p in range(0, gr, piece)]
        hb = hb_sc[pl.ds(g * gr, gr), :]
        acc = jnp.zeros((gr, x1_sc.shape[1]), f32)
        for ci, (c, e) in enumerate(chunks):
            gt = jnp.dot(hb, wg_ref[:, c:e], preferred_element_type=f32)
            up = jnp.dot(hb, wu_ref[:, c:e], preferred_element_type=f32)
            act = (gt * jax.nn.sigmoid(gt) * up).astype(bf16)
            acc = acc + jnp.dot(act, wd_ref[c:e, :], preferred_element_type=f32)
            for task in side[ci * len(side) // len(chunks):(ci + 1) * len(side) // len(chunks)]:
                task()
        done = acc
    for p in range(0, gr, piece):
        after_ffn((groups - 1) * gr + p, done[p:p + piece])


def _post(x, ya, yb, cols, wo, gate1, sc2, sh2, gate2, ng, layer, wg, wu, wd, tm=1024, hc=512):
    bsz, s, d = x.shape
    half = wo.shape[0] // 2
    hidden = wg.shape[-1]
    row = lambda b, i: (b, i, 0)
    per_b = lambda b, i: (b, 0, 0)
    mod = pl.BlockSpec((1, 1, d), per_b)
    slab = lambda shape: pl.BlockSpec((None,) + shape, lambda b, i: (layer, 0, 0), pipeline_mode=pl.Buffered(1))
    return pl.pallas_call(
        functools.partial(_post_kernel, hc=hc, groups=2, piece=128),
        grid=(bsz, s // tm),
        in_specs=[
            pl.BlockSpec((1, tm, d), row),
            pl.BlockSpec((1, tm, half), lambda b, i: (b, i, cols[0])),
            pl.BlockSpec((1, tm, half), lambda b, i: (b, i, cols[1])),
            _resident(wo.shape),
            mod, mod, mod, mod,
            _resident(ng.shape),
            slab((d, hidden)), slab((d, hidden)), slab((hidden, d)),
        ],
        out_specs=pl.BlockSpec((1, tm, d), row),
        out_shape=jax.ShapeDtypeStruct((bsz, s, d), f32),
        scratch_shapes=[pltpu.VMEM((tm, d), f32), pltpu.VMEM((tm, d), bf16)],
        compiler_params=_params("parallel", "parallel"),
        name="post",
    )(x, ya, yb, wo, gate1, sc2, sh2, gate2, ng, wg, wu, wd)


def _alibi_slopes(n):
    return 2.0 ** (-8.0 * jnp.arange(1, n + 1, dtype=f32) / n)


def kernel(x, c, ada_w, ada_b, norm_g, ev_w_in, ev_lambda, ev_subln_g, ev_w_out, od_w_in, od_w_g2,
           od_b_g2, od_head_g, od_w_out, ffn_w_gate, ffn_w_up, ffn_w_down):
    depth = ada_w.shape[0]
    mod = _adaln(c, ada_w, ada_b)
    ffn_w = (ffn_w_gate.astype(bf16), ffn_w_up.astype(bf16), ffn_w_down.astype(bf16))
    for l in range(depth):
        sh1, sc1, g1, sh2, sc2, g2 = (mod[l, :, j] for j in range(6))
        pre_g = norm_g[l, 0:1]
        if l % 2 == 0:
            e = l // 2
            lam_init = 0.8 - 0.6 * math.exp(-0.3 * l)
            pa, pb = _inproj_even(x, pre_g, sc1, sh1, ev_w_in[e].astype(bf16))
            ya = _diff_attn(pa, _alibi_slopes(A_HEADS), ev_lambda[e], ev_subln_g[e][:, None], lam_init)
            yb = _dilated_attn(pb, _alibi_slopes(B_HEADS))
            cols = (0, 0)
            wo = ev_w_out[e]
        else:
            o = l // 2
            w = od_w_in[o].astype(bf16)
            nmain = w.shape[1] - C_GATE_RANK
            wg1 = jnp.zeros((w.shape[0], LANES), bf16).at[:, :C_GATE_RANK].set(w[:, nmain:])
            wg2 = jnp.zeros((LANES, od_w_g2.shape[-1]), bf16).at[:C_GATE_RANK].set(od_w_g2[o].astype(bf16))
            pqk, pvr, pla = _inproj_odd(x, pre_g, sc1, sh1, w, wg1, wg2, od_b_g2[o][None, :])
            ya = yb = _gla(pqk, pvr, pla, od_head_g[o][None, :])
            cols = (0, 1)
            wo = od_w_out[o]
        x = _post(x, ya, yb, cols, wo.astype(bf16), g1, sc2, sh2, g2, norm_g[l, 1:4], l, *ffn_w)
    return x
```

```python
import functools
import math

import jax
import jax.numpy as jnp
from jax import lax
from jax.experimental import pallas as pl
from jax.experimental.pallas import tpu as pltpu

f32 = jnp.float32
bf16 = jnp.bfloat16

EPS = 1e-6
LANES = 128
HEAD_DIM = 64
A_HEADS = 4
B_HEADS = 8
DILATIONS = (1, 4, 16)
BAND = 128
C_HEADS = 4
C_DK = 128
C_DV = 256
C_GATE_RANK = 16
C_TAU = 16.0
VMEM_LIMIT = 56 * 1024 * 1024
LOG2E = math.log2(math.e)
QUERY_CHUNK = 256
ONES_ROWS = 16

NT_DIMS = (((1,), (1,)), ((), ()))
TN_DIMS = (((0,), (0,)), ((), ()))


def _rms(x, g):
    return x * lax.rsqrt(jnp.mean(x * x, axis=-1, keepdims=True) + EPS) * g


def _params(*sem):
    return pltpu.CompilerParams(dimension_semantics=sem, vmem_limit_bytes=VMEM_LIMIT)


def _resident(shape):
    nd = len(shape)
    return pl.BlockSpec(shape, lambda *_: (0,) * nd, pipeline_mode=pl.Buffered(1))


def _adaln_kernel(c_ref, w_ref, b_ref, o_ref):
    c = c_ref[...]
    a = (c * jax.nn.sigmoid(c)).astype(bf16)
    o_ref[0] = jnp.dot(a, w_ref[0].astype(bf16), preferred_element_type=f32) + b_ref[0]


def _adaln(c, ada_w, ada_b):
    depth, d, n = ada_w.shape
    bsz = c.shape[0]
    rows = 8
    cp = jnp.zeros((rows, d), f32).at[:bsz].set(c)
    tn = 3072
    out = pl.pallas_call(
        _adaln_kernel,
        grid=(depth, n // tn),
        in_specs=[
            pl.BlockSpec((rows, d), lambda l, j: (0, 0)),
            pl.BlockSpec((1, d, tn), lambda l, j: (l, 0, j)),
            pl.BlockSpec((1, 1, tn), lambda l, j: (l, 0, j)),
        ],
        out_specs=pl.BlockSpec((1, rows, tn), lambda l, j: (l, 0, j)),
        out_shape=jax.ShapeDtypeStruct((depth, rows, n), f32),
        compiler_params=_params("parallel", "parallel"),
        name="adaln",
    )(cp, ada_w, ada_b.reshape(depth, 1, n))
    return out[:, :bsz].reshape(depth, bsz, 6, 1, d)


def _modulated(x_ref, g_ref, sc_ref, sh_ref):
    h = _rms(x_ref[0], g_ref[...])
    return (h * (1.0 + sc_ref[0]) + sh_ref[0]).astype(bf16)


def _inproj_even_kernel(x_ref, g_ref, sc_ref, sh_ref, w_ref, oa_ref, ob_ref, *, nc):
    hb = _modulated(x_ref, g_ref, sc_ref, sh_ref)
    na = oa_ref.shape[-1]
    for c in range(0, na, nc):
        oa_ref[0, :, c:c + nc] = jnp.dot(hb, w_ref[:, c:c + nc], preferred_element_type=f32).astype(bf16)
    for c in range(0, ob_ref.shape[-1], nc):
        ob_ref[0, :, c:c + nc] = jnp.dot(hb, w_ref[:, na + c:na + c + nc], preferred_element_type=f32)


def _inproj_even(x, g, sc, sh, w, tm=512):
    bsz, s, d = x.shape
    na = 3 * A_HEADS * 2 * HEAD_DIM
    nb = w.shape[1] - na
    row = lambda b, i: (b, i, 0)
    per_b = lambda b, i: (b, 0, 0)
    return pl.pallas_call(
        functools.partial(_inproj_even_kernel, nc=512),
        grid=(bsz, s // tm),
        in_specs=[
            pl.BlockSpec((1, tm, d), row),
            pl.BlockSpec((1, d), lambda b, i: (0, 0)),
            pl.BlockSpec((1, 1, d), per_b),
            pl.BlockSpec((1, 1, d), per_b),
            _resident(w.shape),
        ],
        out_specs=[pl.BlockSpec((1, tm, na), row), pl.BlockSpec((1, tm, nb), row)],
        out_shape=[jax.ShapeDtypeStruct((bsz, s, na), bf16), jax.ShapeDtypeStruct((bsz, s, nb), f32)],
        compiler_params=_params("parallel", "parallel"),
        name="inproj_even",
    )(x, g, sc, sh, w)


def _inproj_odd_kernel(x_ref, g_ref, sc_ref, sh_ref, w_ref, wg1_ref, wg2_ref, bg2_ref,
                       oqk_ref, ovr_ref, ola_ref, *, nc):
    hb = _modulated(x_ref, g_ref, sc_ref, sh_ref)
    glr = jnp.dot(hb, wg1_ref[...], preferred_element_type=f32)
    z = jnp.dot(glr.astype(bf16), wg2_ref[...], preferred_element_type=f32) + bg2_ref[...]
    ola_ref[0] = (jnp.minimum(z, 0.0) - jnp.log(1.0 + jnp.exp(-jnp.abs(z)))) * (1.0 / C_TAU)
    nqk = oqk_ref.shape[-1]
    for c in range(0, nqk, nc):
        oqk_ref[0, :, c:c + nc] = jnp.dot(hb, w_ref[:, c:c + nc], preferred_element_type=f32).astype(bf16)
    for c in range(0, ovr_ref.shape[-1], nc):
        ovr_ref[0, :, c:c + nc] = jnp.dot(hb, w_ref[:, nqk + c:nqk + c + nc], preferred_element_type=f32).astype(bf16)


def _inproj_odd(x, g, sc, sh, w, wg1, wg2, bg2, tm=512):
    bsz, s, d = x.shape
    nqk = 2 * C_HEADS * C_DK
    nvr = 2 * C_HEADS * C_DV
    nla = C_HEADS * C_DK
    row = lambda b, i: (b, i, 0)
    per_b = lambda b, i: (b, 0, 0)
    return pl.pallas_call(
        functools.partial(_inproj_odd_kernel, nc=512),
        grid=(bsz, s // tm),
        in_specs=[
            pl.BlockSpec((1, tm, d), row),
            pl.BlockSpec((1, d), lambda b, i: (0, 0)),
            pl.BlockSpec((1, 1, d), per_b),
            pl.BlockSpec((1, 1, d), per_b),
            _resident((d, nqk + nvr)),
            _resident(wg1.shape),
            _resident(wg2.shape),
            _resident(bg2.shape),
        ],
        out_specs=[pl.BlockSpec((1, tm, nqk), row), pl.BlockSpec((1, tm, nvr), row), pl.BlockSpec((1, tm, nla), row)],
        out_shape=[jax.ShapeDtypeStruct((bsz, s, nqk), bf16), jax.ShapeDtypeStruct((bsz, s, nvr), bf16),
                   jax.ShapeDtypeStruct((bsz, s, nla), f32)],
        compiler_params=_params("parallel", "parallel"),
        name="inproj_odd",
    )(x, g, sc, sh, w, wg1, wg2, bg2)


def _diff_attn_kernel(slope_ref, lam_ref, q_ref, k_ref, v_ref, g_ref, o_ref,
                      ks_sc, vt_sc, bias_sc, sta_sc, stb_sc, cma_sc, cmb_sc, m_sc, acc_sc, *, t, lam_init):
    h = pl.program_id(1)
    i = pl.program_id(2)
    nblk = k_ref.shape[1] // t
    width = 2 * HEAD_DIM
    slope2 = slope_ref[h] * LOG2E

    @pl.when(i == 0)
    def _():
        lane = lax.broadcasted_iota(jnp.int32, (t, LANES), 1)
        first_half = lane < HEAD_DIM
        ones_row = jnp.where(lax.broadcasted_iota(jnp.int32, (ONES_ROWS, t), 0) == 0, 1.0, 0.0).astype(bf16)
        for c in range(nblk):
            kb = k_ref[0, c * t:(c + 1) * t, :]
            zero = jnp.zeros_like(kb)
            ks_sc[c, 0:t, :] = jnp.where(first_half, kb, zero)
            ks_sc[c, t:2 * t, :] = jnp.where(first_half, zero, kb)
            vt_sc[c, 0:width, :] = v_ref[0, c * t:(c + 1) * t, :].astype(f32).T.astype(bf16)
            vt_sc[c, width:width + ONES_ROWS, :] = ones_row
        kr = lax.broadcasted_iota(jnp.int32, (t, 2 * t), 0)
        qc = lax.broadcasted_iota(jnp.int32, (t, 2 * t), 1) % t
        bias_sc[...] = slope2 * (qc - kr).astype(f32)

    qt = (q_ref[0].astype(f32) * (HEAD_DIM ** -0.5 * LOG2E)).T.astype(bf16)
    lane = lax.broadcasted_iota(jnp.int32, (1, 2 * t), 1)
    one_block_further = jnp.where(lane >= t, slope2 * t, 0.0)

    def scores_into(st_ref, cm_ref, j, q0, q1):
        st = jnp.dot(ks_sc[j], qt[:, q0:q1], preferred_element_type=f32)
        for c in range(2):
            s = st[c * t:(c + 1) * t] - bias_sc[:, q0:q1]
            st_ref[c * t:(c + 1) * t, q0:q1] = s
            cm_ref[c, :, q0:q1] = jnp.max(s, axis=0, keepdims=True)

    def softmax_pv(st_ref, cm_ref, j, q0, q1, diagonal_from):
        vt = vt_sc[j]
        shift = slope2 * ((2 * i - j) * t).astype(f32) + one_block_further[:, q0:q1]
        for c in range(2):
            s = st_ref[c * t:(c + 1) * t, q0:q1]
            if diagonal_from is not None:
                kr = lax.broadcasted_iota(jnp.int32, s.shape, 0)
                qc = lax.broadcasted_iota(jnp.int32, s.shape, 1) + (q0 - diagonal_from)
                s = jnp.where(jnp.logical_or(qc >= t, kr <= qc), s, -jnp.inf)
                cmax = jnp.max(s, axis=0, keepdims=True)
            else:
                cmax = cm_ref[c, :, q0:q1]
            m_old = m_sc[c, :, q0:q1]
            m_new = jnp.maximum(m_old, cmax - shift)
            p = jnp.exp2(s - (m_new + shift)).astype(bf16)
            pv = jnp.dot(vt, p, preferred_element_type=f32)
            acc_sc[c, :, q0:q1] = jnp.exp2(m_old - m_new) * acc_sc[c, :, q0:q1] + pv
            m_sc[c, :, q0:q1] = m_new

    def overlapped(nxt, cur):
        for q0 in range(0, 2 * t, QUERY_CHUNK):
            if nxt is not None and q0 >= nxt[3]:
                scores_into(nxt[0], nxt[1], nxt[2], q0, q0 + QUERY_CHUNK)
            if cur is not None and q0 >= cur[3]:
                softmax_pv(cur[0], cur[1], cur[2], q0, q0 + QUERY_CHUNK, cur[4])

    m_sc[...] = jnp.full(m_sc.shape, -jnp.inf, f32)
    acc_sc[...] = jnp.zeros(acc_sc.shape, f32)
    overlapped((sta_sc, cma_sc, 0, 0), None)

    def body(jj, carry):
        j = 2 * jj
        overlapped((stb_sc, cmb_sc, j + 1, 0), (sta_sc, cma_sc, j, 0, None))
        overlapped((sta_sc, cma_sc, j + 2, 0), (stb_sc, cmb_sc, j + 1, 0, None))
        return carry

    lax.fori_loop(0, i, body, 0)

    overlapped((stb_sc, cmb_sc, 2 * i + 1, t), (sta_sc, cma_sc, 2 * i, 0, 0))
    overlapped(None, (stb_sc, cmb_sc, 2 * i + 1, t, t))

    lp = lam_ref[...]
    lam = (jnp.exp(jnp.sum(lp[0:1] * lp[1:2], axis=-1, keepdims=True))
           - jnp.exp(jnp.sum(lp[2:3] * lp[3:4], axis=-1, keepdims=True)) + lam_init)
    outs = [acc_sc[c, 0:width, :] / acc_sc[c, width:width + 1, :] for c in range(2)]
    ot = outs[0] - lam * outs[1]
    ot = ot * lax.rsqrt(jnp.mean(ot * ot, axis=0, keepdims=True) + EPS) * (g_ref[...] * (1.0 - lam_init))
    o_ref[0] = ot.T.astype(bf16)


def _diff_attn(pa, slopes, lam_p, subln_g, lam_init, t=512):
    bsz, s, _ = pa.shape
    width = 2 * HEAD_DIM
    nblk = s // t
    return pl.pallas_call(
        functools.partial(_diff_attn_kernel, t=t, lam_init=lam_init),
        grid=(bsz, A_HEADS, nblk // 2),
        in_specs=[
            pl.BlockSpec(memory_space=pltpu.SMEM),
            pl.BlockSpec(lam_p.shape, lambda b, h, i: (0, 0)),
            pl.BlockSpec((1, 2 * t, width), lambda b, h, i: (b, i, h)),
            pl.BlockSpec((1, s, width), lambda b, h, i: (b, 0, A_HEADS + h)),
            pl.BlockSpec((1, s, width), lambda b, h, i: (b, 0, 2 * A_HEADS + h)),
            pl.BlockSpec((width, 1), lambda b, h, i: (0, 0)),
        ],
        out_specs=pl.BlockSpec((1, 2 * t, width), lambda b, h, i: (b, i, h)),
        out_shape=jax.ShapeDtypeStruct((bsz, s, A_HEADS * width), bf16),
        scratch_shapes=[pltpu.VMEM((nblk, 2 * t, width), bf16),
                        pltpu.VMEM((nblk, width + ONES_ROWS, t), bf16),
                        pltpu.VMEM((t, 2 * t), f32),
                        pltpu.VMEM((2 * t, 2 * t), f32),
                        pltpu.VMEM((2 * t, 2 * t), f32),
                        pltpu.VMEM((2, 1, 2 * t), f32),
                        pltpu.VMEM((2, 1, 2 * t), f32),
                        pltpu.VMEM((2, 1, 2 * t), f32),
                        pltpu.VMEM((2, width + ONES_ROWS, 2 * t), f32)],
        compiler_params=_params("parallel", "parallel", "arbitrary"),
        name="diff_attn",
    )(slopes, lam_p, pa, pa, pa, subln_g)


def _dilated_kernel(slope_ref, q_ref, kp_ref, kc_ref, vp_ref, vc_ref, o_ref,
                    acc_sc, m_sc, l_sc, *, span):
    hp = pl.program_id(1)
    sidx = pl.program_id(2)
    blk = BAND
    nblocks = span // blk

    lane = lax.broadcasted_iota(jnp.int32, (blk, LANES), 1)
    head_a = lane < HEAD_DIM
    a = lax.broadcasted_iota(jnp.int32, (blk, 2 * blk), 0)
    bi = lax.broadcasted_iota(jnp.int32, (blk, 2 * blk), 1)
    rel = blk + a - bi
    valid = (rel >= 0) & (rel <= BAND)
    before_start = bi < blk
    steps = jnp.where(valid, rel.astype(f32), jnp.inf)
    slopes = (slope_ref[2 * hp], slope_ref[2 * hp + 1])
    for br, d in enumerate(DILATIONS):
        per_stream = nblocks // d
        rates = [sl * (LOG2E * d) for sl in slopes]

        def one_block(r, n, br=br, d=d, rates=rates):
            q_start = n * (blk * d) + r
            qb = (q_ref[0, pl.ds(q_start, blk, stride=d), :] * (HEAD_DIM ** -0.5 * LOG2E)).astype(bf16)
            if n == 0:
                tail = pl.ds(span - blk * d + r, blk, stride=d)
                own = pl.ds(r, blk, stride=d)
                kk = jnp.concatenate([kp_ref[0, tail, :], kc_ref[0, own, :]], axis=0).astype(bf16)
                vv = jnp.concatenate([vp_ref[0, tail, :], vc_ref[0, own, :]], axis=0).astype(bf16)
            else:
                both = pl.ds((n - 1) * (blk * d) + r, 2 * blk, stride=d)
                kk = kc_ref[0, both, :].astype(bf16)
                vv = vc_ref[0, both, :].astype(bf16)
            zero = jnp.zeros_like(qb)
            qh = (jnp.where(head_a, qb, zero), jnp.where(head_a, zero, qb))
            outs, ms, ls = [], [], []
            for c in range(2):
                s = lax.dot_general(qh[c], kk, NT_DIMS, preferred_element_type=f32) - rates[c] * steps
                if n == 0:
                    s = jnp.where(jnp.logical_and(sidx == 0, before_start), -jnp.inf, s)
                m = jnp.max(s, axis=-1, keepdims=True)
                p = jnp.exp2(s - m)
                ls.append(jnp.sum(p, axis=-1, keepdims=True))
                ms.append(m)
                outs.append(jnp.dot(p.astype(bf16), vv, preferred_element_type=f32))
            dst = pl.ds(q_start, blk, stride=d)
            acc_sc[br, dst, :] = jnp.where(head_a, outs[0], outs[1])
            m_sc[br, dst, :] = jnp.where(head_a, ms[0], ms[1])
            l_sc[br, dst, :] = jnp.where(head_a, ls[0], ls[1])

        for r in range(d):
            for n in range(per_stream):
                one_block(r, n)

    m_all = jnp.maximum(jnp.maximum(m_sc[0], m_sc[1]), m_sc[2])
    num = jnp.zeros((span, LANES), f32)
    den = jnp.zeros((span, LANES), f32)
    for br in range(len(DILATIONS)):
        w = jnp.exp2(m_sc[br] - m_all)
        num = num + w * acc_sc[br]
        den = den + w * l_sc[br]
    o_ref[0] = (num / den).astype(bf16)


def _dilated_attn(pb, slopes, span=2048):
    bsz, s, _ = pb.shape
    pairs = B_HEADS // 2
    cur = lambda off: (lambda b, h, i: (b, i, off + h))
    prev = lambda off: (lambda b, h, i: (b, jnp.maximum(i - 1, 0), off + h))
    blk = (1, span, LANES)
    nbr = len(DILATIONS)
    return pl.pallas_call(
        functools.partial(_dilated_kernel, span=span),
        grid=(bsz, pairs, s // span),
        in_specs=[
            pl.BlockSpec(memory_space=pltpu.SMEM),
            pl.BlockSpec(blk, cur(0)),
            pl.BlockSpec(blk, prev(pairs)),
            pl.BlockSpec(blk, cur(pairs)),
            pl.BlockSpec(blk, prev(2 * pairs)),
            pl.BlockSpec(blk, cur(2 * pairs)),
        ],
        out_specs=pl.BlockSpec(blk, cur(0)),
        out_shape=jax.ShapeDtypeStruct((bsz, s, B_HEADS * HEAD_DIM), bf16),
        scratch_shapes=[pltpu.VMEM((nbr, span, LANES), f32), pltpu.VMEM((nbr, span, LANES), f32),
                        pltpu.VMEM((nbr, span, LANES), f32)],
        compiler_params=_params("parallel", "parallel", "parallel"),
        name="dilated_attn",
    )(slopes, pb, pb, pb, pb, pb)


def _gla_kernel(q_ref, k_ref, v_ref, r_ref, la_ref, hg_ref, o_ref, state_sc, b_sc, q_sc, *, chunk):
    tb = q_ref.shape[1]

    @pl.when(pl.program_id(2) == 0)
    def _():
        state_sc[...] = jnp.zeros_like(state_sc)

    row = lax.broadcasted_iota(jnp.int32, (chunk, chunk), 0)
    col = lax.broadcasted_iota(jnp.int32, (chunk, chunk), 1)
    causal = row >= col
    tril = jnp.where(causal, 1.0, 0.0).astype(bf16)
    steep = jnp.min(la_ref[0]) * chunk < -80.0

    def intra_exact(b, q, k):
        b_sc[...] = b
        q_sc[...] = q

        def one_row(i, at):
            bi = b_sc[pl.ds(i, 1), :]
            qi = q_sc[pl.ds(i, 1), :]
            g = jnp.exp(jnp.minimum(bi - b, 0.0)) * k * qi
            colv = jnp.sum(g, axis=-1, keepdims=True)
            return jnp.where(col == i, colv, at)

        return lax.fori_loop(0, chunk, one_row, jnp.zeros((chunk, chunk), f32)).T

    def step(exact):
        sls = [pl.ds(ci * chunk, chunk) for ci in range(tb // chunk)]
        bs = []
        for sl in sls:
            la = la_ref[0, sl, :]
            hi = la.astype(bf16)
            lo = (la - hi.astype(f32)).astype(bf16)
            bs.append(jnp.dot(tril, hi, preferred_element_type=f32)
                      + jnp.dot(tril, lo, preferred_element_type=f32))
        qs = [q_ref[0, sl, :].astype(f32) * (C_DK ** -0.5) for sl in sls]
        ks = [k_ref[0, sl, :].astype(f32) for sl in sls]
        qts = [(q * jnp.exp(b)).astype(bf16) for q, b in zip(qs, bs)]
        lasts = [b[chunk - 1:chunk, :] for b in bs]
        kds = [(k * jnp.exp(bl - b)).astype(bf16) for k, b, bl in zip(ks, bs, lasts)]
        if exact:
            ats = [intra_exact(b, q, k) for b, q, k in zip(bs, qs, ks)]
        else:
            kts = [(k * jnp.exp(-b)).astype(bf16) for k, b in zip(ks, bs)]
            ats = [lax.dot_general(qt, kt, NT_DIMS, preferred_element_type=f32) for qt, kt in zip(qts, kts)]
        updates = [lax.dot_general(v_ref[0, sl, :], kd, TN_DIMS, preferred_element_type=f32)
                   for sl, kd in zip(sls, kds)]
        intras = [jnp.dot(jnp.where(causal, a, 0.0).astype(bf16), v_ref[0, sl, :], preferred_element_type=f32)
                  for sl, a in zip(sls, ats)]
        states = [state_sc[...]]
        for bl, update in zip(lasts, updates):
            states.append(states[-1] * jnp.exp(bl) + update)
        state_sc[...] = states[-1]
        inters = [lax.dot_general(qt, st.astype(bf16), NT_DIMS, preferred_element_type=f32)
                  for qt, st in zip(qts, states[:-1])]
        for sl, o_intra, o_inter in zip(sls, intras, inters):
            rr = r_ref[0, sl, :].astype(f32)
            o_ref[0, sl, :] = (_rms(o_intra + o_inter, hg_ref[...]) * (rr * jax.nn.sigmoid(rr))).astype(bf16)

    @pl.when(jnp.logical_not(steep))
    def _():
        step(False)

    @pl.when(steep)
    def _():
        step(True)


def _gla(pqk, pvr, pla, head_g, tb=1024, chunk=128):
    bsz, s, _ = pqk.shape
    return pl.pallas_call(
        functools.partial(_gla_kernel, chunk=chunk),
        grid=(bsz, C_HEADS, s // tb),
        in_specs=[
            pl.BlockSpec((1, tb, C_DK), lambda b, h, i: (b, i, h)),
            pl.BlockSpec((1, tb, C_DK), lambda b, h, i: (b, i, C_HEADS + h)),
            pl.BlockSpec((1, tb, C_DV), lambda b, h, i: (b, i, h)),
            pl.BlockSpec((1, tb, C_DV), lambda b, h, i: (b, i, C_HEADS + h)),
            pl.BlockSpec((1, tb, C_DK), lambda b, h, i: (b, i, h)),
            pl.BlockSpec((1, C_DV), lambda b, h, i: (0, 0)),
        ],
        out_specs=pl.BlockSpec((1, tb, C_DV), lambda b, h, i: (b, i, h)),
        out_shape=jax.ShapeDtypeStruct((bsz, s, C_HEADS * C_DV), bf16),
        scratch_shapes=[pltpu.VMEM((C_DV, C_DK), f32),
                        pltpu.VMEM((chunk, C_DK), f32), pltpu.VMEM((chunk, C_DK), f32)],
        compiler_params=_params("parallel", "parallel", "arbitrary"),
        name="gla",
    )(pqk, pqk, pvr, pvr, pla, head_g)


def _post_kernel(x_ref, ya_ref, yb_ref, wo_ref, gate1_ref, sc2_ref, sh2_ref, gate2_ref, ng_ref,
                 wg_ref, wu_ref, wd_ref, o_ref, x1_sc, hb_sc, *, hc, groups, piece):
    half = ya_ref.shape[-1]
    tm = x_ref.shape[1]
    gr = tm // groups
    hidden = wg_ref.shape[1]
    chunks = [(c, min(c + hc, hidden)) for c in range(0, hidden, hc)]

    x1_sc[...] = (jnp.dot(ya_ref[0], wo_ref[0:half, :], preferred_element_type=f32)
                  + jnp.dot(yb_ref[0], wo_ref[half:2 * half, :], preferred_element_type=f32))

    def before_ffn(start):
        rows = pl.ds(start, piece)
        x1 = x_ref[0, rows, :] + (1.0 + gate1_ref[0]) * _rms(x1_sc[rows, :], ng_ref[0:1, :])
        x1_sc[rows, :] = x1
        hb_sc[rows, :] = (_rms(x1, ng_ref[1:2, :]) * (1.0 + sc2_ref[0]) + sh2_ref[0]).astype(bf16)

    def after_ffn(start, y2):
        rows = pl.ds(start, piece)
        o_ref[0, rows, :] = x1_sc[rows, :] + (1.0 + gate2_ref[0]) * _rms(y2, ng_ref[2:3, :])

    for p in range(0, gr, piece):
        before_ffn(p)
    done = None
    for g in range(groups):
        side = []
        if done is not None:
            side += [functools.partial(after_ffn, (g - 1) * gr + p, done[p:p + piece]) for p in range(0, gr, piece)]
        if g + 1 < groups:
            side += [functools.partial(before_ffn, (g + 1) * gr + p) for p in range(0, gr, piece)]
        hb = hb_sc[pl.ds(g * gr, gr), :]
        acc = jnp.zeros((gr, x1_sc.shape[1]), f32)
        for ci, (c, e) in enumerate(chunks):
            gt = jnp.dot(hb, wg_ref[:, c:e], preferred_element_type=f32)
            up = jnp.dot(hb, wu_ref[:, c:e], preferred_element_type=f32)
            act = (gt * jax.nn.sigmoid(gt) * up).astype(bf16)
            acc = acc + jnp.dot(act, wd_ref[c:e, :], preferred_element_type=f32)
            for task in side[ci * len(side) // len(chunks):(ci + 1) * len(side) // len(chunks)]:
                task()
        done = acc
    for p in range(0, gr, piece):
        after_ffn((groups - 1) * gr + p, done[p:p + piece])


def _post(x, ya, yb, cols, wo, gate1, sc2, sh2, gate2, ng, layer, wg, wu, wd, tm=1024, hc=512):
    bsz, s, d = x.shape
    half = wo.shape[0] // 2
    hidden = wg.shape[-1]
    row = lambda b, i: (b, i, 0)
    per_b = lambda b, i: (b, 0, 0)
    mod = pl.BlockSpec((1, 1, d), per_b)
    slab = lambda shape: pl.BlockSpec((None,) + shape, lambda b, i: (layer, 0, 0), pipeline_mode=pl.Buffered(1))
    return pl.pallas_call(
        functools.partial(_post_kernel, hc=hc, groups=2, piece=128),
        grid=(bsz, s // tm),
        in_specs=[
            pl.BlockSpec((1, tm, d), row),
            pl.BlockSpec((1, tm, half), lambda b, i: (b, i, cols[0])),
            pl.BlockSpec((1, tm, half), lambda b, i: (b, i, cols[1])),
            _resident(wo.shape),
            mod, mod, mod, mod,
            _resident(ng.shape),
            slab((d, hidden)), slab((d, hidden)), slab((hidden, d)),
        ],
        out_specs=pl.BlockSpec((1, tm, d), row),
        out_shape=jax.ShapeDtypeStruct((bsz, s, d), f32),
        scratch_shapes=[pltpu.VMEM((tm, d), f32), pltpu.VMEM((tm, d), bf16)],
        compiler_params=_params("parallel", "parallel"),
        name="post",
    )(x, ya, yb, wo, gate1, sc2, sh2, gate2, ng, wg, wu, wd)


def _alibi_slopes(n):
    return 2.0 ** (-8.0 * jnp.arange(1, n + 1, dtype=f32) / n)


def kernel(x, c, ada_w, ada_b, norm_g, ev_w_in, ev_lambda, ev_subln_g, ev_w_out, od_w_in, od_w_g2,
           od_b_g2, od_head_g, od_w_out, ffn_w_gate, ffn_w_up, ffn_w_down):
    depth = ada_w.shape[0]
    mod = _adaln(c, ada_w, ada_b)
    ffn_w = (ffn_w_gate.astype(bf16), ffn_w_up.astype(bf16), ffn_w_down.astype(bf16))
    for l in range(depth):
        sh1, sc1, g1, sh2, sc2, g2 = (mod[l, :, j] for j in range(6))
        pre_g = norm_g[l, 0:1]
        if l % 2 == 0:
            e = l // 2
            lam_init = 0.8 - 0.6 * math.exp(-0.3 * l)
            pa, pb = _inproj_even(x, pre_g, sc1, sh1, ev_w_in[e].astype(bf16))
            ya = _diff_attn(pa, _alibi_slopes(A_HEADS), ev_lambda[e], ev_subln_g[e][:, None], lam_init)
            yb = _dilated_attn(pb, _alibi_slopes(B_HEADS))
            cols = (0, 0)
            wo = ev_w_out[e]
        else:
            o = l // 2
            w = od_w_in[o].astype(bf16)
            nmain = w.shape[1] - C_GATE_RANK
            wg1 = jnp.zeros((w.shape[0], LANES), bf16).at[:, :C_GATE_RANK].set(w[:, nmain:])
            wg2 = jnp.zeros((LANES, od_w_g2.shape[-1]), bf16).at[:C_GATE_RANK].set(od_w_g2[o].astype(bf16))
            pqk, pvr, pla = _inproj_odd(x, pre_g, sc1, sh1, w, wg1, wg2, od_b_g2[o][None, :])
            ya = yb = _gla(pqk, pvr, pla, od_head_g[o][None, :])
            cols = (0, 1)
            wo = od_w_out[o]
        x = _post(x, ya, yb, cols, wo.astype(bf16), g1, sc2, sh2, g2, norm_g[l, 1:4], l, *ffn_w)
    return x
```

```python
import functools
import math

import jax
import jax.numpy as jnp
from jax import lax
from jax.experimental import pallas as pl
from jax.experimental.pallas import tpu as pltpu

f32 = jnp.float32
bf16 = jnp.bfloat16

EPS = 1e-6
LANES = 128
HEAD_DIM = 64
A_HEADS = 4
B_HEADS = 8
DILATIONS = (1, 4, 16)
BAND = 128
C_HEADS = 4
C_DK = 128
C_DV = 256
C_GATE_RANK = 16
C_TAU = 16.0
VMEM_LIMIT = 56 * 1024 * 1024
LOG2E = math.log2(math.e)
QUERY_CHUNK = 256
ONES_ROWS = 16

NT_DIMS = (((1,), (1,)), ((), ()))
TN_DIMS = (((0,), (0,)), ((), ()))


def _rms(x, g):
    return x * lax.rsqrt(jnp.mean(x * x, axis=-1, keepdims=True) + EPS) * g


def _params(*sem):
    return pltpu.CompilerParams(dimension_semantics=sem, vmem_limit_bytes=VMEM_LIMIT)


def _resident(shape):
    nd = len(shape)
    return pl.BlockSpec(shape, lambda *_: (0,) * nd, pipeline_mode=pl.Buffered(1))


def _adaln_kernel(c_ref, w_ref, b_ref, o_ref):
    c = c_ref[...]
    a = (c * jax.nn.sigmoid(c)).astype(bf16)
    o_ref[0] = jnp.dot(a, w_ref[0].astype(bf16), preferred_element_type=f32) + b_ref[0]


def _adaln(c, ada_w, ada_b):
    depth, d, n = ada_w.shape
    bsz = c.shape[0]
    rows = 8
    cp = jnp.zeros((rows, d), f32).at[:bsz].set(c)
    tn = 3072
    out = pl.pallas_call(
        _adaln_kernel,
        grid=(depth, n // tn),
        in_specs=[
            pl.BlockSpec((rows, d), lambda l, j: (0, 0)),
            pl.BlockSpec((1, d, tn), lambda l, j: (l, 0, j)),
            pl.BlockSpec((1, 1, tn), lambda l, j: (l, 0, j)),
        ],
        out_specs=pl.BlockSpec((1, rows, tn), lambda l, j: (l, 0, j)),
        out_shape=jax.ShapeDtypeStruct((depth, rows, n), f32),
        compiler_params=_params("parallel", "parallel"),
        name="adaln",
    )(cp, ada_w, ada_b.reshape(depth, 1, n))
    return out[:, :bsz].reshape(depth, bsz, 6, 1, d)


def _modulated(x_ref, g_ref, sc_ref, sh_ref):
    h = _rms(x_ref[0], g_ref[...])
    return (h * (1.0 + sc_ref[0]) + sh_ref[0]).astype(bf16)


def _inproj_even_kernel(x_ref, g_ref, sc_ref, sh_ref, w_ref, oa_ref, ob_ref, *, nc):
    hb = _modulated(x_ref, g_ref, sc_ref, sh_ref)
    na = oa_ref.shape[-1]
    for c in range(0, na, nc):
        oa_ref[0, :, c:c + nc] = jnp.dot(hb, w_ref[:, c:c + nc], preferred_element_type=f32).astype(bf16)
    for c in range(0, ob_ref.shape[-1], nc):
        ob_ref[0, :, c:c + nc] = jnp.dot(hb, w_ref[:, na + c:na + c + nc], preferred_element_type=f32)


def _inproj_even(x, g, sc, sh, w, tm=1024):
    bsz, s, d = x.shape
    na = 3 * A_HEADS * 2 * HEAD_DIM
    nb = w.shape[1] - na
    row = lambda b, i: (b, i, 0)
    per_b = lambda b, i: (b, 0, 0)
    return pl.pallas_call(
        functools.partial(_inproj_even_kernel, nc=512),
        grid=(bsz, s // tm),
        in_specs=[
            pl.BlockSpec((1, tm, d), row),
            pl.BlockSpec((1, d), lambda b, i: (0, 0)),
            pl.BlockSpec((1, 1, d), per_b),
            pl.BlockSpec((1, 1, d), per_b),
            _resident(w.shape),
        ],
        out_specs=[pl.BlockSpec((1, tm, na), row), pl.BlockSpec((1, tm, nb), row)],
        out_shape=[jax.ShapeDtypeStruct((bsz, s, na), bf16), jax.ShapeDtypeStruct((bsz, s, nb), f32)],
        compiler_params=_params("parallel", "parallel"),
        name="inproj_even",
    )(x, g, sc, sh, w)


def _inproj_odd_kernel(x_ref, g_ref, sc_ref, sh_ref, w_ref, wg1_ref, wg2_ref, bg2_ref,
                       oqk_ref, ovr_ref, ola_ref, *, nc):
    hb = _modulated(x_ref, g_ref, sc_ref, sh_ref)
    glr = jnp.dot(hb, wg1_ref[...], preferred_element_type=f32)
    z = jnp.dot(glr.astype(bf16), wg2_ref[...], preferred_element_type=f32) + bg2_ref[...]
    ola_ref[0] = (jnp.minimum(z, 0.0) - jnp.log(1.0 + jnp.exp(-jnp.abs(z)))) * (1.0 / C_TAU)
    nqk = oqk_ref.shape[-1]
    for c in range(0, nqk, nc):
        oqk_ref[0, :, c:c + nc] = jnp.dot(hb, w_ref[:, c:c + nc], preferred_element_type=f32).astype(bf16)
    for c in range(0, ovr_ref.shape[-1], nc):
        ovr_ref[0, :, c:c + nc] = jnp.dot(hb, w_ref[:, nqk + c:nqk + c + nc], preferred_element_type=f32).astype(bf16)


def _inproj_odd(x, g, sc, sh, w, wg1, wg2, bg2, tm=1024):
    bsz, s, d = x.shape
    nqk = 2 * C_HEADS * C_DK
    nvr = 2 * C_HEADS * C_DV
    nla = C_HEADS * C_DK
    row = lambda b, i: (b, i, 0)
    per_b = lambda b, i: (b, 0, 0)
    return pl.pallas_call(
        functools.partial(_inproj_odd_kernel, nc=512),
        grid=(bsz, s // tm),
        in_specs=[
            pl.BlockSpec((1, tm, d), row),
            pl.BlockSpec((1, d), lambda b, i: (0, 0)),
            pl.BlockSpec((1, 1, d), per_b),
            pl.BlockSpec((1, 1, d), per_b),
            _resident((d, nqk + nvr)),
            _resident(wg1.shape),
            _resident(wg2.shape),
            _resident(bg2.shape),
        ],
        out_specs=[pl.BlockSpec((1, tm, nqk), row), pl.BlockSpec((1, tm, nvr), row), pl.BlockSpec((1, tm, nla), row)],
        out_shape=[jax.ShapeDtypeStruct((bsz, s, nqk), bf16), jax.ShapeDtypeStruct((bsz, s, nvr), bf16),
                   jax.ShapeDtypeStruct((bsz, s, nla), f32)],
        compiler_params=_params("parallel", "parallel"),
        name="inproj_odd",
    )(x, g, sc, sh, w, wg1, wg2, bg2)


def _diff_attn_kernel(slope_ref, lam_ref, q_ref, k_ref, v_ref, g_ref, o_ref,
                      ks_sc, vt_sc, bias_sc, sta_sc, stb_sc, cma_sc, cmb_sc, m_sc, acc_sc, *, t, lam_init):
    h = pl.program_id(1)
    i = pl.program_id(2)
    nblk = k_ref.shape[1] // t
    width = 2 * HEAD_DIM
    slope2 = slope_ref[h] * LOG2E

    @pl.when(i == 0)
    def _():
        lane = lax.broadcasted_iota(jnp.int32, (t, LANES), 1)
        first_half = lane < HEAD_DIM
        ones_row = jnp.where(lax.broadcasted_iota(jnp.int32, (ONES_ROWS, t), 0) == 0, 1.0, 0.0).astype(bf16)
        for c in range(nblk):
            kb = k_ref[0, c * t:(c + 1) * t, :]
            zero = jnp.zeros_like(kb)
            ks_sc[c, 0:t, :] = jnp.where(first_half, kb, zero)
            ks_sc[c, t:2 * t, :] = jnp.where(first_half, zero, kb)
            vt_sc[c, 0:width, :] = v_ref[0, c * t:(c + 1) * t, :].astype(f32).T.astype(bf16)
            vt_sc[c, width:width + ONES_ROWS, :] = ones_row
        kr = lax.broadcasted_iota(jnp.int32, (t, 2 * t), 0)
        qc = lax.broadcasted_iota(jnp.int32, (t, 2 * t), 1) % t
        bias_sc[...] = slope2 * (qc - kr).astype(f32)

    qt = (q_ref[0].astype(f32) * (HEAD_DIM ** -0.5 * LOG2E)).T.astype(bf16)
    lane = lax.broadcasted_iota(jnp.int32, (1, 2 * t), 1)
    one_block_further = jnp.where(lane >= t, slope2 * t, 0.0)

    def scores_into(st_ref, cm_ref, j, q0, q1):
        st = jnp.dot(ks_sc[j], qt[:, q0:q1], preferred_element_type=f32)
        for c in range(2):
            s = st[c * t:(c + 1) * t] - bias_sc[:, q0:q1]
            st_ref[c * t:(c + 1) * t, q0:q1] = s
            cm_ref[c, :, q0:q1] = jnp.max(s, axis=0, keepdims=True)

    def softmax_pv(st_ref, cm_ref, j, q0, q1, diagonal_from):
        vt = vt_sc[j]
        shift = slope2 * ((2 * i - j) * t).astype(f32) + one_block_further[:, q0:q1]
        for c in range(2):
            s = st_ref[c * t:(c + 1) * t, q0:q1]
            if diagonal_from is not None:
                kr = lax.broadcasted_iota(jnp.int32, s.shape, 0)
                qc = lax.broadcasted_iota(jnp.int32, s.shape, 1) + (q0 - diagonal_from)
                s = jnp.where(jnp.logical_or(qc >= t, kr <= qc), s, -jnp.inf)
                cmax = jnp.max(s, axis=0, keepdims=True)
            else:
                cmax = cm_ref[c, :, q0:q1]
            m_old = m_sc[c, :, q0:q1]
            m_new = jnp.maximum(m_old, cmax - shift)
            p = jnp.exp2(s - (m_new + shift)).astype(bf16)
            pv = jnp.dot(vt, p, preferred_element_type=f32)
            acc_sc[c, :, q0:q1] = jnp.exp2(m_old - m_new) * acc_sc[c, :, q0:q1] + pv
            m_sc[c, :, q0:q1] = m_new

    def overlapped(nxt, cur):
        for q0 in range(0, 2 * t, QUERY_CHUNK):
            if nxt is not None and q0 >= nxt[3]:
                scores_into(nxt[0], nxt[1], nxt[2], q0, q0 + QUERY_CHUNK)
            if cur is not None and q0 >= cur[3]:
                softmax_pv(cur[0], cur[1], cur[2], q0, q0 + QUERY_CHUNK, cur[4])

    m_sc[...] = jnp.full(m_sc.shape, -jnp.inf, f32)
    acc_sc[...] = jnp.zeros(acc_sc.shape, f32)
    overlapped((sta_sc, cma_sc, 0, 0), None)

    def body(jj, carry):
        j = 2 * jj
        overlapped((stb_sc, cmb_sc, j + 1, 0), (sta_sc, cma_sc, j, 0, None))
        overlapped((sta_sc, cma_sc, j + 2, 0), (stb_sc, cmb_sc, j + 1, 0, None))
        return carry

    lax.fori_loop(0, i, body, 0)

    overlapped((stb_sc, cmb_sc, 2 * i + 1, t), (sta_sc, cma_sc, 2 * i, 0, 0))
    overlapped(None, (stb_sc, cmb_sc, 2 * i + 1, t, t))

    lp = lam_ref[...]
    lam = (jnp.exp(jnp.sum(lp[0:1] * lp[1:2], axis=-1, keepdims=True))
           - jnp.exp(jnp.sum(lp[2:3] * lp[3:4], axis=-1, keepdims=True)) + lam_init)
    outs = [acc_sc[c, 0:width, :] / acc_sc[c, width:width + 1, :] for c in range(2)]
    ot = outs[0] - lam * outs[1]
    ot = ot * lax.rsqrt(jnp.mean(ot * ot, axis=0, keepdims=True) + EPS) * (g_ref[...] * (1.0 - lam_init))
    o_ref[0] = ot.T.astype(bf16)


def _diff_attn(pa, slopes, lam_p, subln_g, lam_init, t=512):
    bsz, s, _ = pa.shape
    width = 2 * HEAD_DIM
    nblk = s // t
    return pl.pallas_call(
        functools.partial(_diff_attn_kernel, t=t, lam_init=lam_init),
        grid=(bsz, A_HEADS, nblk // 2),
        in_specs=[
            pl.BlockSpec(memory_space=pltpu.SMEM),
            pl.BlockSpec(lam_p.shape, lambda b, h, i: (0, 0)),
            pl.BlockSpec((1, 2 * t, width), lambda b, h, i: (b, i, h)),
            pl.BlockSpec((1, s, width), lambda b, h, i: (b, 0, A_HEADS + h)),
            pl.BlockSpec((1, s, width), lambda b, h, i: (b, 0, 2 * A_HEADS + h)),
            pl.BlockSpec((width, 1), lambda b, h, i: (0, 0)),
        ],
        out_specs=pl.BlockSpec((1, 2 * t, width), lambda b, h, i: (b, i, h)),
        out_shape=jax.ShapeDtypeStruct((bsz, s, A_HEADS * width), bf16),
        scratch_shapes=[pltpu.VMEM((nblk, 2 * t, width), bf16),
                        pltpu.VMEM((nblk, width + ONES_ROWS, t), bf16),
                        pltpu.VMEM((t, 2 * t), f32),
                        pltpu.VMEM((2 * t, 2 * t), f32),
                        pltpu.VMEM((2 * t, 2 * t), f32),
                        pltpu.VMEM((2, 1, 2 * t), f32),
                        pltpu.VMEM((2, 1, 2 * t), f32),
                        pltpu.VMEM((2, 1, 2 * t), f32),
                        pltpu.VMEM((2, width + ONES_ROWS, 2 * t), f32)],
        compiler_params=_params("parallel", "parallel", "arbitrary"),
        name="diff_attn",
    )(slopes, lam_p, pa, pa, pa, subln_g)


def _dilated_kernel(slope_ref, q_ref, kp_ref, kc_ref, vp_ref, vc_ref, o_ref,
                    acc_sc, m_sc, l_sc, *, span):
    hp = pl.program_id(1)
    sidx = pl.program_id(2)
    blk = BAND
    nblocks = span // blk

    lane = lax.broadcasted_iota(jnp.int32, (blk, LANES), 1)
    head_a = lane < HEAD_DIM
    a = lax.broadcasted_iota(jnp.int32, (blk, 2 * blk), 0)
    bi = lax.broadcasted_iota(jnp.int32, (blk, 2 * blk), 1)
    rel = blk + a - bi
    valid = (rel >= 0) & (rel <= BAND)
    before_start = bi < blk
    steps = jnp.where(valid, rel.astype(f32), jnp.inf)
    slopes = (slope_ref[2 * hp], slope_ref[2 * hp + 1])
    for br, d in enumerate(DILATIONS):
        per_stream = nblocks // d
        rates = [sl * (LOG2E * d) for sl in slopes]

        def one_block(r, n, br=br, d=d, rates=rates):
            q_start = n * (blk * d) + r
            qb = (q_ref[0, pl.ds(q_start, blk, stride=d), :] * (HEAD_DIM ** -0.5 * LOG2E)).astype(bf16)
            if n == 0:
                tail = pl.ds(span - blk * d + r, blk, stride=d)
                own = pl.ds(r, blk, stride=d)
                kk = jnp.concatenate([kp_ref[0, tail, :], kc_ref[0, own, :]], axis=0).astype(bf16)
                vv = jnp.concatenate([vp_ref[0, tail, :], vc_ref[0, own, :]], axis=0).astype(bf16)
            else:
                both = pl.ds((n - 1) * (blk * d) + r, 2 * blk, stride=d)
                kk = kc_ref[0, both, :].astype(bf16)
                vv = vc_ref[0, both, :].astype(bf16)
            zero = jnp.zeros_like(qb)
            qh = (jnp.where(head_a, qb, zero), jnp.where(head_a, zero, qb))
            outs, ms, ls = [], [], []
            for c in range(2):
                s = lax.dot_general(qh[c], kk, NT_DIMS, preferred_element_type=f32) - rates[c] * steps
                if n == 0:
                    s = jnp.where(jnp.logical_and(sidx == 0, before_start), -jnp.inf, s)
                m = jnp.max(s, axis=-1, keepdims=True)
                p = jnp.exp2(s - m)
                ls.append(jnp.sum(p, axis=-1, keepdims=True))
                ms.append(m)
                outs.append(jnp.dot(p.astype(bf16), vv, preferred_element_type=f32))
            dst = pl.ds(q_start, blk, stride=d)
            acc_sc[br, dst, :] = jnp.where(head_a, outs[0], outs[1])
            m_sc[br, dst, :] = jnp.where(head_a, ms[0], ms[1])
            l_sc[br, dst, :] = jnp.where(head_a, ls[0], ls[1])

        for r in range(d):
            for n in range(per_stream):
                one_block(r, n)

    m_all = jnp.maximum(jnp.maximum(m_sc[0], m_sc[1]), m_sc[2])
    num = jnp.zeros((span, LANES), f32)
    den = jnp.zeros((span, LANES), f32)
    for br in range(len(DILATIONS)):
        w = jnp.exp2(m_sc[br] - m_all)
        num = num + w * acc_sc[br]
        den = den + w * l_sc[br]
    o_ref[0] = (num / den).astype(bf16)


def _dilated_attn(pb, slopes, span=2048):
    bsz, s, _ = pb.shape
    pairs = B_HEADS // 2
    cur = lambda off: (lambda b, h, i: (b, i, off + h))
    prev = lambda off: (lambda b, h, i: (b, jnp.maximum(i - 1, 0), off + h))
    blk = (1, span, LANES)
    nbr = len(DILATIONS)
    return pl.pallas_call(
        functools.partial(_dilated_kernel, span=span),
        grid=(bsz, pairs, s // span),
        in_specs=[
            pl.BlockSpec(memory_space=pltpu.SMEM),
            pl.BlockSpec(blk, cur(0)),
            pl.BlockSpec(blk, prev(pairs)),
            pl.BlockSpec(blk, cur(pairs)),
            pl.BlockSpec(blk, prev(2 * pairs)),
            pl.BlockSpec(blk, cur(2 * pairs)),
        ],
        out_specs=pl.BlockSpec(blk, cur(0)),
        out_shape=jax.ShapeDtypeStruct((bsz, s, B_HEADS * HEAD_DIM), bf16),
        scratch_shapes=[pltpu.VMEM((nbr, span, LANES), f32), pltpu.VMEM((nbr, span, LANES), f32),
                        pltpu.VMEM((nbr, span, LANES), f32)],
        compiler_params=_params("parallel", "parallel", "parallel"),
        name="dilated_attn",
    )(slopes, pb, pb, pb, pb, pb)


def _gla_kernel(q_ref, k_ref, v_ref, r_ref, la_ref, hg_ref, o_ref, state_sc, b_sc, q_sc, *, chunk):
    tb = q_ref.shape[1]

    @pl.when(pl.program_id(2) == 0)
    def _():
        state_sc[...] = jnp.zeros_like(state_sc)

    row = lax.broadcasted_iota(jnp.int32, (chunk, chunk), 0)
    col = lax.broadcasted_iota(jnp.int32, (chunk, chunk), 1)
    causal = row >= col
    tril = jnp.where(causal, 1.0, 0.0).astype(bf16)
    steep = jnp.min(la_ref[0]) * chunk < -80.0

    def intra_exact(b, q, k):
        b_sc[...] = b
        q_sc[...] = q

        def one_row(i, at):
            bi = b_sc[pl.ds(i, 1), :]
            qi = q_sc[pl.ds(i, 1), :]
            g = jnp.exp(jnp.minimum(bi - b, 0.0)) * k * qi
            colv = jnp.sum(g, axis=-1, keepdims=True)
            return jnp.where(col == i, colv, at)

        return lax.fori_loop(0, chunk, one_row, jnp.zeros((chunk, chunk), f32)).T

    def step(exact):
        sls = [pl.ds(ci * chunk, chunk) for ci in range(tb // chunk)]
        bs = []
        for sl in sls:
            la = la_ref[0, sl, :]
            hi = la.astype(bf16)
            lo = (la - hi.astype(f32)).astype(bf16)
            bs.append(jnp.dot(tril, hi, preferred_element_type=f32)
                      + jnp.dot(tril, lo, preferred_element_type=f32))
        qs = [q_ref[0, sl, :].astype(f32) * (C_DK ** -0.5) for sl in sls]
        ks = [k_ref[0, sl, :].astype(f32) for sl in sls]
        qts = [(q * jnp.exp(b)).astype(bf16) for q, b in zip(qs, bs)]
        lasts = [b[chunk - 1:chunk, :] for b in bs]
        kds = [(k * jnp.exp(bl - b)).astype(bf16) for k, b, bl in zip(ks, bs, lasts)]
        if exact:
            ats = [intra_exact(b, q, k) for b, q, k in zip(bs, qs, ks)]
        else:
            kts = [(k * jnp.exp(-b)).astype(bf16) for k, b in zip(ks, bs)]
            ats = [lax.dot_general(qt, kt, NT_DIMS, preferred_element_type=f32) for qt, kt in zip(qts, kts)]
        updates = [lax.dot_general(v_ref[0, sl, :], kd, TN_DIMS, preferred_element_type=f32)
                   for sl, kd in zip(sls, kds)]
        intras = [jnp.dot(jnp.where(causal, a, 0.0).astype(bf16), v_ref[0, sl, :], preferred_element_type=f32)
                  for sl, a in zip(sls, ats)]
        states = [state_sc[...]]
        for bl, update in zip(lasts, updates):
            states.append(states[-1] * jnp.exp(bl) + update)
        state_sc[...] = states[-1]
        inters = [lax.dot_general(qt, st.astype(bf16), NT_DIMS, preferred_element_type=f32)
                  for qt, st in zip(qts, states[:-1])]
        for sl, o_intra, o_inter in zip(sls, intras, inters):
            rr = r_ref[0, sl, :].astype(f32)
            o_ref[0, sl, :] = (_rms(o_intra + o_inter, hg_ref[...]) * (rr * jax.nn.sigmoid(rr))).astype(bf16)

    @pl.when(jnp.logical_not(steep))
    def _():
        step(False)

    @pl.when(steep)
    def _():
        step(True)


def _gla(pqk, pvr, pla, head_g, tb=2048, chunk=128):
    bsz, s, _ = pqk.shape
    return pl.pallas_call(
        functools.partial(_gla_kernel, chunk=chunk),
        grid=(bsz, C_HEADS, s // tb),
        in_specs=[
            pl.BlockSpec((1, tb, C_DK), lambda b, h, i: (b, i, h)),
            pl.BlockSpec((1, tb, C_DK), lambda b, h, i: (b, i, C_HEADS + h)),
            pl.BlockSpec((1, tb, C_DV), lambda b, h, i: (b, i, h)),
            pl.BlockSpec((1, tb, C_DV), lambda b, h, i: (b, i, C_HEADS + h)),
            pl.BlockSpec((1, tb, C_DK), lambda b, h, i: (b, i, h)),
            pl.BlockSpec((1, C_DV), lambda b, h, i: (0, 0)),
        ],
        out_specs=pl.BlockSpec((1, tb, C_DV), lambda b, h, i: (b, i, h)),
        out_shape=jax.ShapeDtypeStruct((bsz, s, C_HEADS * C_DV), bf16),
        scratch_shapes=[pltpu.VMEM((C_DV, C_DK), f32),
                        pltpu.VMEM((chunk, C_DK), f32), pltpu.VMEM((chunk, C_DK), f32)],
        compiler_params=_params("parallel", "parallel", "arbitrary"),
        name="gla",
    )(pqk, pqk, pvr, pvr, pla, head_g)


def _post_kernel(x_ref, ya_ref, yb_ref, wo_ref, gate1_ref, sc2_ref, sh2_ref, gate2_ref, ng_ref,
                 wg_ref, wu_ref, wd_ref, o_ref, x1_sc, hb_sc, *, hc, groups, piece):
    half = ya_ref.shape[-1]
    hidden = wg_ref.shape[1]
    chunks = [(c, min(c + hc, hidden)) for c in range(0, hidden, hc)]
    starts = [sum(groups[:g]) for g in range(len(groups))]

    def pieces(g):
        return range(starts[g], starts[g] + groups[g], piece)

    def out_proj(g):
        rows = pl.ds(starts[g], groups[g])
        x1_sc[rows, :] = (jnp.dot(ya_ref[0, rows, :], wo_ref[0:half, :], preferred_element_type=f32)
                          + jnp.dot(yb_ref[0, rows, :], wo_ref[half:2 * half, :], preferred_element_type=f32))

    def before_ffn(start):
        rows = pl.ds(start, piece)
        x1 = x_ref[0, rows, :] + (1.0 + gate1_ref[0]) * _rms(x1_sc[rows, :], ng_ref[0:1, :])
        x1_sc[rows, :] = x1
        hb_sc[rows, :] = (_rms(x1, ng_ref[1:2, :]) * (1.0 + sc2_ref[0]) + sh2_ref[0]).astype(bf16)

    def after_ffn(start, y2):
        rows = pl.ds(start, piece)
        o_ref[0, rows, :] = x1_sc[rows, :] + (1.0 + gate2_ref[0]) * _rms(y2, ng_ref[2:3, :])

    out_proj(0)
    if len(groups) > 1:
        out_proj(1)
    for p in pieces(0):
        before_ffn(p)
    done = None
    for g in range(len(groups)):
        side = []
        if g + 2 < len(groups):
            side.append(functools.partial(out_proj, g + 2))
        if g + 1 < len(groups):
            side += [functools.partial(before_ffn, p) for p in pieces(g + 1)]
        if done is not None:
            side += [functools.partial(after_ffn, p, done[p - starts[g - 1]:p - starts[g - 1] + piece])
                     for p in pieces(g - 1)]
        hb = hb_sc[pl.ds(starts[g], groups[g]), :]
        acc = jnp.zeros((groups[g], x1_sc.shape[1]), f32)
        for ci, (c, e) in enumerate(chunks):
            gt = jnp.dot(hb, wg_ref[:, c:e], preferred_element_type=f32)
            up = jnp.dot(hb, wu_ref[:, c:e], preferred_element_type=f32)
            act = (gt * jax.nn.sigmoid(gt) * up).astype(bf16)
            acc = acc + jnp.dot(act, wd_ref[c:e, :], preferred_element_type=f32)
            for task in side[ci * len(side) // len(chunks):(ci + 1) * len(side) // len(chunks)]:
                task()
        done = acc
    last = len(groups) - 1
    for p in pieces(last):
        after_ffn(p, done[p - starts[last]:p - starts[last] + piece])


def _post(x, ya, yb, cols, wo, gate1, sc2, sh2, gate2, ng, layer, wg, wu, wd, tm=1024, hc=512):
    bsz, s, d = x.shape
    half = wo.shape[0] // 2
    hidden = wg.shape[-1]
    row = lambda b, i: (b, i, 0)
    per_b = lambda b, i: (b, 0, 0)
    mod = pl.BlockSpec((1, 1, d), per_b)
    slab = lambda shape: pl.BlockSpec((None,) + shape, lambda b, i: (layer, 0, 0), pipeline_mode=pl.Buffered(1))
    return pl.pallas_call(
        functools.partial(_post_kernel, hc=hc, groups=(tm // 4, tm // 2, tm // 4), piece=128),
        grid=(bsz, s // tm),
        in_specs=[
            pl.BlockSpec((1, tm, d), row),
            pl.BlockSpec((1, tm, half), lambda b, i: (b, i, cols[0])),
            pl.BlockSpec((1, tm, half), lambda b, i: (b, i, cols[1])),
            _resident(wo.shape),
            mod, mod, mod, mod,
            _resident(ng.shape),
            slab((d, hidden)), slab((d, hidden)), slab((hidden, d)),
        ],
        out_specs=pl.BlockSpec((1, tm, d), row),
        out_shape=jax.ShapeDtypeStruct((bsz, s, d), f32),
        scratch_shapes=[pltpu.VMEM((tm, d), f32), pltpu.VMEM((tm, d), bf16)],
        compiler_params=_params("parallel", "parallel"),
        name="post",
    )(x, ya, yb, wo, gate1, sc2, sh2, gate2, ng, wg, wu, wd)


def _alibi_slopes(n):
    return 2.0 ** (-8.0 * jnp.arange(1, n + 1, dtype=f32) / n)


def kernel(x, c, ada_w, ada_b, norm_g, ev_w_in, ev_lambda, ev_subln_g, ev_w_out, od_w_in, od_w_g2,
           od_b_g2, od_head_g, od_w_out, ffn_w_gate, ffn_w_up, ffn_w_down):
    depth = ada_w.shape[0]
    mod = _adaln(c, ada_w, ada_b)
    ffn_w = (ffn_w_gate.astype(bf16), ffn_w_up.astype(bf16), ffn_w_down.astype(bf16))
    for l in range(depth):
        sh1, sc1, g1, sh2, sc2, g2 = (mod[l, :, j] for j in range(6))
        pre_g = norm_g[l, 0:1]
        if l % 2 == 0:
            e = l // 2
            lam_init = 0.8 - 0.6 * math.exp(-0.3 * l)
            pa, pb = _inproj_even(x, pre_g, sc1, sh1, ev_w_in[e].astype(bf16))
            ya = _diff_attn(pa, _alibi_slopes(A_HEADS), ev_lambda[e], ev_subln_g[e][:, None], lam_init)
            yb = _dilated_attn(pb, _alibi_slopes(B_HEADS))
            cols = (0, 0)
            wo = ev_w_out[e]
        else:
            o = l // 2
            w = od_w_in[o].astype(bf16)
            nmain = w.shape[1] - C_GATE_RANK
            wg1 = jnp.zeros((w.shape[0], LANES), bf16).at[:, :C_GATE_RANK].set(w[:, nmain:])
            wg2 = jnp.zeros((LANES, od_w_g2.shape[-1]), bf16).at[:C_GATE_RANK].set(od_w_g2[o].astype(bf16))
            pqk, pvr, pla = _inproj_odd(x, pre_g, sc1, sh1, w, wg1, wg2, od_b_g2[o][None, :])
            ya = yb = _gla(pqk, pvr, pla, od_head_g[o][None, :])
            cols = (0, 1)
            wo = od_w_out[o]
        x = _post(x, ya, yb, cols, wo.astype(bf16), g1, sc2, sh2, g2, norm_g[l, 1:4], l, *ffn_w)
    return x
```

```python
import functools
import math

import jax
import jax.numpy as jnp
from jax import lax
from jax.experimental import pallas as pl
from jax.experimental.pallas import tpu as pltpu

f32 = jnp.float32
bf16 = jnp.bfloat16

EPS = 1e-6
LANES = 128
HEAD_DIM = 64
A_HEADS = 4
B_HEADS = 8
DILATIONS = (1, 4, 16)
BAND = 128
C_HEADS = 4
C_DK = 128
C_DV = 256
C_GATE_RANK = 16
C_TAU = 16.0
VMEM_LIMIT = 56 * 1024 * 1024
LOG2E = math.log2(math.e)
QUERY_CHUNK = 256
ONES_ROWS = 16

NT_DIMS = (((1,), (1,)), ((), ()))
TN_DIMS = (((0,), (0,)), ((), ()))


def _rms(x, g):
    return x * lax.rsqrt(jnp.mean(x * x, axis=-1, keepdims=True) + EPS) * g


def _params(*sem):
    return pltpu.CompilerParams(dimension_semantics=sem, vmem_limit_bytes=VMEM_LIMIT)


def _resident(shape):
    nd = len(shape)
    return pl.BlockSpec(shape, lambda *_: (0,) * nd, pipeline_mode=pl.Buffered(1))


def _adaln_kernel(c_ref, w_ref, b_ref, o_ref):
    c = c_ref[...]
    a = (c * jax.nn.sigmoid(c)).astype(bf16)
    o_ref[0] = jnp.dot(a, w_ref[0].astype(bf16), preferred_element_type=f32) + b_ref[0]


def _adaln(c, ada_w, ada_b):
    depth, d, n = ada_w.shape
    bsz = c.shape[0]
    rows = 8
    cp = jnp.zeros((rows, d), f32).at[:bsz].set(c)
    tn = 3072
    out = pl.pallas_call(
        _adaln_kernel,
        grid=(depth, n // tn),
        in_specs=[
            pl.BlockSpec((rows, d), lambda l, j: (0, 0)),
            pl.BlockSpec((1, d, tn), lambda l, j: (l, 0, j)),
            pl.BlockSpec((1, 1, tn), lambda l, j: (l, 0, j)),
        ],
        out_specs=pl.BlockSpec((1, rows, tn), lambda l, j: (l, 0, j)),
        out_shape=jax.ShapeDtypeStruct((depth, rows, n), f32),
        compiler_params=_params("parallel", "parallel"),
        name="adaln",
    )(cp, ada_w, ada_b.reshape(depth, 1, n))
    return out[:, :bsz].reshape(depth, bsz, 6, 1, d)


def _modulated(x_ref, g_ref, sc_ref, sh_ref):
    h = _rms(x_ref[0], g_ref[...])
    return (h * (1.0 + sc_ref[0]) + sh_ref[0]).astype(bf16)


def _inproj_even_kernel(x_ref, g_ref, sc_ref, sh_ref, w_ref, oa_ref, ob_ref, *, nc):
    hb = _modulated(x_ref, g_ref, sc_ref, sh_ref)
    na = oa_ref.shape[-1]
    for c in range(0, na, nc):
        oa_ref[0, :, c:c + nc] = jnp.dot(hb, w_ref[:, c:c + nc], preferred_element_type=f32).astype(bf16)
    for c in range(0, ob_ref.shape[-1], nc):
        ob_ref[0, :, c:c + nc] = jnp.dot(hb, w_ref[:, na + c:na + c + nc], preferred_element_type=f32)


def _inproj_even(x, g, sc, sh, w, tm=1024):
    bsz, s, d = x.shape
    na = 3 * A_HEADS * 2 * HEAD_DIM
    nb = w.shape[1] - na
    row = lambda b, i: (b, i, 0)
    per_b = lambda b, i: (b, 0, 0)
    return pl.pallas_call(
        functools.partial(_inproj_even_kernel, nc=512),
        grid=(bsz, s // tm),
        in_specs=[
            pl.BlockSpec((1, tm, d), row),
            pl.BlockSpec((1, d), lambda b, i: (0, 0)),
            pl.BlockSpec((1, 1, d), per_b),
            pl.BlockSpec((1, 1, d), per_b),
            _resident(w.shape),
        ],
        out_specs=[pl.BlockSpec((1, tm, na), row), pl.BlockSpec((1, tm, nb), row)],
        out_shape=[jax.ShapeDtypeStruct((bsz, s, na), bf16), jax.ShapeDtypeStruct((bsz, s, nb), f32)],
        compiler_params=_params("parallel", "parallel"),
        name="inproj_even",
    )(x, g, sc, sh, w)


def _inproj_odd_kernel(x_ref, g_ref, sc_ref, sh_ref, w_ref, wg1_ref, wg2_ref, bg2_ref,
                       oqk_ref, ovr_ref, ola_ref, *, nc):
    hb = _modulated(x_ref, g_ref, sc_ref, sh_ref)
    glr = jnp.dot(hb, wg1_ref[...], preferred_element_type=f32)
    z = jnp.dot(glr.astype(bf16), wg2_ref[...], preferred_element_type=f32) + bg2_ref[...]
    ola_ref[0] = (jnp.minimum(z, 0.0) - jnp.log(1.0 + jnp.exp(-jnp.abs(z)))) * (1.0 / C_TAU)
    nqk = oqk_ref.shape[-1]
    for c in range(0, nqk, nc):
        oqk_ref[0, :, c:c + nc] = jnp.dot(hb, w_ref[:, c:c + nc], preferred_element_type=f32).astype(bf16)
    for c in range(0, ovr_ref.shape[-1], nc):
        ovr_ref[0, :, c:c + nc] = jnp.dot(hb, w_ref[:, nqk + c:nqk + c + nc], preferred_element_type=f32).astype(bf16)


def _inproj_odd(x, g, sc, sh, w, wg1, wg2, bg2, tm=1024):
    bsz, s, d = x.shape
    nqk = 2 * C_HEADS * C_DK
    nvr = 2 * C_HEADS * C_DV
    nla = C_HEADS * C_DK
    row = lambda b, i: (b, i, 0)
    per_b = lambda b, i: (b, 0, 0)
    return pl.pallas_call(
        functools.partial(_inproj_odd_kernel, nc=512),
        grid=(bsz, s // tm),
        in_specs=[
            pl.BlockSpec((1, tm, d), row),
            pl.BlockSpec((1, d), lambda b, i: (0, 0)),
            pl.BlockSpec((1, 1, d), per_b),
            pl.BlockSpec((1, 1, d), per_b),
            _resident((d, nqk + nvr)),
            _resident(wg1.shape),
            _resident(wg2.shape),
            _resident(bg2.shape),
        ],
        out_specs=[pl.BlockSpec((1, tm, nqk), row), pl.BlockSpec((1, tm, nvr), row), pl.BlockSpec((1, tm, nla), row)],
        out_shape=[jax.ShapeDtypeStruct((bsz, s, nqk), bf16), jax.ShapeDtypeStruct((bsz, s, nvr), bf16),
                   jax.ShapeDtypeStruct((bsz, s, nla), f32)],
        compiler_params=_params("parallel", "parallel"),
        name="inproj_odd",
    )(x, g, sc, sh, w, wg1, wg2, bg2)


def _diff_attn_kernel(slope_ref, lam_ref, q_ref, k_ref, v_ref, g_ref, o_ref,
                      ks_sc, vt_sc, bias_sc, sta_sc, stb_sc, cma_sc, cmb_sc, m_sc, acc_sc, *, t, lam_init):
    h = pl.program_id(1)
    i = pl.program_id(2)
    nblk = k_ref.shape[1] // t
    width = 2 * HEAD_DIM
    slope2 = slope_ref[h] * LOG2E

    @pl.when(i == 0)
    def _():
        lane = lax.broadcasted_iota(jnp.int32, (t, LANES), 1)
        first_half = lane < HEAD_DIM
        ones_row = jnp.where(lax.broadcasted_iota(jnp.int32, (ONES_ROWS, t), 0) == 0, 1.0, 0.0).astype(bf16)
        for c in range(nblk):
            kb = k_ref[0, c * t:(c + 1) * t, :]
            zero = jnp.zeros_like(kb)
            ks_sc[c, 0:t, :] = jnp.where(first_half, kb, zero)
            ks_sc[c, t:2 * t, :] = jnp.where(first_half, zero, kb)
            vt_sc[c, 0:width, :] = v_ref[0, c * t:(c + 1) * t, :].astype(f32).T.astype(bf16)
            vt_sc[c, width:width + ONES_ROWS, :] = ones_row
        kr = lax.broadcasted_iota(jnp.int32, (t, 2 * t), 0)
        qc = lax.broadcasted_iota(jnp.int32, (t, 2 * t), 1) % t
        bias_sc[...] = slope2 * (qc - kr).astype(f32)

    qt = (q_ref[0].astype(f32) * (HEAD_DIM ** -0.5 * LOG2E)).T.astype(bf16)
    lane = lax.broadcasted_iota(jnp.int32, (1, 2 * t), 1)
    one_block_further = jnp.where(lane >= t, slope2 * t, 0.0)

    def scores_into(st_ref, cm_ref, j, q0, q1):
        st = jnp.dot(ks_sc[j], qt[:, q0:q1], preferred_element_type=f32)
        for c in range(2):
            s = st[c * t:(c + 1) * t] - bias_sc[:, q0:q1]
            st_ref[c * t:(c + 1) * t, q0:q1] = s
            cm_ref[c, :, q0:q1] = jnp.max(s, axis=0, keepdims=True)

    def softmax_pv(st_ref, cm_ref, j, q0, q1, diagonal_from):
        vt = vt_sc[j]
        shift = slope2 * ((2 * i - j) * t).astype(f32) + one_block_further[:, q0:q1]
        for c in range(2):
            s = st_ref[c * t:(c + 1) * t, q0:q1]
            if diagonal_from is not None:
                kr = lax.broadcasted_iota(jnp.int32, s.shape, 0)
                qc = lax.broadcasted_iota(jnp.int32, s.shape, 1) + (q0 - diagonal_from)
                s = jnp.where(jnp.logical_or(qc >= t, kr <= qc), s, -jnp.inf)
                cmax = jnp.max(s, axis=0, keepdims=True)
            else:
                cmax = cm_ref[c, :, q0:q1]
            m_old = m_sc[c, :, q0:q1]
            m_new = jnp.maximum(m_old, cmax - shift)
            p = jnp.exp2(s - (m_new + shift)).astype(bf16)
            pv = jnp.dot(vt, p, preferred_element_type=f32)
            acc_sc[c, :, q0:q1] = jnp.exp2(m_old - m_new) * acc_sc[c, :, q0:q1] + pv
            m_sc[c, :, q0:q1] = m_new

    def overlapped(nxt, cur):
        for q0 in range(0, 2 * t, QUERY_CHUNK):
            if nxt is not None and q0 >= nxt[3]:
                scores_into(nxt[0], nxt[1], nxt[2], q0, q0 + QUERY_CHUNK)
            if cur is not None and q0 >= cur[3]:
                softmax_pv(cur[0], cur[1], cur[2], q0, q0 + QUERY_CHUNK, cur[4])

    m_sc[...] = jnp.full(m_sc.shape, -jnp.inf, f32)
    acc_sc[...] = jnp.zeros(acc_sc.shape, f32)
    overlapped((sta_sc, cma_sc, 0, 0), None)

    def body(jj, carry):
        j = 2 * jj
        overlapped((stb_sc, cmb_sc, j + 1, 0), (sta_sc, cma_sc, j, 0, None))
        overlapped((sta_sc, cma_sc, j + 2, 0), (stb_sc, cmb_sc, j + 1, 0, None))
        return carry

    lax.fori_loop(0, i, body, 0)

    overlapped((stb_sc, cmb_sc, 2 * i + 1, t), (sta_sc, cma_sc, 2 * i, 0, 0))
    overlapped(None, (stb_sc, cmb_sc, 2 * i + 1, t, t))

    lp = lam_ref[...]
    lam = (jnp.exp(jnp.sum(lp[0:1] * lp[1:2], axis=-1, keepdims=True))
           - jnp.exp(jnp.sum(lp[2:3] * lp[3:4], axis=-1, keepdims=True)) + lam_init)
    outs = [acc_sc[c, 0:width, :] / acc_sc[c, width:width + 1, :] for c in range(2)]
    ot = outs[0] - lam * outs[1]
    ot = ot * lax.rsqrt(jnp.mean(ot * ot, axis=0, keepdims=True) + EPS) * (g_ref[...] * (1.0 - lam_init))
    o_ref[0] = ot.T.astype(bf16)


def _diff_attn(pa, slopes, lam_p, subln_g, lam_init, t=512):
    bsz, s, _ = pa.shape
    width = 2 * HEAD_DIM
    nblk = s // t
    return pl.pallas_call(
        functools.partial(_diff_attn_kernel, t=t, lam_init=lam_init),
        grid=(bsz, A_HEADS, nblk // 2),
        in_specs=[
            pl.BlockSpec(memory_space=pltpu.SMEM),
            pl.BlockSpec(lam_p.shape, lambda b, h, i: (0, 0)),
            pl.BlockSpec((1, 2 * t, width), lambda b, h, i: (b, i, h)),
            pl.BlockSpec((1, s, width), lambda b, h, i: (b, 0, A_HEADS + h)),
            pl.BlockSpec((1, s, width), lambda b, h, i: (b, 0, 2 * A_HEADS + h)),
            pl.BlockSpec((width, 1), lambda b, h, i: (0, 0)),
        ],
        out_specs=pl.BlockSpec((1, 2 * t, width), lambda b, h, i: (b, i, h)),
        out_shape=jax.ShapeDtypeStruct((bsz, s, A_HEADS * width), bf16),
        scratch_shapes=[pltpu.VMEM((nblk, 2 * t, width), bf16),
                        pltpu.VMEM((nblk, width + ONES_ROWS, t), bf16),
                        pltpu.VMEM((t, 2 * t), f32),
                        pltpu.VMEM((2 * t, 2 * t), f32),
                        pltpu.VMEM((2 * t, 2 * t), f32),
                        pltpu.VMEM((2, 1, 2 * t), f32),
                        pltpu.VMEM((2, 1, 2 * t), f32),
                        pltpu.VMEM((2, 1, 2 * t), f32),
                        pltpu.VMEM((2, width + ONES_ROWS, 2 * t), f32)],
        compiler_params=_params("parallel", "parallel", "arbitrary"),
        name="diff_attn",
    )(slopes, lam_p, pa, pa, pa, subln_g)


def _dilated_kernel(slope_ref, q_ref, kp_ref, kc_ref, vp_ref, vc_ref, o_ref,
                    acc_sc, m_sc, l_sc, *, span):
    hp = pl.program_id(1)
    sidx = pl.program_id(2)
    blk = BAND
    nblocks = span // blk

    lane = lax.broadcasted_iota(jnp.int32, (blk, LANES), 1)
    head_a = lane < HEAD_DIM
    a = lax.broadcasted_iota(jnp.int32, (blk, 2 * blk), 0)
    bi = lax.broadcasted_iota(jnp.int32, (blk, 2 * blk), 1)
    rel = blk + a - bi
    valid = (rel >= 0) & (rel <= BAND)
    before_start = bi < blk
    steps = jnp.where(valid, rel.astype(f32), jnp.inf)
    slopes = (slope_ref[2 * hp], slope_ref[2 * hp + 1])
    for br, d in enumerate(DILATIONS):
        per_stream = nblocks // d
        rates = [sl * (LOG2E * d) for sl in slopes]

        def one_block(r, n, br=br, d=d, rates=rates):
            q_start = n * (blk * d) + r
            qb = (q_ref[0, pl.ds(q_start, blk, stride=d), :] * (HEAD_DIM ** -0.5 * LOG2E)).astype(bf16)
            if n == 0:
                tail = pl.ds(span - blk * d + r, blk, stride=d)
                own = pl.ds(r, blk, stride=d)
                kk = jnp.concatenate([kp_ref[0, tail, :], kc_ref[0, own, :]], axis=0).astype(bf16)
                vv = jnp.concatenate([vp_ref[0, tail, :], vc_ref[0, own, :]], axis=0).astype(bf16)
            else:
                both = pl.ds((n - 1) * (blk * d) + r, 2 * blk, stride=d)
                kk = kc_ref[0, both, :].astype(bf16)
                vv = vc_ref[0, both, :].astype(bf16)
            zero = jnp.zeros_like(qb)
            qh = (jnp.where(head_a, qb, zero), jnp.where(head_a, zero, qb))
            outs, ms, ls = [], [], []
            for c in range(2):
                s = lax.dot_general(qh[c], kk, NT_DIMS, preferred_element_type=f32) - rates[c] * steps
                if n == 0:
                    s = jnp.where(jnp.logical_and(sidx == 0, before_start), -jnp.inf, s)
                m = jnp.max(s, axis=-1, keepdims=True)
                p = jnp.exp2(s - m)
                ls.append(jnp.sum(p, axis=-1, keepdims=True))
                ms.append(m)
                outs.append(jnp.dot(p.astype(bf16), vv, preferred_element_type=f32))
            dst = pl.ds(q_start, blk, stride=d)
            acc_sc[br, dst, :] = jnp.where(head_a, outs[0], outs[1])
            m_sc[br, dst, :] = jnp.where(head_a, ms[0], ms[1])
            l_sc[br, dst, :] = jnp.where(head_a, ls[0], ls[1])

        for r in range(d):
            for n in range(per_stream):
                one_block(r, n)

    m_all = jnp.maximum(jnp.maximum(m_sc[0], m_sc[1]), m_sc[2])
    num = jnp.zeros((span, LANES), f32)
    den = jnp.zeros((span, LANES), f32)
    for br in range(len(DILATIONS)):
        w = jnp.exp2(m_sc[br] - m_all)
        num = num + w * acc_sc[br]
        den = den + w * l_sc[br]
    o_ref[0] = (num / den).astype(bf16)


def _dilated_attn(pb, slopes, span=2048):
    bsz, s, _ = pb.shape
    pairs = B_HEADS // 2
    cur = lambda off: (lambda b, h, i: (b, i, off + h))
    prev = lambda off: (lambda b, h, i: (b, jnp.maximum(i - 1, 0), off + h))
    blk = (1, span, LANES)
    nbr = len(DILATIONS)
    return pl.pallas_call(
        functools.partial(_dilated_kernel, span=span),
        grid=(bsz, pairs, s // span),
        in_specs=[
            pl.BlockSpec(memory_space=pltpu.SMEM),
            pl.BlockSpec(blk, cur(0)),
            pl.BlockSpec(blk, prev(pairs)),
            pl.BlockSpec(blk, cur(pairs)),
            pl.BlockSpec(blk, prev(2 * pairs)),
            pl.BlockSpec(blk, cur(2 * pairs)),
        ],
        out_specs=pl.BlockSpec(blk, cur(0)),
        out_shape=jax.ShapeDtypeStruct((bsz, s, B_HEADS * HEAD_DIM), bf16),
        scratch_shapes=[pltpu.VMEM((nbr, span, LANES), f32), pltpu.VMEM((nbr, span, LANES), f32),
                        pltpu.VMEM((nbr, span, LANES), f32)],
        compiler_params=_params("parallel", "parallel", "parallel"),
        name="dilated_attn",
    )(slopes, pb, pb, pb, pb, pb)


def _gla_kernel(q_ref, k_ref, v_ref, r_ref, la_ref, hg_ref, o_ref, state_sc, b_sc, q_sc, *, chunk):
    tb = q_ref.shape[1]

    @pl.when(pl.program_id(2) == 0)
    def _():
        state_sc[...] = jnp.zeros_like(state_sc)

    row = lax.broadcasted_iota(jnp.int32, (chunk, chunk), 0)
    col = lax.broadcasted_iota(jnp.int32, (chunk, chunk), 1)
    causal = row >= col
    tril = jnp.where(causal, 1.0, 0.0).astype(bf16)
    steep = jnp.min(la_ref[0]) * chunk < -80.0

    def intra_exact(b, q, k):
        b_sc[...] = b
        q_sc[...] = q

        def one_row(i, at):
            bi = b_sc[pl.ds(i, 1), :]
            qi = q_sc[pl.ds(i, 1), :]
            g = jnp.exp(jnp.minimum(bi - b, 0.0)) * k * qi
            colv = jnp.sum(g, axis=-1, keepdims=True)
            return jnp.where(col == i, colv, at)

        return lax.fori_loop(0, chunk, one_row, jnp.zeros((chunk, chunk), f32)).T

    def step(exact):
        sls = [pl.ds(ci * chunk, chunk) for ci in range(tb // chunk)]
        bs = []
        for sl in sls:
            la = la_ref[0, sl, :]
            hi = la.astype(bf16)
            lo = (la - hi.astype(f32)).astype(bf16)
            bs.append(jnp.dot(tril, hi, preferred_element_type=f32)
                      + jnp.dot(tril, lo, preferred_element_type=f32))
        qs = [q_ref[0, sl, :].astype(f32) * (C_DK ** -0.5) for sl in sls]
        ks = [k_ref[0, sl, :].astype(f32) for sl in sls]
        qts = [(q * jnp.exp(b)).astype(bf16) for q, b in zip(qs, bs)]
        lasts = [b[chunk - 1:chunk, :] for b in bs]
        kds = [(k * jnp.exp(bl - b)).astype(bf16) for k, b, bl in zip(ks, bs, lasts)]
        if exact:
            ats = [intra_exact(b, q, k) for b, q, k in zip(bs, qs, ks)]
        else:
            kts = [(k * jnp.exp(-b)).astype(bf16) for k, b in zip(ks, bs)]
            ats = [lax.dot_general(qt, kt, NT_DIMS, preferred_element_type=f32) for qt, kt in zip(qts, kts)]
        updates = [lax.dot_general(v_ref[0, sl, :], kd, TN_DIMS, preferred_element_type=f32)
                   for sl, kd in zip(sls, kds)]
        intras = [jnp.dot(jnp.where(causal, a, 0.0).astype(bf16), v_ref[0, sl, :], preferred_element_type=f32)
                  for sl, a in zip(sls, ats)]
        states = [state_sc[...]]
        for bl, update in zip(lasts, updates):
            states.append(states[-1] * jnp.exp(bl) + update)
        state_sc[...] = states[-1]
        inters = [lax.dot_general(qt, st.astype(bf16), NT_DIMS, preferred_element_type=f32)
                  for qt, st in zip(qts, states[:-1])]
        for sl, o_intra, o_inter in zip(sls, intras, inters):
            rr = r_ref[0, sl, :].astype(f32)
            o_ref[0, sl, :] = (_rms(o_intra + o_inter, hg_ref[...]) * (rr * jax.nn.sigmoid(rr))).astype(bf16)

    @pl.when(jnp.logical_not(steep))
    def _():
        step(False)

    @pl.when(steep)
    def _():
        step(True)


def _gla(pqk, pvr, pla, head_g, tb=2048, chunk=128):
    bsz, s, _ = pqk.shape
    return pl.pallas_call(
        functools.partial(_gla_kernel, chunk=chunk),
        grid=(bsz, C_HEADS, s // tb),
        in_specs=[
            pl.BlockSpec((1, tb, C_DK), lambda b, h, i: (b, i, h)),
            pl.BlockSpec((1, tb, C_DK), lambda b, h, i: (b, i, C_HEADS + h)),
            pl.BlockSpec((1, tb, C_DV), lambda b, h, i: (b, i, h)),
            pl.BlockSpec((1, tb, C_DV), lambda b, h, i: (b, i, C_HEADS + h)),
            pl.BlockSpec((1, tb, C_DK), lambda b, h, i: (b, i, h)),
            pl.BlockSpec((1, C_DV), lambda b, h, i: (0, 0)),
        ],
        out_specs=pl.BlockSpec((1, tb, C_DV), lambda b, h, i: (b, i, h)),
        out_shape=jax.ShapeDtypeStruct((bsz, s, C_HEADS * C_DV), bf16),
        scratch_shapes=[pltpu.VMEM((C_DV, C_DK), f32),
                        pltpu.VMEM((chunk, C_DK), f32), pltpu.VMEM((chunk, C_DK), f32)],
        compiler_params=_params("parallel", "parallel", "arbitrary"),
        name="gla",
    )(pqk, pqk, pvr, pvr, pla, head_g)


def _post_kernel(x_ref, ya_ref, yb_ref, wo_ref, gate1_ref, sc2_ref, sh2_ref, gate2_ref, ng_ref,
                 wg_ref, wu_ref, wd_ref, o_ref, x1_sc, hb_sc, *, hc, groups, piece):
    half = ya_ref.shape[-1]
    hidden = wg_ref.shape[1]
    chunks = [(c, min(c + hc, hidden)) for c in range(0, hidden, hc)]
    starts = [sum(groups[:g]) for g in range(len(groups))]

    def pieces(g):
        return range(starts[g], starts[g] + groups[g], piece)

    def out_proj(g):
        rows = pl.ds(starts[g], groups[g])
        x1_sc[rows, :] = (jnp.dot(ya_ref[0, rows, :], wo_ref[0:half, :], preferred_element_type=f32)
                          + jnp.dot(yb_ref[0, rows, :], wo_ref[half:2 * half, :], preferred_element_type=f32))

    def before_ffn(start):
        rows = pl.ds(start, piece)
        x1 = x_ref[0, rows, :] + (1.0 + gate1_ref[0]) * _rms(x1_sc[rows, :], ng_ref[0:1, :])
        x1_sc[rows, :] = x1
        hb_sc[rows, :] = (_rms(x1, ng_ref[1:2, :]) * (1.0 + sc2_ref[0]) + sh2_ref[0]).astype(bf16)

    def after_ffn(start, y2):
        rows = pl.ds(start, piece)
        o_ref[0, rows, :] = x1_sc[rows, :] + (1.0 + gate2_ref[0]) * _rms(y2, ng_ref[2:3, :])

    out_proj(0)
    if len(groups) > 1:
        out_proj(1)
    for p in pieces(0):
        before_ffn(p)
    done = None
    for g in range(len(groups)):
        side = []
        if g + 2 < len(groups):
            side.append(functools.partial(out_proj, g + 2))
        if g + 1 < len(groups):
            side += [functools.partial(before_ffn, p) for p in pieces(g + 1)]
        if done is not None:
            side += [functools.partial(after_ffn, p, done[p - starts[g - 1]:p - starts[g - 1] + piece])
                     for p in pieces(g - 1)]
        hb = hb_sc[pl.ds(starts[g], groups[g]), :]
        acc = jnp.zeros((groups[g], x1_sc.shape[1]), f32)
        for ci, (c, e) in enumerate(chunks):
            gt = jnp.dot(hb, wg_ref[:, c:e], preferred_element_type=f32)
            up = jnp.dot(hb, wu_ref[:, c:e], preferred_element_type=f32)
            act = (gt * jax.nn.sigmoid(gt) * up).astype(bf16)
            acc = acc + jnp.dot(act, wd_ref[c:e, :], preferred_element_type=f32)
            for task in side[ci * len(side) // len(chunks):(ci + 1) * len(side) // len(chunks)]:
                task()
        done = acc
    last = len(groups) - 1
    for p in pieces(last):
        after_ffn(p, done[p - starts[last]:p - starts[last] + piece])


def _post(x, ya, yb, cols, wo, gate1, sc2, sh2, gate2, ng, layer, wg, wu, wd, tm=1024, hc=512):
    bsz, s, d = x.shape
    half = wo.shape[0] // 2
    hidden = wg.shape[-1]
    row = lambda b, i: (b, i, 0)
    per_b = lambda b, i: (b, 0, 0)
    mod = pl.BlockSpec((1, 1, d), per_b)
    slab = lambda shape: pl.BlockSpec((None,) + shape, lambda b, i: (layer, 0, 0), pipeline_mode=pl.Buffered(1))
    return pl.pallas_call(
        functools.partial(_post_kernel, hc=hc, groups=(tm // 2, tm // 2), piece=128),
        grid=(bsz, s // tm),
        in_specs=[
            pl.BlockSpec((1, tm, d), row),
            pl.BlockSpec((1, tm, half), lambda b, i: (b, i, cols[0])),
            pl.BlockSpec((1, tm, half), lambda b, i: (b, i, cols[1])),
            _resident(wo.shape),
            mod, mod, mod, mod,
            _resident(ng.shape),
            slab((d, hidden)), slab((d, hidden)), slab((hidden, d)),
        ],
        out_specs=pl.BlockSpec((1, tm, d), row),
        out_shape=jax.ShapeDtypeStruct((bsz, s, d), f32),
        scratch_shapes=[pltpu.VMEM((tm, d), f32), pltpu.VMEM((tm, d), bf16)],
        compiler_params=_params("parallel", "parallel"),
        name="post",
    )(x, ya, yb, wo, gate1, sc2, sh2, gate2, ng, wg, wu, wd)


def _alibi_slopes(n):
    return 2.0 ** (-8.0 * jnp.arange(1, n + 1, dtype=f32) / n)


def kernel(x, c, ada_w, ada_b, norm_g, ev_w_in, ev_lambda, ev_subln_g, ev_w_out, od_w_in, od_w_g2,
           od_b_g2, od_head_g, od_w_out, ffn_w_gate, ffn_w_up, ffn_w_down):
    depth = ada_w.shape[0]
    mod = _adaln(c, ada_w, ada_b)
    ffn_w = (ffn_w_gate.astype(bf16), ffn_w_up.astype(bf16), ffn_w_down.astype(bf16))
    for l in range(depth):
        sh1, sc1, g1, sh2, sc2, g2 = (mod[l, :, j] for j in range(6))
        pre_g = norm_g[l, 0:1]
        if l % 2 == 0:
            e = l // 2
            lam_init = 0.8 - 0.6 * math.exp(-0.3 * l)
            pa, pb = _inproj_even(x, pre_g, sc1, sh1, ev_w_in[e].astype(bf16))
            ya = _diff_attn(pa, _alibi_slopes(A_HEADS), ev_lambda[e], ev_subln_g[e][:, None], lam_init)
            yb = _dilated_attn(pb, _alibi_slopes(B_HEADS))
            cols = (0, 0)
            wo = ev_w_out[e]
        else:
            o = l // 2
            w = od_w_in[o].astype(bf16)
            nmain = w.shape[1] - C_GATE_RANK
            wg1 = jnp.zeros((w.shape[0], LANES), bf16).at[:, :C_GATE_RANK].set(w[:, nmain:])
            wg2 = jnp.zeros((LANES, od_w_g2.shape[-1]), bf16).at[:C_GATE_RANK].set(od_w_g2[o].astype(bf16))
            pqk, pvr, pla = _inproj_odd(x, pre_g, sc1, sh1, w, wg1, wg2, od_b_g2[o][None, :])
            ya = yb = _gla(pqk, pvr, pla, od_head_g[o][None, :])
            cols = (0, 1)
            wo = od_w_out[o]
        x = _post(x, ya, yb, cols, wo.astype(bf16), g1, sc2, sh2, g2, norm_g[l, 1:4], l, *ffn_w)
    return x
```

```python
import functools
import math

import jax
import jax.numpy as jnp
from jax import lax
from jax.experimental import pallas as pl
from jax.experimental.pallas import tpu as pltpu

f32 = jnp.float32
bf16 = jnp.bfloat16

EPS = 1e-6
LANES = 128
HEAD_DIM = 64
A_HEADS = 4
B_HEADS = 8
DILATIONS = (1, 4, 16)
BAND = 128
C_HEADS = 4
C_DK = 128
C_DV = 256
C_GATE_RANK = 16
C_TAU = 16.0
VMEM_LIMIT = 56 * 1024 * 1024
LOG2E = math.log2(math.e)
QUERY_CHUNK = 256
ONES_ROWS = 16

NT_DIMS = (((1,), (1,)), ((), ()))
TN_DIMS = (((0,), (0,)), ((), ()))


def _rms(x, g):
    return x * lax.rsqrt(jnp.mean(x * x, axis=-1, keepdims=True) + EPS) * g


def _params(*sem):
    return pltpu.CompilerParams(dimension_semantics=sem, vmem_limit_bytes=VMEM_LIMIT)


def _resident(shape):
    nd = len(shape)
    return pl.BlockSpec(shape, lambda *_: (0,) * nd, pipeline_mode=pl.Buffered(1))


def _adaln_kernel(c_ref, w_ref, b_ref, o_ref):
    c = c_ref[...]
    a = (c * jax.nn.sigmoid(c)).astype(bf16)
    o_ref[0] = jnp.dot(a, w_ref[0].astype(bf16), preferred_element_type=f32) + b_ref[0]


def _adaln(c, ada_w, ada_b):
    depth, d, n = ada_w.shape
    bsz = c.shape[0]
    rows = 8
    cp = jnp.zeros((rows, d), f32).at[:bsz].set(c)
    tn = 3072
    out = pl.pallas_call(
        _adaln_kernel,
        grid=(depth, n // tn),
        in_specs=[
            pl.BlockSpec((rows, d), lambda l, j: (0, 0)),
            pl.BlockSpec((1, d, tn), lambda l, j: (l, 0, j)),
            pl.BlockSpec((1, 1, tn), lambda l, j: (l, 0, j)),
        ],
        out_specs=pl.BlockSpec((1, rows, tn), lambda l, j: (l, 0, j)),
        out_shape=jax.ShapeDtypeStruct((depth, rows, n), f32),
        compiler_params=_params("parallel", "parallel"),
        name="adaln",
    )(cp, ada_w, ada_b.reshape(depth, 1, n))
    return out[:, :bsz].reshape(depth, bsz, 6, 1, d)


def _modulated(x_ref, g_ref, sc_ref, sh_ref):
    h = _rms(x_ref[0], g_ref[...])
    return (h * (1.0 + sc_ref[0]) + sh_ref[0]).astype(bf16)


def _inproj_even_kernel(x_ref, g_ref, sc_ref, sh_ref, w_ref, oa_ref, ob_ref, *, nc):
    hb = _modulated(x_ref, g_ref, sc_ref, sh_ref)
    na = oa_ref.shape[-1]
    for c in range(0, na, nc):
        oa_ref[0, :, c:c + nc] = jnp.dot(hb, w_ref[:, c:c + nc], preferred_element_type=f32).astype(bf16)
    for c in range(0, ob_ref.shape[-1], nc):
        ob_ref[0, :, c:c + nc] = jnp.dot(hb, w_ref[:, na + c:na + c + nc], preferred_element_type=f32)


def _inproj_even(x, g, sc, sh, w, tm=1024):
    bsz, s, d = x.shape
    na = 3 * A_HEADS * 2 * HEAD_DIM
    nb = w.shape[1] - na
    row = lambda b, i: (b, i, 0)
    per_b = lambda b, i: (b, 0, 0)
    return pl.pallas_call(
        functools.partial(_inproj_even_kernel, nc=512),
        grid=(bsz, s // tm),
        in_specs=[
            pl.BlockSpec((1, tm, d), row),
            pl.BlockSpec((1, d), lambda b, i: (0, 0)),
            pl.BlockSpec((1, 1, d), per_b),
            pl.BlockSpec((1, 1, d), per_b),
            _resident(w.shape),
        ],
        out_specs=[pl.BlockSpec((1, tm, na), row), pl.BlockSpec((1, tm, nb), row)],
        out_shape=[jax.ShapeDtypeStruct((bsz, s, na), bf16), jax.ShapeDtypeStruct((bsz, s, nb), f32)],
        compiler_params=_params("parallel", "parallel"),
        name="inproj_even",
    )(x, g, sc, sh, w)


def _inproj_odd_kernel(x_ref, g_ref, sc_ref, sh_ref, w_ref, wg1_ref, wg2_ref, bg2_ref,
                       oqk_ref, ovr_ref, ola_ref, *, nc):
    hb = _modulated(x_ref, g_ref, sc_ref, sh_ref)
    glr = jnp.dot(hb, wg1_ref[...], preferred_element_type=f32)
    z = jnp.dot(glr.astype(bf16), wg2_ref[...], preferred_element_type=f32) + bg2_ref[...]
    ola_ref[0] = (jnp.minimum(z, 0.0) - jnp.log(1.0 + jnp.exp(-jnp.abs(z)))) * (1.0 / C_TAU)
    nqk = oqk_ref.shape[-1]
    for c in range(0, nqk, nc):
        oqk_ref[0, :, c:c + nc] = jnp.dot(hb, w_ref[:, c:c + nc], preferred_element_type=f32).astype(bf16)
    for c in range(0, ovr_ref.shape[-1], nc):
        ovr_ref[0, :, c:c + nc] = jnp.dot(hb, w_ref[:, nqk + c:nqk + c + nc], preferred_element_type=f32).astype(bf16)


def _inproj_odd(x, g, sc, sh, w, wg1, wg2, bg2, tm=1024):
    bsz, s, d = x.shape
    nqk = 2 * C_HEADS * C_DK
    nvr = 2 * C_HEADS * C_DV
    nla = C_HEADS * C_DK
    row = lambda b, i: (b, i, 0)
    per_b = lambda b, i: (b, 0, 0)
    return pl.pallas_call(
        functools.partial(_inproj_odd_kernel, nc=512),
        grid=(bsz, s // tm),
        in_specs=[
            pl.BlockSpec((1, tm, d), row),
            pl.BlockSpec((1, d), lambda b, i: (0, 0)),
            pl.BlockSpec((1, 1, d), per_b),
            pl.BlockSpec((1, 1, d), per_b),
            _resident((d, nqk + nvr)),
            _resident(wg1.shape),
            _resident(wg2.shape),
            _resident(bg2.shape),
        ],
        out_specs=[pl.BlockSpec((1, tm, nqk), row), pl.BlockSpec((1, tm, nvr), row), pl.BlockSpec((1, tm, nla), row)],
        out_shape=[jax.ShapeDtypeStruct((bsz, s, nqk), bf16), jax.ShapeDtypeStruct((bsz, s, nvr), bf16),
                   jax.ShapeDtypeStruct((bsz, s, nla), f32)],
        compiler_params=_params("parallel", "parallel"),
        name="inproj_odd",
    )(x, g, sc, sh, w, wg1, wg2, bg2)


def _diff_attn_kernel(slope_ref, lam_ref, q_ref, qn_ref, k_ref, v_ref, g_ref, o_ref,
                      ks_sc, vt_sc, bias_sc, sta_sc, stb_sc, cma_sc, cmb_sc, m_sc, acc_sc, *, t, lam_init):
    h = pl.program_id(1)
    i = pl.program_id(2)
    nblk = k_ref.shape[1] // t
    width = 2 * HEAD_DIM
    slope2 = slope_ref[h] * LOG2E

    @pl.when(i == 0)
    def _():
        lane = lax.broadcasted_iota(jnp.int32, (t, LANES), 1)
        first_half = lane < HEAD_DIM
        ones_row = jnp.where(lax.broadcasted_iota(jnp.int32, (ONES_ROWS, t), 0) == 0, 1.0, 0.0).astype(bf16)
        for c in range(nblk):
            kb = k_ref[0, c * t:(c + 1) * t, :]
            zero = jnp.zeros_like(kb)
            ks_sc[c, 0:t, :] = jnp.where(first_half, kb, zero)
            ks_sc[c, t:2 * t, :] = jnp.where(first_half, zero, kb)
            vt_sc[c, 0:width, :] = v_ref[0, c * t:(c + 1) * t, :].astype(f32).T.astype(bf16)
            vt_sc[c, width:width + ONES_ROWS, :] = ones_row
        kr = lax.broadcasted_iota(jnp.int32, (t, 2 * t), 0)
        qc = lax.broadcasted_iota(jnp.int32, (t, 2 * t), 1) % t
        bias_sc[...] = slope2 * (qc - kr).astype(f32)

    def transposed_queries(ref):
        return (ref[0].astype(f32) * (HEAD_DIM ** -0.5 * LOG2E)).T.astype(bf16)

    qt = transposed_queries(q_ref)
    lane = lax.broadcasted_iota(jnp.int32, (1, 2 * t), 1)
    one_block_further = jnp.where(lane >= t, slope2 * t, 0.0)

    def scores_into(st_ref, cm_ref, j, q0, q1, qt=qt):
        st = jnp.dot(ks_sc[j], qt[:, q0:q1], preferred_element_type=f32)
        for c in range(2):
            s = st[c * t:(c + 1) * t] - bias_sc[:, q0:q1]
            st_ref[c * t:(c + 1) * t, q0:q1] = s
            cm_ref[c, :, q0:q1] = jnp.max(s, axis=0, keepdims=True)

    def softmax_pv(st_ref, cm_ref, j, q0, q1, diagonal_from):
        vt = vt_sc[j]
        shift = slope2 * ((2 * i - j) * t).astype(f32) + one_block_further[:, q0:q1]
        for c in range(2):
            s = st_ref[c * t:(c + 1) * t, q0:q1]
            if diagonal_from is not None:
                kr = lax.broadcasted_iota(jnp.int32, s.shape, 0)
                qc = lax.broadcasted_iota(jnp.int32, s.shape, 1) + (q0 - diagonal_from)
                s = jnp.where(jnp.logical_or(qc >= t, kr <= qc), s, -jnp.inf)
                cmax = jnp.max(s, axis=0, keepdims=True)
            else:
                cmax = cm_ref[c, :, q0:q1]
            m_old = m_sc[c, :, q0:q1]
            m_new = jnp.maximum(m_old, cmax - shift)
            p = jnp.exp2(s - (m_new + shift)).astype(bf16)
            pv = jnp.dot(vt, p, preferred_element_type=f32)
            acc_sc[c, :, q0:q1] = jnp.exp2(m_old - m_new) * acc_sc[c, :, q0:q1] + pv
            m_sc[c, :, q0:q1] = m_new

    def overlapped(nxt, cur):
        for q0 in range(0, 2 * t, QUERY_CHUNK):
            if nxt is not None and q0 >= nxt[3]:
                scores_into(nxt[0], nxt[1], nxt[2], q0, q0 + QUERY_CHUNK, *nxt[4:])
            if cur is not None and q0 >= cur[3]:
                softmax_pv(cur[0], cur[1], cur[2], q0, q0 + QUERY_CHUNK, cur[4])

    m_sc[...] = jnp.full(m_sc.shape, -jnp.inf, f32)
    acc_sc[...] = jnp.zeros(acc_sc.shape, f32)

    @pl.when(i == 0)
    def _():
        overlapped((sta_sc, cma_sc, 0, 0), None)

    def body(jj, carry):
        j = 2 * jj
        overlapped((stb_sc, cmb_sc, j + 1, 0), (sta_sc, cma_sc, j, 0, None))
        overlapped((sta_sc, cma_sc, j + 2, 0), (stb_sc, cmb_sc, j + 1, 0, None))
        return carry

    lax.fori_loop(0, i, body, 0)

    overlapped((stb_sc, cmb_sc, 2 * i + 1, t), (sta_sc, cma_sc, 2 * i, 0, 0))
    overlapped((sta_sc, cma_sc, 0, 0, transposed_queries(qn_ref)), (stb_sc, cmb_sc, 2 * i + 1, t, t))

    lp = lam_ref[...]
    lam = (jnp.exp(jnp.sum(lp[0:1] * lp[1:2], axis=-1, keepdims=True))
           - jnp.exp(jnp.sum(lp[2:3] * lp[3:4], axis=-1, keepdims=True)) + lam_init)
    outs = [acc_sc[c, 0:width, :] / acc_sc[c, width:width + 1, :] for c in range(2)]
    ot = outs[0] - lam * outs[1]
    ot = ot * lax.rsqrt(jnp.mean(ot * ot, axis=0, keepdims=True) + EPS) * (g_ref[...] * (1.0 - lam_init))
    o_ref[0] = ot.T.astype(bf16)


def _diff_attn(pa, slopes, lam_p, subln_g, lam_init, t=512):
    bsz, s, _ = pa.shape
    width = 2 * HEAD_DIM
    nblk = s // t
    return pl.pallas_call(
        functools.partial(_diff_attn_kernel, t=t, lam_init=lam_init),
        grid=(bsz, A_HEADS, nblk // 2),
        in_specs=[
            pl.BlockSpec(memory_space=pltpu.SMEM),
            pl.BlockSpec(lam_p.shape, lambda b, h, i: (0, 0)),
            pl.BlockSpec((1, 2 * t, width), lambda b, h, i: (b, i, h)),
            pl.BlockSpec((1, 2 * t, width), lambda b, h, i: (b, jnp.minimum(i + 1, nblk // 2 - 1), h)),
            pl.BlockSpec((1, s, width), lambda b, h, i: (b, 0, A_HEADS + h)),
            pl.BlockSpec((1, s, width), lambda b, h, i: (b, 0, 2 * A_HEADS + h)),
            pl.BlockSpec((width, 1), lambda b, h, i: (0, 0)),
        ],
        out_specs=pl.BlockSpec((1, 2 * t, width), lambda b, h, i: (b, i, h)),
        out_shape=jax.ShapeDtypeStruct((bsz, s, A_HEADS * width), bf16),
        scratch_shapes=[pltpu.VMEM((nblk, 2 * t, width), bf16),
                        pltpu.VMEM((nblk, width + ONES_ROWS, t), bf16),
                        pltpu.VMEM((t, 2 * t), f32),
                        pltpu.VMEM((2 * t, 2 * t), f32),
                        pltpu.VMEM((2 * t, 2 * t), f32),
                        pltpu.VMEM((2, 1, 2 * t), f32),
                        pltpu.VMEM((2, 1, 2 * t), f32),
                        pltpu.VMEM((2, 1, 2 * t), f32),
                        pltpu.VMEM((2, width + ONES_ROWS, 2 * t), f32)],
        compiler_params=_params("parallel", "parallel", "arbitrary"),
        name="diff_attn",
    )(slopes, lam_p, pa, pa, pa, pa, subln_g)


def _dilated_kernel(slope_ref, q_ref, kp_ref, kc_ref, vp_ref, vc_ref, o_ref,
                    acc_sc, m_sc, l_sc, *, span):
    hp = pl.program_id(1)
    sidx = pl.program_id(2)
    blk = BAND
    nblocks = span // blk

    lane = lax.broadcasted_iota(jnp.int32, (blk, LANES), 1)
    head_a = lane < HEAD_DIM
    a = lax.broadcasted_iota(jnp.int32, (blk, 2 * blk), 0)
    bi = lax.broadcasted_iota(jnp.int32, (blk, 2 * blk), 1)
    rel = blk + a - bi
    valid = (rel >= 0) & (rel <= BAND)
    before_start = bi < blk
    steps = jnp.where(valid, rel.astype(f32), jnp.inf)
    slopes = (slope_ref[2 * hp], slope_ref[2 * hp + 1])
    for br, d in enumerate(DILATIONS):
        per_stream = nblocks // d
        rates = [sl * (LOG2E * d) for sl in slopes]

        def one_block(r, n, br=br, d=d, rates=rates):
            q_start = n * (blk * d) + r
            qb = (q_ref[0, pl.ds(q_start, blk, stride=d), :] * (HEAD_DIM ** -0.5 * LOG2E)).astype(bf16)
            if n == 0:
                tail = pl.ds(span - blk * d + r, blk, stride=d)
                own = pl.ds(r, blk, stride=d)
                kk = jnp.concatenate([kp_ref[0, tail, :], kc_ref[0, own, :]], axis=0).astype(bf16)
                vv = jnp.concatenate([vp_ref[0, tail, :], vc_ref[0, own, :]], axis=0).astype(bf16)
            else:
                both = pl.ds((n - 1) * (blk * d) + r, 2 * blk, stride=d)
                kk = kc_ref[0, both, :].astype(bf16)
                vv = vc_ref[0, both, :].astype(bf16)
            zero = jnp.zeros_like(qb)
            qh = (jnp.where(head_a, qb, zero), jnp.where(head_a, zero, qb))
            outs, ms, ls = [], [], []
            for c in range(2):
                s = lax.dot_general(qh[c], kk, NT_DIMS, preferred_element_type=f32) - rates[c] * steps
                if n == 0:
                    s = jnp.where(jnp.logical_and(sidx == 0, before_start), -jnp.inf, s)
                m = jnp.max(s, axis=-1, keepdims=True)
                p = jnp.exp2(s - m)
                ls.append(jnp.sum(p, axis=-1, keepdims=True))
                ms.append(m)
                outs.append(jnp.dot(p.astype(bf16), vv, preferred_element_type=f32))
            dst = pl.ds(q_start, blk, stride=d)
            acc_sc[br, dst, :] = jnp.where(head_a, outs[0], outs[1])
            m_sc[br, dst, :] = jnp.where(head_a, ms[0], ms[1])
            l_sc[br, dst, :] = jnp.where(head_a, ls[0], ls[1])

        for r in range(d):
            for n in range(per_stream):
                one_block(r, n)

    m_all = jnp.maximum(jnp.maximum(m_sc[0], m_sc[1]), m_sc[2])
    num = jnp.zeros((span, LANES), f32)
    den = jnp.zeros((span, LANES), f32)
    for br in range(len(DILATIONS)):
        w = jnp.exp2(m_sc[br] - m_all)
        num = num + w * acc_sc[br]
        den = den + w * l_sc[br]
    o_ref[0] = (num / den).astype(bf16)


def _dilated_attn(pb, slopes, span=2048):
    bsz, s, _ = pb.shape
    pairs = B_HEADS // 2
    cur = lambda off: (lambda b, h, i: (b, i, off + h))
    prev = lambda off: (lambda b, h, i: (b, jnp.maximum(i - 1, 0), off + h))
    blk = (1, span, LANES)
    nbr = len(DILATIONS)
    return pl.pallas_call(
        functools.partial(_dilated_kernel, span=span),
        grid=(bsz, pairs, s // span),
        in_specs=[
            pl.BlockSpec(memory_space=pltpu.SMEM),
            pl.BlockSpec(blk, cur(0)),
            pl.BlockSpec(blk, prev(pairs)),
            pl.BlockSpec(blk, cur(pairs)),
            pl.BlockSpec(blk, prev(2 * pairs)),
            pl.BlockSpec(blk, cur(2 * pairs)),
        ],
        out_specs=pl.BlockSpec(blk, cur(0)),
        out_shape=jax.ShapeDtypeStruct((bsz, s, B_HEADS * HEAD_DIM), bf16),
        scratch_shapes=[pltpu.VMEM((nbr, span, LANES), f32), pltpu.VMEM((nbr, span, LANES), f32),
                        pltpu.VMEM((nbr, span, LANES), f32)],
        compiler_params=_params("parallel", "parallel", "parallel"),
        name="dilated_attn",
    )(slopes, pb, pb, pb, pb, pb)


def _gla_kernel(q_ref, k_ref, v_ref, r_ref, la_ref, hg_ref, o_ref, state_sc, b_sc, q_sc, *, chunk):
    tb = q_ref.shape[1]

    @pl.when(pl.program_id(2) == 0)
    def _():
        state_sc[...] = jnp.zeros_like(state_sc)

    row = lax.broadcasted_iota(jnp.int32, (chunk, chunk), 0)
    col = lax.broadcasted_iota(jnp.int32, (chunk, chunk), 1)
    causal = row >= col
    tril = jnp.where(causal, 1.0, 0.0).astype(bf16)
    steep = jnp.min(la_ref[0]) * chunk < -80.0

    def intra_exact(b, q, k):
        b_sc[...] = b
        q_sc[...] = q

        def one_row(i, at):
            bi = b_sc[pl.ds(i, 1), :]
            qi = q_sc[pl.ds(i, 1), :]
            g = jnp.exp(jnp.minimum(bi - b, 0.0)) * k * qi
            colv = jnp.sum(g, axis=-1, keepdims=True)
            return jnp.where(col == i, colv, at)

        return lax.fori_loop(0, chunk, one_row, jnp.zeros((chunk, chunk), f32)).T

    def step(exact):
        sls = [pl.ds(ci * chunk, chunk) for ci in range(tb // chunk)]
        bs = []
        for sl in sls:
            la = la_ref[0, sl, :]
            hi = la.astype(bf16)
            lo = (la - hi.astype(f32)).astype(bf16)
            bs.append(jnp.dot(tril, hi, preferred_element_type=f32)
                      + jnp.dot(tril, lo, preferred_element_type=f32))
        qs = [q_ref[0, sl, :].astype(f32) * (C_DK ** -0.5) for sl in sls]
        ks = [k_ref[0, sl, :].astype(f32) for sl in sls]
        qts = [(q * jnp.exp(b)).astype(bf16) for q, b in zip(qs, bs)]
        lasts = [b[chunk - 1:chunk, :] for b in bs]
        kds = [(k * jnp.exp(bl - b)).astype(bf16) for k, b, bl in zip(ks, bs, lasts)]
        if exact:
            ats = [intra_exact(b, q, k) for b, q, k in zip(bs, qs, ks)]
        else:
            kts = [(k * jnp.exp(-b)).astype(bf16) for k, b in zip(ks, bs)]
            ats = [lax.dot_general(qt, kt, NT_DIMS, preferred_element_type=f32) for qt, kt in zip(qts, kts)]
        updates = [lax.dot_general(v_ref[0, sl, :], kd, TN_DIMS, preferred_element_type=f32)
                   for sl, kd in zip(sls, kds)]
        intras = [jnp.dot(jnp.where(causal, a, 0.0).astype(bf16), v_ref[0, sl, :], preferred_element_type=f32)
                  for sl, a in zip(sls, ats)]
        states = [state_sc[...]]
        for bl, update in zip(lasts, updates):
            states.append(states[-1] * jnp.exp(bl) + update)
        state_sc[...] = states[-1]
        inters = [lax.dot_general(qt, st.astype(bf16), NT_DIMS, preferred_element_type=f32)
                  for qt, st in zip(qts, states[:-1])]
        for sl, o_intra, o_inter in zip(sls, intras, inters):
            rr = r_ref[0, sl, :].astype(f32)
            o_ref[0, sl, :] = (_rms(o_intra + o_inter, hg_ref[...]) * (rr * jax.nn.sigmoid(rr))).astype(bf16)

    @pl.when(jnp.logical_not(steep))
    def _():
        step(False)

    @pl.when(steep)
    def _():
        step(True)


def _gla(pqk, pvr, pla, head_g, tb=2048, chunk=128):
    bsz, s, _ = pqk.shape
    return pl.pallas_call(
        functools.partial(_gla_kernel, chunk=chunk),
        grid=(bsz, C_HEADS, s // tb),
        in_specs=[
            pl.BlockSpec((1, tb, C_DK), lambda b, h, i: (b, i, h)),
            pl.BlockSpec((1, tb, C_DK), lambda b, h, i: (b, i, C_HEADS + h)),
            pl.BlockSpec((1, tb, C_DV), lambda b, h, i: (b, i, h)),
            pl.BlockSpec((1, tb, C_DV), lambda b, h, i: (b, i, C_HEADS + h)),
            pl.BlockSpec((1, tb, C_DK), lambda b, h, i: (b, i, h)),
            pl.BlockSpec((1, C_DV), lambda b, h, i: (0, 0)),
        ],
        out_specs=pl.BlockSpec((1, tb, C_DV), lambda b, h, i: (b, i, h)),
        out_shape=jax.ShapeDtypeStruct((bsz, s, C_HEADS * C_DV), bf16),
        scratch_shapes=[pltpu.VMEM((C_DV, C_DK), f32),
                        pltpu.VMEM((chunk, C_DK), f32), pltpu.VMEM((chunk, C_DK), f32)],
        compiler_params=_params("parallel", "parallel", "arbitrary"),
        name="gla",
    )(pqk, pqk, pvr, pvr, pla, head_g)


def _post_kernel(x_ref, ya_ref, yb_ref, wo_ref, gate1_ref, sc2_ref, sh2_ref, gate2_ref, ng_ref,
                 wg_ref, wu_ref, wd_ref, o_ref, x1_sc, hb_sc, *, hc, groups, piece):
    half = ya_ref.shape[-1]
    hidden = wg_ref.shape[1]
    chunks = [(c, min(c + hc, hidden)) for c in range(0, hidden, hc)]
    starts = [sum(groups[:g]) for g in range(len(groups))]

    def pieces(g):
        return range(starts[g], starts[g] + groups[g], piece)

    def out_proj(g):
        rows = pl.ds(starts[g], groups[g])
        x1_sc[rows, :] = (jnp.dot(ya_ref[0, rows, :], wo_ref[0:half, :], preferred_element_type=f32)
                          + jnp.dot(yb_ref[0, rows, :], wo_ref[half:2 * half, :], preferred_element_type=f32))

    def before_ffn(start):
        rows = pl.ds(start, piece)
        x1 = x_ref[0, rows, :] + (1.0 + gate1_ref[0]) * _rms(x1_sc[rows, :], ng_ref[0:1, :])
        x1_sc[rows, :] = x1
        hb_sc[rows, :] = (_rms(x1, ng_ref[1:2, :]) * (1.0 + sc2_ref[0]) + sh2_ref[0]).astype(bf16)

    def after_ffn(start, y2):
        rows = pl.ds(start, piece)
        o_ref[0, rows, :] = x1_sc[rows, :] + (1.0 + gate2_ref[0]) * _rms(y2, ng_ref[2:3, :])

    out_proj(0)
    if len(groups) > 1:
        out_proj(1)
    for p in pieces(0):
        before_ffn(p)
    done = None
    for g in range(len(groups)):
        side = []
        if g + 2 < len(groups):
            side.append(functools.partial(out_proj, g + 2))
        if g + 1 < len(groups):
            side += [functools.partial(before_ffn, p) for p in pieces(g + 1)]
        if done is not None:
            side += [functools.partial(after_ffn, p, done[p - starts[g - 1]:p - starts[g - 1] + piece])
                     for p in pieces(g - 1)]
        hb = hb_sc[pl.ds(starts[g], groups[g]), :]
        acc = jnp.zeros((groups[g], x1_sc.shape[1]), f32)
        for ci, (c, e) in enumerate(chunks):
            gt = jnp.dot(hb, wg_ref[:, c:e], preferred_element_type=f32)
            up = jnp.dot(hb, wu_ref[:, c:e], preferred_element_type=f32)
            act = (gt * jax.nn.sigmoid(gt) * up).astype(bf16)
            acc = acc + jnp.dot(act, wd_ref[c:e, :], preferred_element_type=f32)
            for task in side[ci * len(side) // len(chunks):(ci + 1) * len(side) // len(chunks)]:
                task()
        done = acc
    last = len(groups) - 1
    for p in pieces(last):
        after_ffn(p, done[p - starts[last]:p - starts[last] + piece])


def _post(x, ya, yb, cols, wo, gate1, sc2, sh2, gate2, ng, layer, wg, wu, wd, tm=1024, hc=512):
    bsz, s, d = x.shape
    half = wo.shape[0] // 2
    hidden = wg.shape[-1]
    row = lambda b, i: (b, i, 0)
    per_b = lambda b, i: (b, 0, 0)
    mod = pl.BlockSpec((1, 1, d), per_b)
    slab = lambda shape: pl.BlockSpec((None,) + shape, lambda b, i: (layer, 0, 0), pipeline_mode=pl.Buffered(1))
    return pl.pallas_call(
        functools.partial(_post_kernel, hc=hc, groups=(tm // 2, tm // 2), piece=128),
        grid=(bsz, s // tm),
        in_specs=[
            pl.BlockSpec((1, tm, d), row),
            pl.BlockSpec((1, tm, half), lambda b, i: (b, i, cols[0])),
            pl.BlockSpec((1, tm, half), lambda b, i: (b, i, cols[1])),
            _resident(wo.shape),
            mod, mod, mod, mod,
            _resident(ng.shape),
            slab((d, hidden)), slab((d, hidden)), slab((hidden, d)),
        ],
        out_specs=pl.BlockSpec((1, tm, d), row),
        out_shape=jax.ShapeDtypeStruct((bsz, s, d), f32),
        scratch_shapes=[pltpu.VMEM((tm, d), f32), pltpu.VMEM((tm, d), bf16)],
        compiler_params=_params("parallel", "parallel"),
        name="post",
    )(x, ya, yb, wo, gate1, sc2, sh2, gate2, ng, wg, wu, wd)


def _alibi_slopes(n):
    return 2.0 ** (-8.0 * jnp.arange(1, n + 1, dtype=f32) / n)


def kernel(x, c, ada_w, ada_b, norm_g, ev_w_in, ev_lambda, ev_subln_g, ev_w_out, od_w_in, od_w_g2,
           od_b_g2, od_head_g, od_w_out, ffn_w_gate, ffn_w_up, ffn_w_down):
    depth = ada_w.shape[0]
    mod = _adaln(c, ada_w, ada_b)
    ffn_w = (ffn_w_gate.astype(bf16), ffn_w_up.astype(bf16), ffn_w_down.astype(bf16))
    for l in range(depth):
        sh1, sc1, g1, sh2, sc2, g2 = (mod[l, :, j] for j in range(6))
        pre_g = norm_g[l, 0:1]
        if l % 2 == 0:
            e = l // 2
            lam_init = 0.8 - 0.6 * math.exp(-0.3 * l)
            pa, pb = _inproj_even(x, pre_g, sc1, sh1, ev_w_in[e].astype(bf16))
            ya = _diff_attn(pa, _alibi_slopes(A_HEADS), ev_lambda[e], ev_subln_g[e][:, None], lam_init)
            yb = _dilated_attn(pb, _alibi_slopes(B_HEADS))
            cols = (0, 0)
            wo = ev_w_out[e]
        else:
            o = l // 2
            w = od_w_in[o].astype(bf16)
            nmain = w.shape[1] - C_GATE_RANK
            wg1 = jnp.zeros((w.shape[0], LANES), bf16).at[:, :C_GATE_RANK].set(w[:, nmain:])
            wg2 = jnp.zeros((LANES, od_w_g2.shape[-1]), bf16).at[:C_GATE_RANK].set(od_w_g2[o].astype(bf16))
            pqk, pvr, pla = _inproj_odd(x, pre_g, sc1, sh1, w, wg1, wg2, od_b_g2[o][None, :])
            ya = yb = _gla(pqk, pvr, pla, od_head_g[o][None, :])
            cols = (0, 1)
            wo = od_w_out[o]
        x = _post(x, ya, yb, cols, wo.astype(bf16), g1, sc2, sh2, g2, norm_g[l, 1:4], l, *ffn_w)
    return x
```

```python
import functools
import math

import jax
import jax.numpy as jnp
from jax import lax
from jax.experimental import pallas as pl
from jax.experimental.pallas import tpu as pltpu

f32 = jnp.float32
bf16 = jnp.bfloat16

EPS = 1e-6
LANES = 128
SUBLANES = 8
HEAD_DIM = 64
A_HEADS = 4
B_HEADS = 8
DILATIONS = (1, 4, 16)
BAND = 128
C_HEADS = 4
C_DK = 128
C_DV = 256
C_GATE_RANK = 16
C_TAU = 16.0
VMEM_LIMIT = 56 * 1024 * 1024
LOG2E = math.log2(math.e)
QUERY_CHUNK = 256
ONES_ROWS = 16

NT_DIMS = (((1,), (1,)), ((), ()))
TN_DIMS = (((0,), (0,)), ((), ()))


def _rms(x, g):
    return x * lax.rsqrt(jnp.mean(x * x, axis=-1, keepdims=True) + EPS) * g


def _params(*sem):
    return pltpu.CompilerParams(dimension_semantics=sem, vmem_limit_bytes=VMEM_LIMIT)


def _resident(shape):
    nd = len(shape)
    return pl.BlockSpec(shape, lambda *_: (0,) * nd, pipeline_mode=pl.Buffered(1))


def _adaln_kernel(c_ref, w_ref, b_ref, o_ref):
    c = c_ref[...]
    a = (c * jax.nn.sigmoid(c)).astype(bf16)
    o_ref[0] = jnp.dot(a, w_ref[0].astype(bf16), preferred_element_type=f32) + b_ref[0]


def _adaln(c, ada_w, ada_b):
    depth, d, n = ada_w.shape
    bsz = c.shape[0]
    rows = SUBLANES
    cp = jnp.zeros((rows, d), f32).at[:bsz].set(c)
    tn = n // 2
    out = pl.pallas_call(
        _adaln_kernel,
        grid=(depth, n // tn),
        in_specs=[
            pl.BlockSpec((rows, d), lambda l, j: (0, 0)),
            pl.BlockSpec((1, d, tn), lambda l, j: (l, 0, j)),
            pl.BlockSpec((1, 1, tn), lambda l, j: (l, 0, j)),
        ],
        out_specs=pl.BlockSpec((1, rows, tn), lambda l, j: (l, 0, j)),
        out_shape=jax.ShapeDtypeStruct((depth, rows, n), f32),
        compiler_params=_params("parallel", "parallel"),
        name="adaln",
    )(cp, ada_w, ada_b.reshape(depth, 1, n))
    return out[:, :bsz].reshape(depth, bsz, 6, 1, d)


def _modulated(x_ref, g_ref, sc_ref, sh_ref):
    h = _rms(x_ref[0], g_ref[...])
    return (h * (1.0 + sc_ref[0]) + sh_ref[0]).astype(bf16)


def _inproj_even_kernel(x_ref, g_ref, sc_ref, sh_ref, w_ref, oa_ref, ob_ref, *, nc):
    hb = _modulated(x_ref, g_ref, sc_ref, sh_ref)
    na = oa_ref.shape[-1]
    for c in range(0, na, nc):
        oa_ref[0, :, c:c + nc] = jnp.dot(hb, w_ref[:, c:c + nc], preferred_element_type=f32).astype(bf16)
    for c in range(0, ob_ref.shape[-1], nc):
        ob_ref[0, :, c:c + nc] = jnp.dot(hb, w_ref[:, na + c:na + c + nc], preferred_element_type=f32)


def _inproj_even(x, g, sc, sh, w, tm=1024):
    bsz, s, d = x.shape
    na = 3 * A_HEADS * 2 * HEAD_DIM
    nb = w.shape[1] - na
    row = lambda b, i: (b, i, 0)
    per_b = lambda b, i: (b, 0, 0)
    return pl.pallas_call(
        functools.partial(_inproj_even_kernel, nc=512),
        grid=(bsz, s // tm),
        in_specs=[
            pl.BlockSpec((1, tm, d), row),
            pl.BlockSpec((1, d), lambda b, i: (0, 0)),
            pl.BlockSpec((1, 1, d), per_b),
            pl.BlockSpec((1, 1, d), per_b),
            _resident(w.shape),
        ],
        out_specs=[pl.BlockSpec((1, tm, na), row), pl.BlockSpec((1, tm, nb), row)],
        out_shape=[jax.ShapeDtypeStruct((bsz, s, na), bf16), jax.ShapeDtypeStruct((bsz, s, nb), f32)],
        compiler_params=_params("parallel", "parallel"),
        name="inproj_even",
    )(x, g, sc, sh, w)


def _inproj_odd_kernel(x_ref, g_ref, sc_ref, sh_ref, w_ref, wg1_ref, wg2_ref, bg2_ref,
                       oqk_ref, ovr_ref, ola_ref, *, nc):
    hb = _modulated(x_ref, g_ref, sc_ref, sh_ref)
    glr = jnp.dot(hb, wg1_ref[...], preferred_element_type=f32)
    z = jnp.dot(glr.astype(bf16), wg2_ref[...], preferred_element_type=f32) + bg2_ref[...]
    ola_ref[0] = (jnp.minimum(z, 0.0) - jnp.log(1.0 + jnp.exp(-jnp.abs(z)))) * (1.0 / C_TAU)
    nqk = oqk_ref.shape[-1]
    for c in range(0, nqk, nc):
        oqk_ref[0, :, c:c + nc] = jnp.dot(hb, w_ref[:, c:c + nc], preferred_element_type=f32).astype(bf16)
    for c in range(0, ovr_ref.shape[-1], nc):
        ovr_ref[0, :, c:c + nc] = jnp.dot(hb, w_ref[:, nqk + c:nqk + c + nc], preferred_element_type=f32).astype(bf16)


def _inproj_odd(x, g, sc, sh, w, wg1, wg2, bg2, tm=1024):
    bsz, s, d = x.shape
    nqk = 2 * C_HEADS * C_DK
    nvr = 2 * C_HEADS * C_DV
    nla = C_HEADS * C_DK
    row = lambda b, i: (b, i, 0)
    per_b = lambda b, i: (b, 0, 0)
    return pl.pallas_call(
        functools.partial(_inproj_odd_kernel, nc=512),
        grid=(bsz, s // tm),
        in_specs=[
            pl.BlockSpec((1, tm, d), row),
            pl.BlockSpec((1, d), lambda b, i: (0, 0)),
            pl.BlockSpec((1, 1, d), per_b),
            pl.BlockSpec((1, 1, d), per_b),
            _resident((d, nqk + nvr)),
            _resident(wg1.shape),
            _resident(wg2.shape),
            _resident(bg2.shape),
        ],
        out_specs=[pl.BlockSpec((1, tm, nqk), row), pl.BlockSpec((1, tm, nvr), row), pl.BlockSpec((1, tm, nla), row)],
        out_shape=[jax.ShapeDtypeStruct((bsz, s, nqk), bf16), jax.ShapeDtypeStruct((bsz, s, nvr), bf16),
                   jax.ShapeDtypeStruct((bsz, s, nla), f32)],
        compiler_params=_params("parallel", "parallel"),
        name="inproj_odd",
    )(x, g, sc, sh, w, wg1, wg2, bg2)


def _diff_attn_kernel(slope_ref, lam_ref, q_ref, qn_ref, k_ref, v_ref, g_ref, o_ref,
                      ks_sc, vt_sc, bias_sc, sta_sc, stb_sc, cma_sc, cmb_sc, m_sc, acc_sc, *, t, lam_init):
    h = pl.program_id(1)
    i = pl.program_id(2)
    nblk = k_ref.shape[1] // t
    width = 2 * HEAD_DIM
    slope2 = slope_ref[h] * LOG2E

    @pl.when(i == 0)
    def _():
        lane = lax.broadcasted_iota(jnp.int32, (t, LANES), 1)
        first_half = lane < HEAD_DIM
        ones_row = jnp.where(lax.broadcasted_iota(jnp.int32, (ONES_ROWS, t), 0) == 0, 1.0, 0.0).astype(bf16)
        for c in range(nblk):
            kb = k_ref[0, c * t:(c + 1) * t, :]
            zero = jnp.zeros_like(kb)
            ks_sc[c, 0:t, :] = jnp.where(first_half, kb, zero)
            ks_sc[c, t:2 * t, :] = jnp.where(first_half, zero, kb)
            vt_sc[c, 0:width, :] = v_ref[0, c * t:(c + 1) * t, :].astype(f32).T.astype(bf16)
            vt_sc[c, width:width + ONES_ROWS, :] = ones_row
        kr = lax.broadcasted_iota(jnp.int32, (t, 2 * t), 0)
        qc = lax.broadcasted_iota(jnp.int32, (t, 2 * t), 1) % t
        bias_sc[...] = slope2 * (qc - kr).astype(f32)

    def transposed_queries(ref):
        return (ref[0].astype(f32) * (HEAD_DIM ** -0.5 * LOG2E)).T.astype(bf16)

    qt = transposed_queries(q_ref)
    lane = lax.broadcasted_iota(jnp.int32, (1, 2 * t), 1)
    one_block_further = jnp.where(lane >= t, slope2 * t, 0.0)

    def scores_into(st_ref, cm_ref, j, q0, q1, qt=qt):
        st = jnp.dot(ks_sc[j], qt[:, q0:q1], preferred_element_type=f32)
        for c in range(2):
            s = st[c * t:(c + 1) * t] - bias_sc[:, q0:q1]
            st_ref[c * t:(c + 1) * t, q0:q1] = s
            cm_ref[c, :, q0:q1] = jnp.max(s, axis=0, keepdims=True)

    def softmax_pv(st_ref, cm_ref, j, q0, q1, diagonal_from):
        vt = vt_sc[j]
        shift = slope2 * ((2 * i - j) * t).astype(f32) + one_block_further[:, q0:q1]
        for c in range(2):
            s = st_ref[c * t:(c + 1) * t, q0:q1]
            if diagonal_from is not None:
                kr = lax.broadcasted_iota(jnp.int32, s.shape, 0)
                qc = lax.broadcasted_iota(jnp.int32, s.shape, 1) + (q0 - diagonal_from)
                s = jnp.where(jnp.logical_or(qc >= t, kr <= qc), s, -jnp.inf)
                cmax = jnp.max(s, axis=0, keepdims=True)
            else:
                cmax = cm_ref[c, :, q0:q1]
            m_old = m_sc[c, :, q0:q1]
            m_new = jnp.maximum(m_old, cmax - shift)
            p = jnp.exp2(s - (m_new + shift)).astype(bf16)
            pv = jnp.dot(vt, p, preferred_element_type=f32)
            acc_sc[c, :, q0:q1] = jnp.exp2(m_old - m_new) * acc_sc[c, :, q0:q1] + pv
            m_sc[c, :, q0:q1] = m_new

    def overlapped(nxt, cur):
        for q0 in range(0, 2 * t, QUERY_CHUNK):
            if nxt is not None and q0 >= nxt[3]:
                scores_into(nxt[0], nxt[1], nxt[2], q0, q0 + QUERY_CHUNK, *nxt[4:])
            if cur is not None and q0 >= cur[3]:
                softmax_pv(cur[0], cur[1], cur[2], q0, q0 + QUERY_CHUNK, cur[4])

    m_sc[...] = jnp.full(m_sc.shape, -jnp.inf, f32)
    acc_sc[...] = jnp.zeros(acc_sc.shape, f32)

    @pl.when(i == 0)
    def _():
        overlapped((sta_sc, cma_sc, 0, 0), None)

    def body(jj, carry):
        j = 2 * jj
        overlapped((stb_sc, cmb_sc, j + 1, 0), (sta_sc, cma_sc, j, 0, None))
        overlapped((sta_sc, cma_sc, j + 2, 0), (stb_sc, cmb_sc, j + 1, 0, None))
        return carry

    lax.fori_loop(0, i, body, 0)

    overlapped((stb_sc, cmb_sc, 2 * i + 1, t), (sta_sc, cma_sc, 2 * i, 0, 0))
    overlapped((sta_sc, cma_sc, 0, 0, transposed_queries(qn_ref)), (stb_sc, cmb_sc, 2 * i + 1, t, t))

    lp = lam_ref[...]
    lam = (jnp.exp(jnp.sum(lp[0:1] * lp[1:2], axis=-1, keepdims=True))
           - jnp.exp(jnp.sum(lp[2:3] * lp[3:4], axis=-1, keepdims=True)) + lam_init)
    outs = [acc_sc[c, 0:width, :] / acc_sc[c, width:width + 1, :] for c in range(2)]
    ot = outs[0] - lam * outs[1]
    ot = ot * lax.rsqrt(jnp.mean(ot * ot, axis=0, keepdims=True) + EPS) * (g_ref[...] * (1.0 - lam_init))
    o_ref[0] = ot.T.astype(bf16)


def _diff_attn(pa, slopes, lam_p, subln_g, lam_init, t=512):
    bsz, s, _ = pa.shape
    width = 2 * HEAD_DIM
    nblk = s // t
    return pl.pallas_call(
        functools.partial(_diff_attn_kernel, t=t, lam_init=lam_init),
        grid=(bsz, A_HEADS, nblk // 2),
        in_specs=[
            pl.BlockSpec(memory_space=pltpu.SMEM),
            pl.BlockSpec(lam_p.shape, lambda b, h, i: (0, 0)),
            pl.BlockSpec((1, 2 * t, width), lambda b, h, i: (b, i, h)),
            pl.BlockSpec((1, 2 * t, width), lambda b, h, i: (b, jnp.minimum(i + 1, nblk // 2 - 1), h)),
            pl.BlockSpec((1, s, width), lambda b, h, i: (b, 0, A_HEADS + h)),
            pl.BlockSpec((1, s, width), lambda b, h, i: (b, 0, 2 * A_HEADS + h)),
            pl.BlockSpec((width, 1), lambda b, h, i: (0, 0)),
        ],
        out_specs=pl.BlockSpec((1, 2 * t, width), lambda b, h, i: (b, i, h)),
        out_shape=jax.ShapeDtypeStruct((bsz, s, A_HEADS * width), bf16),
        scratch_shapes=[pltpu.VMEM((nblk, 2 * t, width), bf16),
                        pltpu.VMEM((nblk, width + ONES_ROWS, t), bf16),
                        pltpu.VMEM((t, 2 * t), f32),
                        pltpu.VMEM((2 * t, 2 * t), f32),
                        pltpu.VMEM((2 * t, 2 * t), f32),
                        pltpu.VMEM((2, 1, 2 * t), f32),
                        pltpu.VMEM((2, 1, 2 * t), f32),
                        pltpu.VMEM((2, 1, 2 * t), f32),
                        pltpu.VMEM((2, width + ONES_ROWS, 2 * t), f32)],
        compiler_params=_params("parallel", "parallel", "arbitrary"),
        name="diff_attn",
    )(slopes, lam_p, pa, pa, pa, pa, subln_g)


def _dilated_kernel(slope_ref, q_ref, kp_ref, kc_ref, vp_ref, vc_ref, o_ref,
                    acc_sc, m_sc, l_sc, *, span):
    hp = pl.program_id(1)
    sidx = pl.program_id(2)
    blk = BAND
    nblocks = span // blk

    lane = lax.broadcasted_iota(jnp.int32, (blk, LANES), 1)
    head_a = lane < HEAD_DIM
    a = lax.broadcasted_iota(jnp.int32, (blk, 2 * blk), 0)
    bi = lax.broadcasted_iota(jnp.int32, (blk, 2 * blk), 1)
    rel = blk + a - bi
    valid = (rel >= 0) & (rel <= BAND)
    before_start = bi < blk
    steps = jnp.where(valid, rel.astype(f32), jnp.inf)
    slopes = (slope_ref[2 * hp], slope_ref[2 * hp + 1])
    for br, d in enumerate(DILATIONS):
        per_stream = nblocks // d
        rates = [sl * (LOG2E * d) for sl in slopes]

        def one_block(r, n, br=br, d=d, rates=rates):
            q_start = n * (blk * d) + r
            qb = (q_ref[0, pl.ds(q_start, blk, stride=d), :] * (HEAD_DIM ** -0.5 * LOG2E)).astype(bf16)
            if n == 0:
                tail = pl.ds(span - blk * d + r, blk, stride=d)
                own = pl.ds(r, blk, stride=d)
                kk = jnp.concatenate([kp_ref[0, tail, :], kc_ref[0, own, :]], axis=0).astype(bf16)
                vv = jnp.concatenate([vp_ref[0, tail, :], vc_ref[0, own, :]], axis=0).astype(bf16)
            else:
                both = pl.ds((n - 1) * (blk * d) + r, 2 * blk, stride=d)
                kk = kc_ref[0, both, :].astype(bf16)
                vv = vc_ref[0, both, :].astype(bf16)
            zero = jnp.zeros_like(qb)
            qh = (jnp.where(head_a, qb, zero), jnp.where(head_a, zero, qb))
            outs, ms, ls = [], [], []
            for c in range(2):
                s = lax.dot_general(qh[c], kk, NT_DIMS, preferred_element_type=f32) - rates[c] * steps
                if n == 0:
                    s = jnp.where(jnp.logical_and(sidx == 0, before_start), -jnp.inf, s)
                m = jnp.max(s, axis=-1, keepdims=True)
                p = jnp.exp2(s - m)
                ls.append(jnp.sum(p, axis=-1, keepdims=True))
                ms.append(m)
                outs.append(jnp.dot(p.astype(bf16), vv, preferred_element_type=f32))
            dst = pl.ds(q_start, blk, stride=d)
            acc_sc[br, dst, :] = jnp.where(head_a, outs[0], outs[1])
            m_sc[br, dst, :] = jnp.where(head_a, ms[0], ms[1])
            l_sc[br, dst, :] = jnp.where(head_a, ls[0], ls[1])

        for r in range(d):
            for n in range(per_stream):
                one_block(r, n)

    m_all = jnp.maximum(jnp.maximum(m_sc[0], m_sc[1]), m_sc[2])
    num = jnp.zeros((span, LANES), f32)
    den = jnp.zeros((span, LANES), f32)
    for br in range(len(DILATIONS)):
        w = jnp.exp2(m_sc[br] - m_all)
        num = num + w * acc_sc[br]
        den = den + w * l_sc[br]
    o_ref[0] = (num / den).astype(bf16)


def _dilated_attn(pb, slopes, span=2048):
    bsz, s, _ = pb.shape
    pairs = B_HEADS // 2
    cur = lambda off: (lambda b, h, i: (b, i, off + h))
    prev = lambda off: (lambda b, h, i: (b, jnp.maximum(i - 1, 0), off + h))
    blk = (1, span, LANES)
    nbr = len(DILATIONS)
    return pl.pallas_call(
        functools.partial(_dilated_kernel, span=span),
        grid=(bsz, pairs, s // span),
        in_specs=[
            pl.BlockSpec(memory_space=pltpu.SMEM),
            pl.BlockSpec(blk, cur(0)),
            pl.BlockSpec(blk, prev(pairs)),
            pl.BlockSpec(blk, cur(pairs)),
            pl.BlockSpec(blk, prev(2 * pairs)),
            pl.BlockSpec(blk, cur(2 * pairs)),
        ],
        out_specs=pl.BlockSpec(blk, cur(0)),
        out_shape=jax.ShapeDtypeStruct((bsz, s, B_HEADS * HEAD_DIM), bf16),
        scratch_shapes=[pltpu.VMEM((nbr, span, LANES), f32), pltpu.VMEM((nbr, span, LANES), f32),
                        pltpu.VMEM((nbr, span, LANES), f32)],
        compiler_params=_params("parallel", "parallel", "parallel"),
        name="dilated_attn",
    )(slopes, pb, pb, pb, pb, pb)


def _gla_kernel(q_ref, k_ref, v_ref, r_ref, la_ref, hg_ref, o_ref, state_sc, b_sc, q_sc, *, chunk):
    tb = q_ref.shape[1]

    @pl.when(pl.program_id(2) == 0)
    def _():
        state_sc[...] = jnp.zeros_like(state_sc)

    row = lax.broadcasted_iota(jnp.int32, (chunk, chunk), 0)
    col = lax.broadcasted_iota(jnp.int32, (chunk, chunk), 1)
    causal = row >= col
    tril = jnp.where(causal, 1.0, 0.0).astype(bf16)
    steep = jnp.min(la_ref[0]) * chunk < -80.0

    def intra_exact(b, q, k):
        b_sc[...] = b
        q_sc[...] = q

        def one_row(i, at):
            bi = b_sc[pl.ds(i, 1), :]
            qi = q_sc[pl.ds(i, 1), :]
            g = jnp.exp(jnp.minimum(bi - b, 0.0)) * k * qi
            colv = jnp.sum(g, axis=-1, keepdims=True)
            return jnp.where(col == i, colv, at)

        return lax.fori_loop(0, chunk, one_row, jnp.zeros((chunk, chunk), f32)).T

    def step(exact):
        sls = [pl.ds(ci * chunk, chunk) for ci in range(tb // chunk)]
        bs = []
        for sl in sls:
            la = la_ref[0, sl, :]
            hi = la.astype(bf16)
            lo = (la - hi.astype(f32)).astype(bf16)
            bs.append(jnp.dot(tril, hi, preferred_element_type=f32)
                      + jnp.dot(tril, lo, preferred_element_type=f32))
        qs = [q_ref[0, sl, :].astype(f32) * (C_DK ** -0.5) for sl in sls]
        ks = [k_ref[0, sl, :].astype(f32) for sl in sls]
        qts = [(q * jnp.exp(b)).astype(bf16) for q, b in zip(qs, bs)]
        lasts = [b[chunk - 1:chunk, :] for b in bs]
        kds = [(k * jnp.exp(bl - b)).astype(bf16) for k, b, bl in zip(ks, bs, lasts)]
        if exact:
            ats = [intra_exact(b, q, k) for b, q, k in zip(bs, qs, ks)]
        else:
            kts = [(k * jnp.exp(-b)).astype(bf16) for k, b in zip(ks, bs)]
            ats = [lax.dot_general(qt, kt, NT_DIMS, preferred_element_type=f32) for qt, kt in zip(qts, kts)]
        updates = [lax.dot_general(v_ref[0, sl, :], kd, TN_DIMS, preferred_element_type=f32)
                   for sl, kd in zip(sls, kds)]
        intras = [jnp.dot(jnp.where(causal, a, 0.0).astype(bf16), v_ref[0, sl, :], preferred_element_type=f32)
                  for sl, a in zip(sls, ats)]
        states = [state_sc[...]]
        for bl, update in zip(lasts, updates):
            states.append(states[-1] * jnp.exp(bl) + update)
        state_sc[...] = states[-1]
        inters = [lax.dot_general(qt, st.astype(bf16), NT_DIMS, preferred_element_type=f32)
                  for qt, st in zip(qts, states[:-1])]
        for sl, o_intra, o_inter in zip(sls, intras, inters):
            rr = r_ref[0, sl, :].astype(f32)
            o_ref[0, sl, :] = (_rms(o_intra + o_inter, hg_ref[...]) * (rr * jax.nn.sigmoid(rr))).astype(bf16)

    @pl.when(jnp.logical_not(steep))
    def _():
        step(False)

    @pl.when(steep)
    def _():
        step(True)


def _gla(pqk, pvr, pla, head_g, tb=4096, chunk=128):
    bsz, s, _ = pqk.shape
    return pl.pallas_call(
        functools.partial(_gla_kernel, chunk=chunk),
        grid=(bsz, C_HEADS, s // tb),
        in_specs=[
            pl.BlockSpec((1, tb, C_DK), lambda b, h, i: (b, i, h)),
            pl.BlockSpec((1, tb, C_DK), lambda b, h, i: (b, i, C_HEADS + h)),
            pl.BlockSpec((1, tb, C_DV), lambda b, h, i: (b, i, h)),
            pl.BlockSpec((1, tb, C_DV), lambda b, h, i: (b, i, C_HEADS + h)),
            pl.BlockSpec((1, tb, C_DK), lambda b, h, i: (b, i, h)),
            pl.BlockSpec((1, C_DV), lambda b, h, i: (0, 0)),
        ],
        out_specs=pl.BlockSpec((1, tb, C_DV), lambda b, h, i: (b, i, h)),
        out_shape=jax.ShapeDtypeStruct((bsz, s, C_HEADS * C_DV), bf16),
        scratch_shapes=[pltpu.VMEM((C_DV, C_DK), f32),
                        pltpu.VMEM((chunk, C_DK), f32), pltpu.VMEM((chunk, C_DK), f32)],
        compiler_params=_params("parallel", "parallel", "arbitrary"),
        name="gla",
    )(pqk, pqk, pvr, pvr, pla, head_g)


def _post_kernel(x_ref, ya_ref, yb_ref, wo_ref, gate1_ref, sc2_ref, sh2_ref, gate2_ref, ng_ref,
                 wg_ref, wu_ref, wd_ref, o_ref, x1_sc, hb_sc, *, hc, groups, piece):
    half = ya_ref.shape[-1]
    hidden = wg_ref.shape[1]
    chunks = [(c, min(c + hc, hidden)) for c in range(0, hidden, hc)]
    starts = [sum(groups[:g]) for g in range(len(groups))]

    def pieces(g):
        return range(starts[g], starts[g] + groups[g], piece)

    def out_proj(g):
        rows = pl.ds(starts[g], groups[g])
        x1_sc[rows, :] = (jnp.dot(ya_ref[0, rows, :], wo_ref[0:half, :], preferred_element_type=f32)
                          + jnp.dot(yb_ref[0, rows, :], wo_ref[half:2 * half, :], preferred_element_type=f32))

    def before_ffn(start):
        rows = pl.ds(start, piece)
        x1 = x_ref[0, rows, :] + (1.0 + gate1_ref[0]) * _rms(x1_sc[rows, :], ng_ref[0:1, :])
        x1_sc[rows, :] = x1
        hb_sc[rows, :] = (_rms(x1, ng_ref[1:2, :]) * (1.0 + sc2_ref[0]) + sh2_ref[0]).astype(bf16)

    def after_ffn(start, y2):
        rows = pl.ds(start, piece)
        o_ref[0, rows, :] = x1_sc[rows, :] + (1.0 + gate2_ref[0]) * _rms(y2, ng_ref[2:3, :])

    out_proj(0)
    if len(groups) > 1:
        out_proj(1)
    for p in pieces(0):
        before_ffn(p)
    done = None
    for g in range(len(groups)):
        side = []
        if g + 2 < len(groups):
            side.append(functools.partial(out_proj, g + 2))
        if g + 1 < len(groups):
            side += [functools.partial(before_ffn, p) for p in pieces(g + 1)]
        if done is not None:
            side += [functools.partial(after_ffn, p, done[p - starts[g - 1]:p - starts[g - 1] + piece])
                     for p in pieces(g - 1)]
        hb = hb_sc[pl.ds(starts[g], groups[g]), :]
        acc = jnp.zeros((groups[g], x1_sc.shape[1]), f32)
        for ci, (c, e) in enumerate(chunks):
            gt = jnp.dot(hb, wg_ref[:, c:e], preferred_element_type=f32)
            up = jnp.dot(hb, wu_ref[:, c:e], preferred_element_type=f32)
            act = (gt * jax.nn.sigmoid(gt) * up).astype(bf16)
            acc = acc + jnp.dot(act, wd_ref[c:e, :], preferred_element_type=f32)
            for task in side[ci * len(side) // len(chunks):(ci + 1) * len(side) // len(chunks)]:
                task()
        done = acc
    last = len(groups) - 1
    for p in pieces(last):
        after_ffn(p, done[p - starts[last]:p - starts[last] + piece])


def _post(x, ya, yb, cols, wo, gate1, sc2, sh2, gate2, ng, layer, wg, wu, wd, tm=1024, hc=512):
    bsz, s, d = x.shape
    half = wo.shape[0] // 2
    hidden = wg.shape[-1]
    row = lambda b, i: (b, i, 0)
    per_b = lambda b, i: (b, 0, 0)
    mod = pl.BlockSpec((1, 1, d), per_b)
    slab = lambda shape: pl.BlockSpec((None,) + shape, lambda b, i: (layer, 0, 0), pipeline_mode=pl.Buffered(1))
    return pl.pallas_call(
        functools.partial(_post_kernel, hc=hc, groups=(tm // 2, tm // 2), piece=128),
        grid=(bsz, s // tm),
        in_specs=[
            pl.BlockSpec((1, tm, d), row),
            pl.BlockSpec((1, tm, half), lambda b, i: (b, i, cols[0])),
            pl.BlockSpec((1, tm, half), lambda b, i: (b, i, cols[1])),
            _resident(wo.shape),
            mod, mod, mod, mod,
            _resident(ng.shape),
            slab((d, hidden)), slab((d, hidden)), slab((hidden, d)),
        ],
        out_specs=pl.BlockSpec((1, tm, d), row),
        out_shape=jax.ShapeDtypeStruct((bsz, s, d), f32),
        scratch_shapes=[pltpu.VMEM((tm, d), f32), pltpu.VMEM((tm, d), bf16)],
        compiler_params=_params("parallel", "parallel"),
        name="post",
    )(x, ya, yb, wo, gate1, sc2, sh2, gate2, ng, wg, wu, wd)


def _alibi_slopes(n):
    return 2.0 ** (-8.0 * jnp.arange(1, n + 1, dtype=f32) / n)


def kernel(x, c, ada_w, ada_b, norm_g, ev_w_in, ev_lambda, ev_subln_g, ev_w_out, od_w_in, od_w_g2,
           od_b_g2, od_head_g, od_w_out, ffn_w_gate, ffn_w_up, ffn_w_down):
    depth = ada_w.shape[0]
    mod = _adaln(c, ada_w, ada_b)
    ffn_w = (ffn_w_gate.astype(bf16), ffn_w_up.astype(bf16), ffn_w_down.astype(bf16))
    for l in range(depth):
        sh1, sc1, g1, sh2, sc2, g2 = (mod[l, :, j] for j in range(6))
        pre_g = norm_g[l, 0:1]
        if l % 2 == 0:
            e = l // 2
            lam_init = 0.8 - 0.6 * math.exp(-0.3 * l)
            pa, pb = _inproj_even(x, pre_g, sc1, sh1, ev_w_in[e].astype(bf16))
            ya = _diff_attn(pa, _alibi_slopes(A_HEADS), ev_lambda[e], ev_subln_g[e][:, None], lam_init)
            yb = _dilated_attn(pb, _alibi_slopes(B_HEADS))
            cols = (0, 0)
            wo = ev_w_out[e]
        else:
            o = l // 2
            w = od_w_in[o].astype(bf16)
            nmain = w.shape[1] - C_GATE_RANK
            wg1 = jnp.zeros((w.shape[0], LANES), bf16).at[:, :C_GATE_RANK].set(w[:, nmain:])
            wg2 = jnp.zeros((LANES, od_w_g2.shape[-1]), bf16).at[:C_GATE_RANK].set(od_w_g2[o].astype(bf16))
            pqk, pvr, pla = _inproj_odd(x, pre_g, sc1, sh1, w, wg1, wg2, od_b_g2[o][None, :])
            ya = yb = _gla(pqk, pvr, pla, od_head_g[o][None, :])
            cols = (0, 1)
            wo = od_w_out[o]
        x = _post(x, ya, yb, cols, wo.astype(bf16), g1, sc2, sh2, g2, norm_g[l, 1:4], l, *ffn_w)
    return x
```

```python
import functools
import math

import jax
import jax.numpy as jnp
from jax import lax
from jax.experimental import pallas as pl
from jax.experimental.pallas import tpu as pltpu

f32 = jnp.float32
bf16 = jnp.bfloat16

EPS = 1e-6
LANES = 128
SUBLANES = 8
HEAD_DIM = 64
A_HEADS = 4
B_HEADS = 8
DILATIONS = (1, 4, 16)
BAND = 128
C_HEADS = 4
C_DK = 128
C_DV = 256
C_GATE_RANK = 16
C_TAU = 16.0
VMEM_LIMIT = 56 * 1024 * 1024
LOG2E = math.log2(math.e)
QUERY_CHUNK = 256
ONES_ROWS = 16

NT_DIMS = (((1,), (1,)), ((), ()))
TN_DIMS = (((0,), (0,)), ((), ()))


def _rms(x, g):
    return x * lax.rsqrt(jnp.mean(x * x, axis=-1, keepdims=True) + EPS) * g


def _params(*sem, fuse_inputs=None):
    return pltpu.CompilerParams(dimension_semantics=sem, vmem_limit_bytes=VMEM_LIMIT,
                                allow_input_fusion=fuse_inputs)


def _resident(shape):
    nd = len(shape)
    return pl.BlockSpec(shape, lambda *_: (0,) * nd, pipeline_mode=pl.Buffered(1))


def _adaln_kernel(c_ref, w_ref, b_ref, o_ref):
    c = c_ref[...]
    a = (c * jax.nn.sigmoid(c)).astype(bf16)
    o_ref[0] = jnp.dot(a, w_ref[0].astype(bf16), preferred_element_type=f32) + b_ref[0]


def _adaln(c, ada_w, ada_b):
    depth, d, n = ada_w.shape
    bsz = c.shape[0]
    rows = SUBLANES
    cp = jnp.zeros((rows, d), f32).at[:bsz].set(c)
    tn = n // 2
    out = pl.pallas_call(
        _adaln_kernel,
        grid=(depth, n // tn),
        in_specs=[
            pl.BlockSpec((rows, d), lambda l, j: (0, 0)),
            pl.BlockSpec((1, d, tn), lambda l, j: (l, 0, j)),
            pl.BlockSpec((1, 1, tn), lambda l, j: (l, 0, j)),
        ],
        out_specs=pl.BlockSpec((1, rows, tn), lambda l, j: (l, 0, j)),
        out_shape=jax.ShapeDtypeStruct((depth, rows, n), f32),
        compiler_params=_params("parallel", "parallel"),
        name="adaln",
    )(cp, ada_w, ada_b.reshape(depth, 1, n))
    return out[:, :bsz].reshape(depth, bsz, 6, 1, d)


def _modulated(x_ref, g_ref, sc_ref, sh_ref):
    h = _rms(x_ref[0], g_ref[...])
    return (h * (1.0 + sc_ref[0]) + sh_ref[0]).astype(bf16)


def _inproj_even_kernel(x_ref, g_ref, sc_ref, sh_ref, w_ref, oa_ref, ob_ref, *, nc):
    hb = _modulated(x_ref, g_ref, sc_ref, sh_ref)
    na = oa_ref.shape[-1]
    for c in range(0, na, nc):
        oa_ref[0, :, c:c + nc] = jnp.dot(hb, w_ref[:, c:c + nc], preferred_element_type=f32).astype(bf16)
    for c in range(0, ob_ref.shape[-1], nc):
        ob_ref[0, :, c:c + nc] = jnp.dot(hb, w_ref[:, na + c:na + c + nc], preferred_element_type=f32)


def _inproj_even(x, g, sc, sh, w, tm=1024):
    bsz, s, d = x.shape
    na = 3 * A_HEADS * 2 * HEAD_DIM
    nb = w.shape[1] - na
    row = lambda b, i: (b, i, 0)
    per_b = lambda b, i: (b, 0, 0)
    return pl.pallas_call(
        functools.partial(_inproj_even_kernel, nc=512),
        grid=(bsz, s // tm),
        in_specs=[
            pl.BlockSpec((1, tm, d), row),
            pl.BlockSpec((1, d), lambda b, i: (0, 0)),
            pl.BlockSpec((1, 1, d), per_b),
            pl.BlockSpec((1, 1, d), per_b),
            _resident(w.shape),
        ],
        out_specs=[pl.BlockSpec((1, tm, na), row), pl.BlockSpec((1, tm, nb), row)],
        out_shape=[jax.ShapeDtypeStruct((bsz, s, na), bf16), jax.ShapeDtypeStruct((bsz, s, nb), f32)],
        compiler_params=_params("parallel", "parallel", fuse_inputs=[False, False, False, False, True]),
        name="inproj_even",
    )(x, g, sc, sh, w)


def _inproj_odd_kernel(x_ref, g_ref, sc_ref, sh_ref, w_ref, wg1_ref, wg2_ref, bg2_ref,
                       oqk_ref, ovr_ref, ola_ref, *, nc):
    hb = _modulated(x_ref, g_ref, sc_ref, sh_ref)
    glr = jnp.dot(hb, wg1_ref[...], preferred_element_type=f32)
    z = jnp.dot(glr.astype(bf16), wg2_ref[...], preferred_element_type=f32) + bg2_ref[...]
    ola_ref[0] = (jnp.minimum(z, 0.0) - jnp.log(1.0 + jnp.exp(-jnp.abs(z)))) * (1.0 / C_TAU)
    nqk = oqk_ref.shape[-1]
    for c in range(0, nqk, nc):
        oqk_ref[0, :, c:c + nc] = jnp.dot(hb, w_ref[:, c:c + nc], preferred_element_type=f32).astype(bf16)
    for c in range(0, ovr_ref.shape[-1], nc):
        ovr_ref[0, :, c:c + nc] = jnp.dot(hb, w_ref[:, nqk + c:nqk + c + nc], preferred_element_type=f32).astype(bf16)


def _inproj_odd(x, g, sc, sh, w, wg1, wg2, bg2, tm=1024):
    bsz, s, d = x.shape
    nqk = 2 * C_HEADS * C_DK
    nvr = 2 * C_HEADS * C_DV
    nla = C_HEADS * C_DK
    row = lambda b, i: (b, i, 0)
    per_b = lambda b, i: (b, 0, 0)
    return pl.pallas_call(
        functools.partial(_inproj_odd_kernel, nc=512),
        grid=(bsz, s // tm),
        in_specs=[
            pl.BlockSpec((1, tm, d), row),
            pl.BlockSpec((1, d), lambda b, i: (0, 0)),
            pl.BlockSpec((1, 1, d), per_b),
            pl.BlockSpec((1, 1, d), per_b),
            _resident((d, nqk + nvr)),
            _resident(wg1.shape),
            _resident(wg2.shape),
            _resident(bg2.shape),
        ],
        out_specs=[pl.BlockSpec((1, tm, nqk), row), pl.BlockSpec((1, tm, nvr), row), pl.BlockSpec((1, tm, nla), row)],
        out_shape=[jax.ShapeDtypeStruct((bsz, s, nqk), bf16), jax.ShapeDtypeStruct((bsz, s, nvr), bf16),
                   jax.ShapeDtypeStruct((bsz, s, nla), f32)],
        compiler_params=_params("parallel", "parallel",
                                fuse_inputs=[False, False, False, False, True, False, False, False]),
        name="inproj_odd",
    )(x, g, sc, sh, w, wg1, wg2, bg2)


def _diff_attn_kernel(slope_ref, lam_ref, q_ref, qn_ref, k_ref, v_ref, g_ref, o_ref,
                      ks_sc, vt_sc, bias_sc, sta_sc, stb_sc, cma_sc, cmb_sc, m_sc, acc_sc, *, t, lam_init):
    h = pl.program_id(1)
    i = pl.program_id(2)
    nblk = k_ref.shape[1] // t
    width = 2 * HEAD_DIM
    slope2 = slope_ref[h] * LOG2E

    @pl.when(i == 0)
    def _():
        lane = lax.broadcasted_iota(jnp.int32, (t, LANES), 1)
        first_half = lane < HEAD_DIM
        ones_row = jnp.where(lax.broadcasted_iota(jnp.int32, (ONES_ROWS, t), 0) == 0, 1.0, 0.0).astype(bf16)
        for c in range(nblk):
            kb = k_ref[0, c * t:(c + 1) * t, :]
            zero = jnp.zeros_like(kb)
            ks_sc[c, 0:t, :] = jnp.where(first_half, kb, zero)
            ks_sc[c, t:2 * t, :] = jnp.where(first_half, zero, kb)
            vt_sc[c, 0:width, :] = v_ref[0, c * t:(c + 1) * t, :].astype(f32).T.astype(bf16)
            vt_sc[c, width:width + ONES_ROWS, :] = ones_row
        kr = lax.broadcasted_iota(jnp.int32, (t, 2 * t), 0)
        qc = lax.broadcasted_iota(jnp.int32, (t, 2 * t), 1) % t
        bias_sc[...] = slope2 * (qc - kr).astype(f32)

    def transposed_queries(ref):
        return (ref[0].astype(f32) * (HEAD_DIM ** -0.5 * LOG2E)).T.astype(bf16)

    qt = transposed_queries(q_ref)
    lane = lax.broadcasted_iota(jnp.int32, (1, 2 * t), 1)
    one_block_further = jnp.where(lane >= t, slope2 * t, 0.0)

    def scores_into(st_ref, cm_ref, j, q0, q1, qt=qt):
        st = jnp.dot(ks_sc[j], qt[:, q0:q1], preferred_element_type=f32)
        for c in range(2):
            s = st[c * t:(c + 1) * t] - bias_sc[:, q0:q1]
            st_ref[c * t:(c + 1) * t, q0:q1] = s
            cm_ref[c, :, q0:q1] = jnp.max(s, axis=0, keepdims=True)

    def softmax_pv(st_ref, cm_ref, j, q0, q1, diagonal_from):
        vt = vt_sc[j]
        shift = slope2 * ((2 * i - j) * t).astype(f32) + one_block_further[:, q0:q1]
        for c in range(2):
            s = st_ref[c * t:(c + 1) * t, q0:q1]
            if diagonal_from is not None:
                kr = lax.broadcasted_iota(jnp.int32, s.shape, 0)
                qc = lax.broadcasted_iota(jnp.int32, s.shape, 1) + (q0 - diagonal_from)
                s = jnp.where(jnp.logical_or(qc >= t, kr <= qc), s, -jnp.inf)
                cmax = jnp.max(s, axis=0, keepdims=True)
            else:
                cmax = cm_ref[c, :, q0:q1]
            m_old = m_sc[c, :, q0:q1]
            m_new = jnp.maximum(m_old, cmax - shift)
            p = jnp.exp2(s - (m_new + shift)).astype(bf16)
            pv = jnp.dot(vt, p, preferred_element_type=f32)
            acc_sc[c, :, q0:q1] = jnp.exp2(m_old - m_new) * acc_sc[c, :, q0:q1] + pv
            m_sc[c, :, q0:q1] = m_new

    def overlapped(nxt, cur):
        for q0 in range(0, 2 * t, QUERY_CHUNK):
            if nxt is not None and q0 >= nxt[3]:
                scores_into(nxt[0], nxt[1], nxt[2], q0, q0 + QUERY_CHUNK, *nxt[4:])
            if cur is not None and q0 >= cur[3]:
                softmax_pv(cur[0], cur[1], cur[2], q0, q0 + QUERY_CHUNK, cur[4])

    m_sc[...] = jnp.full(m_sc.shape, -jnp.inf, f32)
    acc_sc[...] = jnp.zeros(acc_sc.shape, f32)

    @pl.when(i == 0)
    def _():
        overlapped((sta_sc, cma_sc, 0, 0), None)

    def body(jj, carry):
        j = 2 * jj
        overlapped((stb_sc, cmb_sc, j + 1, 0), (sta_sc, cma_sc, j, 0, None))
        overlapped((sta_sc, cma_sc, j + 2, 0), (stb_sc, cmb_sc, j + 1, 0, None))
        return carry

    lax.fori_loop(0, i, body, 0)

    overlapped((stb_sc, cmb_sc, 2 * i + 1, t), (sta_sc, cma_sc, 2 * i, 0, 0))
    overlapped((sta_sc, cma_sc, 0, 0, transposed_queries(qn_ref)), (stb_sc, cmb_sc, 2 * i + 1, t, t))

    lp = lam_ref[...]
    lam = (jnp.exp(jnp.sum(lp[0:1] * lp[1:2], axis=-1, keepdims=True))
           - jnp.exp(jnp.sum(lp[2:3] * lp[3:4], axis=-1, keepdims=True)) + lam_init)
    outs = [acc_sc[c, 0:width, :] / acc_sc[c, width:width + 1, :] for c in range(2)]
    ot = outs[0] - lam * outs[1]
    ot = ot * lax.rsqrt(jnp.mean(ot * ot, axis=0, keepdims=True) + EPS) * (g_ref[...] * (1.0 - lam_init))
    o_ref[0] = ot.T.astype(bf16)


def _diff_attn(pa, slopes, lam_p, subln_g, lam_init, t=512):
    bsz, s, _ = pa.shape
    width = 2 * HEAD_DIM
    nblk = s // t
    return pl.pallas_call(
        functools.partial(_diff_attn_kernel, t=t, lam_init=lam_init),
        grid=(bsz, A_HEADS, nblk // 2),
        in_specs=[
            pl.BlockSpec(memory_space=pltpu.SMEM),
            pl.BlockSpec(lam_p.shape, lambda b, h, i: (0, 0)),
            pl.BlockSpec((1, 2 * t, width), lambda b, h, i: (b, i, h)),
            pl.BlockSpec((1, 2 * t, width), lambda b, h, i: (b, jnp.minimum(i + 1, nblk // 2 - 1), h)),
            pl.BlockSpec((1, s, width), lambda b, h, i: (b, 0, A_HEADS + h)),
            pl.BlockSpec((1, s, width), lambda b, h, i: (b, 0, 2 * A_HEADS + h)),
            pl.BlockSpec((width, 1), lambda b, h, i: (0, 0)),
        ],
        out_specs=pl.BlockSpec((1, 2 * t, width), lambda b, h, i: (b, i, h)),
        out_shape=jax.ShapeDtypeStruct((bsz, s, A_HEADS * width), bf16),
        scratch_shapes=[pltpu.VMEM((nblk, 2 * t, width), bf16),
                        pltpu.VMEM((nblk, width + ONES_ROWS, t), bf16),
                        pltpu.VMEM((t, 2 * t), f32),
                        pltpu.VMEM((2 * t, 2 * t), f32),
                        pltpu.VMEM((2 * t, 2 * t), f32),
                        pltpu.VMEM((2, 1, 2 * t), f32),
                        pltpu.VMEM((2, 1, 2 * t), f32),
                        pltpu.VMEM((2, 1, 2 * t), f32),
                        pltpu.VMEM((2, width + ONES_ROWS, 2 * t), f32)],
        compiler_params=_params("parallel", "parallel", "arbitrary"),
        name="diff_attn",
    )(slopes, lam_p, pa, pa, pa, pa, subln_g)


def _dilated_kernel(slope_ref, q_ref, kp_ref, kc_ref, vp_ref, vc_ref, o_ref,
                    acc_sc, m_sc, l_sc, *, span):
    hp = pl.program_id(1)
    sidx = pl.program_id(2)
    blk = BAND
    nblocks = span // blk

    lane = lax.broadcasted_iota(jnp.int32, (blk, LANES), 1)
    head_a = lane < HEAD_DIM
    a = lax.broadcasted_iota(jnp.int32, (blk, 2 * blk), 0)
    bi = lax.broadcasted_iota(jnp.int32, (blk, 2 * blk), 1)
    rel = blk + a - bi
    valid = (rel >= 0) & (rel <= BAND)
    before_start = bi < blk
    steps = jnp.where(valid, rel.astype(f32), jnp.inf)
    slopes = (slope_ref[2 * hp], slope_ref[2 * hp + 1])
    for br, d in enumerate(DILATIONS):
        per_stream = nblocks // d
        rates = [sl * (LOG2E * d) for sl in slopes]

        def one_block(r, n, br=br, d=d, rates=rates):
            q_start = n * (blk * d) + r
            qb = (q_ref[0, pl.ds(q_start, blk, stride=d), :] * (HEAD_DIM ** -0.5 * LOG2E)).astype(bf16)
            if n == 0:
                tail = pl.ds(span - blk * d + r, blk, stride=d)
                own = pl.ds(r, blk, stride=d)
                kk = jnp.concatenate([kp_ref[0, tail, :], kc_ref[0, own, :]], axis=0).astype(bf16)
                vv = jnp.concatenate([vp_ref[0, tail, :], vc_ref[0, own, :]], axis=0).astype(bf16)
            else:
                both = pl.ds((n - 1) * (blk * d) + r, 2 * blk, stride=d)
                kk = kc_ref[0, both, :].astype(bf16)
                vv = vc_ref[0, both, :].astype(bf16)
            zero = jnp.zeros_like(qb)
            qh = (jnp.where(head_a, qb, zero), jnp.where(head_a, zero, qb))
            outs, ms, ls = [], [], []
            for c in range(2):
                s = lax.dot_general(qh[c], kk, NT_DIMS, preferred_element_type=f32) - rates[c] * steps
                if n == 0:
                    s = jnp.where(jnp.logical_and(sidx == 0, before_start), -jnp.inf, s)
                m = jnp.max(s, axis=-1, keepdims=True)
                p = jnp.exp2(s - m)
                ls.append(jnp.sum(p, axis=-1, keepdims=True))
                ms.append(m)
                outs.append(jnp.dot(p.astype(bf16), vv, preferred_element_type=f32))
            dst = pl.ds(q_start, blk, stride=d)
            acc_sc[br, dst, :] = jnp.where(head_a, outs[0], outs[1])
            m_sc[br, dst, :] = jnp.where(head_a, ms[0], ms[1])
            l_sc[br, dst, :] = jnp.where(head_a, ls[0], ls[1])

        for r in range(d):
            for n in range(per_stream):
                one_block(r, n)

    m_all = jnp.maximum(jnp.maximum(m_sc[0], m_sc[1]), m_sc[2])
    num = jnp.zeros((span, LANES), f32)
    den = jnp.zeros((span, LANES), f32)
    for br in range(len(DILATIONS)):
        w = jnp.exp2(m_sc[br] - m_all)
        num = num + w * acc_sc[br]
        den = den + w * l_sc[br]
    o_ref[0] = (num / den).astype(bf16)


def _dilated_attn(pb, slopes, span=2048):
    bsz, s, _ = pb.shape
    pairs = B_HEADS // 2
    cur = lambda off: (lambda b, h, i: (b, i, off + h))
    prev = lambda off: (lambda b, h, i: (b, jnp.maximum(i - 1, 0), off + h))
    blk = (1, span, LANES)
    nbr = len(DILATIONS)
    return pl.pallas_call(
        functools.partial(_dilated_kernel, span=span),
        grid=(bsz, pairs, s // span),
        in_specs=[
            pl.BlockSpec(memory_space=pltpu.SMEM),
            pl.BlockSpec(blk, cur(0)),
            pl.BlockSpec(blk, prev(pairs)),
            pl.BlockSpec(blk, cur(pairs)),
            pl.BlockSpec(blk, prev(2 * pairs)),
            pl.BlockSpec(blk, cur(2 * pairs)),
        ],
        out_specs=pl.BlockSpec(blk, cur(0)),
        out_shape=jax.ShapeDtypeStruct((bsz, s, B_HEADS * HEAD_DIM), bf16),
        scratch_shapes=[pltpu.VMEM((nbr, span, LANES), f32), pltpu.VMEM((nbr, span, LANES), f32),
                        pltpu.VMEM((nbr, span, LANES), f32)],
        compiler_params=_params("parallel", "parallel", "parallel"),
        name="dilated_attn",
    )(slopes, pb, pb, pb, pb, pb)


def _gla_kernel(q_ref, k_ref, v_ref, r_ref, la_ref, hg_ref, o_ref, state_sc, b_sc, q_sc, *, chunk):
    tb = q_ref.shape[1]

    @pl.when(pl.program_id(2) == 0)
    def _():
        state_sc[...] = jnp.zeros_like(state_sc)

    row = lax.broadcasted_iota(jnp.int32, (chunk, chunk), 0)
    col = lax.broadcasted_iota(jnp.int32, (chunk, chunk), 1)
    causal = row >= col
    tril = jnp.where(causal, 1.0, 0.0).astype(bf16)
    steep = jnp.min(la_ref[0]) * chunk < -80.0

    def intra_exact(b, q, k):
        b_sc[...] = b
        q_sc[...] = q

        def one_row(i, at):
            bi = b_sc[pl.ds(i, 1), :]
            qi = q_sc[pl.ds(i, 1), :]
            g = jnp.exp(jnp.minimum(bi - b, 0.0)) * k * qi
            colv = jnp.sum(g, axis=-1, keepdims=True)
            return jnp.where(col == i, colv, at)

        return lax.fori_loop(0, chunk, one_row, jnp.zeros((chunk, chunk), f32)).T

    def step(exact):
        sls = [pl.ds(ci * chunk, chunk) for ci in range(tb // chunk)]
        bs = []
        for sl in sls:
            la = la_ref[0, sl, :]
            hi = la.astype(bf16)
            lo = (la - hi.astype(f32)).astype(bf16)
            bs.append(jnp.dot(tril, hi, preferred_element_type=f32)
                      + jnp.dot(tril, lo, preferred_element_type=f32))
        qs = [q_ref[0, sl, :].astype(f32) * (C_DK ** -0.5) for sl in sls]
        ks = [k_ref[0, sl, :].astype(f32) for sl in sls]
        qts = [(q * jnp.exp(b)).astype(bf16) for q, b in zip(qs, bs)]
        lasts = [b[chunk - 1:chunk, :] for b in bs]
        kds = [(k * jnp.exp(bl - b)).astype(bf16) for k, b, bl in zip(ks, bs, lasts)]
        if exact:
            ats = [intra_exact(b, q, k) for b, q, k in zip(bs, qs, ks)]
        else:
            kts = [(k * jnp.exp(-b)).astype(bf16) for k, b in zip(ks, bs)]
            ats = [lax.dot_general(qt, kt, NT_DIMS, preferred_element_type=f32) for qt, kt in zip(qts, kts)]
        updates = [lax.dot_general(v_ref[0, sl, :], kd, TN_DIMS, preferred_element_type=f32)
                   for sl, kd in zip(sls, kds)]
        intras = [jnp.dot(jnp.where(causal, a, 0.0).astype(bf16), v_ref[0, sl, :], preferred_element_type=f32)
                  for sl, a in zip(sls, ats)]
        states = [state_sc[...]]
        for bl, update in zip(lasts, updates):
            states.append(states[-1] * jnp.exp(bl) + update)
        state_sc[...] = states[-1]
        inters = [lax.dot_general(qt, st.astype(bf16), NT_DIMS, preferred_element_type=f32)
                  for qt, st in zip(qts, states[:-1])]
        for sl, o_intra, o_inter in zip(sls, intras, inters):
            rr = r_ref[0, sl, :].astype(f32)
            o_ref[0, sl, :] = (_rms(o_intra + o_inter, hg_ref[...]) * (rr * jax.nn.sigmoid(rr))).astype(bf16)

    @pl.when(jnp.logical_not(steep))
    def _():
        step(False)

    @pl.when(steep)
    def _():
        step(True)


def _gla(pqk, pvr, pla, head_g, tb=4096, chunk=128):
    bsz, s, _ = pqk.shape
    return pl.pallas_call(
        functools.partial(_gla_kernel, chunk=chunk),
        grid=(bsz, C_HEADS, s // tb),
        in_specs=[
            pl.BlockSpec((1, tb, C_DK), lambda b, h, i: (b, i, h)),
            pl.BlockSpec((1, tb, C_DK), lambda b, h, i: (b, i, C_HEADS + h)),
            pl.BlockSpec((1, tb, C_DV), lambda b, h, i: (b, i, h)),
            pl.BlockSpec((1, tb, C_DV), lambda b, h, i: (b, i, C_HEADS + h)),
            pl.BlockSpec((1, tb, C_DK), lambda b, h, i: (b, i, h)),
            pl.BlockSpec((1, C_DV), lambda b, h, i: (0, 0)),
        ],
        out_specs=pl.BlockSpec((1, tb, C_DV), lambda b, h, i: (b, i, h)),
        out_shape=jax.ShapeDtypeStruct((bsz, s, C_HEADS * C_DV), bf16),
        scratch_shapes=[pltpu.VMEM((C_DV, C_DK), f32),
                        pltpu.VMEM((chunk, C_DK), f32), pltpu.VMEM((chunk, C_DK), f32)],
        compiler_params=_params("parallel", "parallel", "arbitrary"),
        name="gla",
    )(pqk, pqk, pvr, pvr, pla, head_g)


def _post_kernel(x_ref, ya_ref, yb_ref, wo_ref, gate1_ref, sc2_ref, sh2_ref, gate2_ref, ng_ref,
                 wg_ref, wu_ref, wd_ref, o_ref, x1_sc, hb_sc, *, hc, groups, piece):
    half = ya_ref.shape[-1]
    hidden = wg_ref.shape[1]
    chunks = [(c, min(c + hc, hidden)) for c in range(0, hidden, hc)]
    starts = [sum(groups[:g]) for g in range(len(groups))]

    def pieces(g):
        return range(starts[g], starts[g] + groups[g], piece)

    def out_proj(g):
        rows = pl.ds(starts[g], groups[g])
        x1_sc[rows, :] = (jnp.dot(ya_ref[0, rows, :], wo_ref[0:half, :], preferred_element_type=f32)
                          + jnp.dot(yb_ref[0, rows, :], wo_ref[half:2 * half, :], preferred_element_type=f32))

    def before_ffn(start):
        rows = pl.ds(start, piece)
        x1 = x_ref[0, rows, :] + (1.0 + gate1_ref[0]) * _rms(x1_sc[rows, :], ng_ref[0:1, :])
        x1_sc[rows, :] = x1
        hb_sc[rows, :] = (_rms(x1, ng_ref[1:2, :]) * (1.0 + sc2_ref[0]) + sh2_ref[0]).astype(bf16)

    def after_ffn(start, y2):
        rows = pl.ds(start, piece)
        o_ref[0, rows, :] = x1_sc[rows, :] + (1.0 + gate2_ref[0]) * _rms(y2, ng_ref[2:3, :])

    out_proj(0)
    if len(groups) > 1:
        out_proj(1)
    for p in pieces(0):
        before_ffn(p)
    done = None
    for g in range(len(groups)):
        side = []
        if g + 2 < len(groups):
            side.append(functools.partial(out_proj, g + 2))
        if g + 1 < len(groups):
            side += [functools.partial(before_ffn, p) for p in pieces(g + 1)]
        if done is not None:
            side += [functools.partial(after_ffn, p, done[p - starts[g - 1]:p - starts[g - 1] + piece])
                     for p in pieces(g - 1)]
        hb = hb_sc[pl.ds(starts[g], groups[g]), :]
        acc = jnp.zeros((groups[g], x1_sc.shape[1]), f32)
        for ci, (c, e) in enumerate(chunks):
            gt = jnp.dot(hb, wg_ref[:, c:e], preferred_element_type=f32)
            up = jnp.dot(hb, wu_ref[:, c:e], preferred_element_type=f32)
            act = (gt * jax.nn.sigmoid(gt) * up).astype(bf16)
            acc = acc + jnp.dot(act, wd_ref[c:e, :], preferred_element_type=f32)
            for task in side[ci * len(side) // len(chunks):(ci + 1) * len(side) // len(chunks)]:
                task()
        done = acc
    last = len(groups) - 1
    for p in pieces(last):
        after_ffn(p, done[p - starts[last]:p - starts[last] + piece])


def _post(x, ya, yb, cols, wo, gate1, sc2, sh2, gate2, ng, layer, wg, wu, wd, tm=1024, hc=512):
    bsz, s, d = x.shape
    half = wo.shape[0] // 2
    hidden = wg.shape[-1]
    row = lambda b, i: (b, i, 0)
    per_b = lambda b, i: (b, 0, 0)
    mod = pl.BlockSpec((1, 1, d), per_b)
    slab = lambda shape: pl.BlockSpec((None,) + shape, lambda b, i: (layer, 0, 0), pipeline_mode=pl.Buffered(1))
    return pl.pallas_call(
        functools.partial(_post_kernel, hc=hc, groups=(tm // 2, tm // 2), piece=128),
        grid=(bsz, s // tm),
        in_specs=[
            pl.BlockSpec((1, tm, d), row),
            pl.BlockSpec((1, tm, half), lambda b, i: (b, i, cols[0])),
            pl.BlockSpec((1, tm, half), lambda b, i: (b, i, cols[1])),
            _resident(wo.shape),
            mod, mod, mod, mod,
            _resident(ng.shape),
            slab((d, hidden)), slab((d, hidden)), slab((hidden, d)),
        ],
        out_specs=pl.BlockSpec((1, tm, d), row),
        out_shape=jax.ShapeDtypeStruct((bsz, s, d), f32),
        scratch_shapes=[pltpu.VMEM((tm, d), f32), pltpu.VMEM((tm, d), bf16)],
        compiler_params=_params("parallel", "parallel"),
        name="post",
    )(x, ya, yb, wo, gate1, sc2, sh2, gate2, ng, wg, wu, wd)


def _alibi_slopes(n):
    return 2.0 ** (-8.0 * jnp.arange(1, n + 1, dtype=f32) / n)


def kernel(x, c, ada_w, ada_b, norm_g, ev_w_in, ev_lambda, ev_subln_g, ev_w_out, od_w_in, od_w_g2,
           od_b_g2, od_head_g, od_w_out, ffn_w_gate, ffn_w_up, ffn_w_down):
    depth = ada_w.shape[0]
    mod = _adaln(c, ada_w, ada_b)
    ffn_w = (ffn_w_gate.astype(bf16), ffn_w_up.astype(bf16), ffn_w_down.astype(bf16))
    for l in range(depth):
        sh1, sc1, g1, sh2, sc2, g2 = (mod[l, :, j] for j in range(6))
        pre_g = norm_g[l, 0:1]
        if l % 2 == 0:
            e = l // 2
            lam_init = 0.8 - 0.6 * math.exp(-0.3 * l)
            pa, pb = _inproj_even(x, pre_g, sc1, sh1, ev_w_in[e].astype(bf16))
            ya = _diff_attn(pa, _alibi_slopes(A_HEADS), ev_lambda[e], ev_subln_g[e][:, None], lam_init)
            yb = _dilated_attn(pb, _alibi_slopes(B_HEADS))
            cols = (0, 0)
            wo = ev_w_out[e]
        else:
            o = l // 2
            w = od_w_in[o].astype(bf16)
            nmain = w.shape[1] - C_GATE_RANK
            wg1 = jnp.zeros((w.shape[0], LANES), bf16).at[:, :C_GATE_RANK].set(w[:, nmain:])
            wg2 = jnp.zeros((LANES, od_w_g2.shape[-1]), bf16).at[:C_GATE_RANK].set(od_w_g2[o].astype(bf16))
            pqk, pvr, pla = _inproj_odd(x, pre_g, sc1, sh1, w, wg1, wg2, od_b_g2[o][None, :])
            ya = yb = _gla(pqk, pvr, pla, od_head_g[o][None, :])
            cols = (0, 1)
            wo = od_w_out[o]
        x = _post(x, ya, yb, cols, wo.astype(bf16), g1, sc2, sh2, g2, norm_g[l, 1:4], l, *ffn_w)
    return x
```
